```python
import math
import jax
import jax.numpy as jnp
from jax import lax
import numpy as np

D_MODEL = 1024
BATCH = 4
SEQ = 4096
DEPTH = 4
DEC_BATCH = 128
DEC_SEQ = 4
PAST_LEN = 8192
PAGE_SIZE = 128

N_RET_LAYERS = DEPTH // 2
N_ATTN_LAYERS = DEPTH - N_RET_LAYERS

RET_HEADS = 4
RET_DK = D_MODEL // RET_HEADS
RET_DV = 2 * D_MODEL // RET_HEADS
RET_CHUNK = 128

N_HEADS = 16
N_KV_HEADS = 4
HEAD_DIM = 64
GROUP = N_HEADS // N_KV_HEADS
WINDOW = 128

D_FF = ((8 * D_MODEL // 3 + 255) // 256) * 256

ROPE_THETA = 10000.0
EPS = 1e-6
NEG_INF = -1e30

kernel_name = "yoco_retention_swa_sink_decoder_step"


def _rms(x, g):
    xf = x.astype(jnp.float32)
    y = xf * lax.rsqrt(jnp.mean(xf * xf, axis=-1, keepdims=True) + EPS)
    return (y * g.astype(jnp.float32)).astype(x.dtype)


def _rope(x, pos):
    d = x.shape[-1]
    half = d // 2
    inv = 1.0 / (ROPE_THETA ** (jnp.arange(half, dtype=jnp.float32) / half))
    ang = pos.astype(jnp.float32)[:, None] * inv[None, :]
    cos = jnp.cos(ang)[None, :, None, :]
    sin = jnp.sin(ang)[None, :, None, :]
    xf = x.astype(jnp.float32)
    x1, x2 = xf[..., :half], xf[..., half:]
    return jnp.concatenate([x1 * cos - x2 * sin, x2 * cos + x1 * sin], axis=-1).astype(x.dtype)


def _ret_log_decay():
    return jnp.log(1.0 - 2.0 ** (-5.0 - jnp.arange(RET_HEADS, dtype=jnp.float32)))


def _retention(q, k, v, s0, chunk):
    B, L, H, DK = q.shape
    DV = v.shape[-1]
    n = L // chunk
    lg = _ret_log_decay()
    idx = jnp.arange(chunk, dtype=jnp.float32)
    rel = idx[:, None] - idx[None, :]
    dmask = jnp.where(rel >= 0, jnp.exp(lg[:, None, None] * jnp.maximum(rel, 0.0)), 0.0)
    q_dec = jnp.exp(lg[:, None] * (idx + 1.0)).T[None, :, :, None]
    k_dec = jnp.exp(lg[:, None] * (chunk - 1.0 - idx)).T[None, :, :, None]
    c_dec = jnp.exp(lg * chunk)[None, :, None, None]

    def to_chunks(t):
        return t.astype(jnp.float32).reshape(B, n, chunk, H, t.shape[-1]).transpose(1, 0, 2, 3, 4)

    def step(s, inp):
        qc, kc, vc = inp
        scores = jnp.einsum('bihd,bjhd->bhij', qc, kc) * dmask[None]
        inner = jnp.einsum('bhij,bjhe->bihe', scores, vc)
        cross = jnp.einsum('bihd,bhde->bihe', qc, s) * q_dec
        s_new = s * c_dec + jnp.einsum('bjhd,bjhe->bhde', kc * k_dec, vc)
        return s_new, inner + cross

    s_fin, out = lax.scan(step, s0.astype(jnp.float32), (to_chunks(q), to_chunks(k), to_chunks(v)))
    out = out.transpose(1, 0, 2, 3, 4).reshape(B, L, H, DV)
    return out, s_fin


def _retention_layer(x, pos, s0, ln, w_in, w_out, chunk):
    B, L, _ = x.shape
    h = _rms(x, ln)
    proj = h @ w_in
    dq = RET_HEADS * RET_DK
    dv = RET_HEADS * RET_DV
    q = proj[..., :dq].reshape(B, L, RET_HEADS, RET_DK)
    k = proj[..., dq:2 * dq].reshape(B, L, RET_HEADS, RET_DK)
    v = proj[..., 2 * dq:2 * dq + dv].reshape(B, L, RET_HEADS, RET_DV)
    gate = proj[..., 2 * dq + dv:]
    q = _rope(q, pos)
    k = _rope(k, pos) * (RET_DK ** -0.5)
    o, s_fin = _retention(q, k, v, s0, chunk)
    mu = jnp.mean(o, axis=-1, keepdims=True)
    var = jnp.mean(jnp.square(o - mu), axis=-1, keepdims=True)
    o = ((o - mu) * lax.rsqrt(var + EPS)).reshape(B, L, dv).astype(x.dtype)
    return (jax.nn.silu(gate) * o) @ w_out, s_fin


def _swiglu(x, ln, w_in, w_out):
    h = _rms(x, ln) @ w_in
    a, b = h[..., :D_FF], h[..., D_FF:]
    return (jax.nn.silu(a) * b) @ w_out


def _shared_kv(x, pos, ln_kv, w_kv, k_norm):
    B, L, _ = x.shape
    kv = _rms(x, ln_kv) @ w_kv
    dk = N_KV_HEADS * HEAD_DIM
    k = kv[..., :dk].reshape(B, L, N_KV_HEADS, HEAD_DIM)
    v = kv[..., dk:].reshape(B, L, N_KV_HEADS, HEAD_DIM)
    k = _rope(_rms(k, k_norm), pos)
    return k, v


def _sink_attend(q, k, v, mask, sink):
    s = jnp.einsum('...qhgd,...khd->...hgqk', q.astype(jnp.float32), k.astype(jnp.float32)) * (HEAD_DIM ** -0.5)
    s = jnp.where(mask, s, NEG_INF)
    sk = sink.astype(jnp.float32)[:, :, None, None]
    m = jnp.maximum(jnp.max(s, axis=-1, keepdims=True), sk)
    p = jnp.exp(s - m)
    denom = jnp.sum(p, axis=-1, keepdims=True) + jnp.exp(sk - m)
    o = jnp.einsum('...hgqk,...khd->...qhgd', p / denom, v.astype(jnp.float32))
    return o.astype(q.dtype)


def _band_blocks(t):
    B, L = t.shape[:2]
    tb = t.reshape(B, L // WINDOW, WINDOW, t.shape[2], t.shape[3])
    prev = jnp.pad(tb, ((0, 0), (1, 0), (0, 0), (0, 0), (0, 0)))[:, :-1]
    return jnp.concatenate([prev, tb], axis=2)


def _window_layer(x, pos, kk, vv, mask, ln, w_q, q_norm, sink, w_o, banded):
    B, L, _ = x.shape
    q = (_rms(x, ln) @ w_q).reshape(B, L, N_HEADS, HEAD_DIM)
    q = _rope(_rms(q, q_norm), pos)
    if banded:
        q = q.reshape(B, L // WINDOW, WINDOW, N_KV_HEADS, GROUP, HEAD_DIM)
    else:
        q = q.reshape(B, L, N_KV_HEADS, GROUP, HEAD_DIM)
    o = _sink_attend(q, kk, vv, mask, sink.reshape(N_KV_HEADS, GROUP))
    return o.reshape(B, L, N_HEADS * HEAD_DIM) @ w_o


def _forward(x, pos, ret_s0, win_k, win_v, ln_ret, w_ret_in, w_ret_out, ln_ffn, w_ffn_in, w_ffn_out,
             ln_kv, w_kv, k_norm, ln_attn, w_q, q_norm, sinks, w_o):
    B, L, _ = x.shape
    banded = win_k is None
    chunk = math.gcd(L, RET_CHUNK)
    ret_states = []
    kk = vv = mask = None
    k_rows = v_rows = None
    for layer in range(DEPTH):
        if layer < N_RET_LAYERS:
            y, s = _retention_layer(x, pos, ret_s0[layer], ln_ret[layer], w_ret_in[layer], w_ret_out[layer], chunk)
            ret_states.append(s)
            x = x + y
        else:
            if layer == N_RET_LAYERS:
                k_new, v_new = _shared_kv(x, pos, ln_kv, w_kv, k_norm)
                if banded:
                    nb = L // WINDOW
                    kk, vv = _band_blocks(k_new), _band_blocks(v_new)
                    qi = jnp.arange(WINDOW)
                    kj = jnp.arange(2 * WINDOW) - WINDOW
                    rel = qi[:, None] - kj[None, :]
                    band = (rel >= 0) & (rel < WINDOW)
                    kabs = jnp.arange(nb)[:, None] * WINDOW + kj[None, :]
                    mask = (band[None] & (kabs >= 0)[:, None, :])[:, None, None]
                    w_keep = min(WINDOW, L)
                    k_rows, v_rows = k_new[:, -w_keep:], v_new[:, -w_keep:]
                else:
                    wc = win_k.shape[1]
                    kk = jnp.concatenate([win_k.astype(k_new.dtype), k_new], axis=1)
                    vv = jnp.concatenate([win_v.astype(v_new.dtype), v_new], axis=1)
                    kpos = jnp.concatenate([PAST_LEN - wc + jnp.arange(wc), pos])
                    rel = pos[:, None] - kpos[None, :]
                    mask = (rel >= 0) & (rel < WINDOW)
                    k_rows, v_rows = kk[:, -wc:], vv[:, -wc:]
            j = layer - N_RET_LAYERS
            x = x + _window_layer(x, pos, kk, vv, mask, ln_attn[j], w_q[j], q_norm[j], sinks[j], w_o[j], banded)
        x = x + _swiglu(x, ln_ffn[layer], w_ffn_in[layer], w_ffn_out[layer])
    return x, jnp.stack(ret_states), k_rows, v_rows


def setup_inputs(seed: int = 0) -> dict:
    key = jax.random.key(seed)
    ks = jax.random.split(key, 20)
    f32 = jnp.float32
    res = (2.0 * DEPTH) ** -0.5
    w_cache = min(WINDOW, PAST_LEN)
    nrm = lambda k, shape, s: jax.random.normal(k, shape, f32) * s
    return {
        "x_prompt": nrm(ks[0], (BATCH, SEQ, D_MODEL), 1.0),
        "x_sample": nrm(ks[1], (DEC_BATCH, DEC_SEQ, D_MODEL), 1.0),
        "state_ret": nrm(ks[2], (N_RET_LAYERS, DEC_BATCH, RET_HEADS, RET_DK, RET_DV), 0.1),
        "cache_k": nrm(ks[3], (DEC_BATCH, w_cache, N_KV_HEADS, HEAD_DIM), 1.0),
        "cache_v": nrm(ks[4], (DEC_BATCH, w_cache, N_KV_HEADS, HEAD_DIM), 1.0),
        "ln_ret": 1.0 + nrm(ks[5], (N_RET_LAYERS, D_MODEL), 0.02),
        "w_ret_in": nrm(ks[6], (N_RET_LAYERS, D_MODEL, 2 * RET_HEADS * RET_DK + 2 * RET_HEADS * RET_DV), D_MODEL ** -0.5),
        "w_ret_out": nrm(ks[7], (N_RET_LAYERS, RET_HEADS * RET_DV, D_MODEL), res * (RET_HEADS * RET_DV) ** -0.5),
        "ln_ffn": 1.0 + nrm(ks[8], (DEPTH, D_MODEL), 0.02),
        "w_ffn_in": nrm(ks[9], (DEPTH, D_MODEL, 2 * D_FF), D_MODEL ** -0.5),
        "w_ffn_out": nrm(ks[10], (DEPTH, D_FF, D_MODEL), res * D_FF ** -0.5),
        "ln_kv": 1.0 + nrm(ks[11], (D_MODEL,), 0.02),
        "w_kv": nrm(ks[12], (D_MODEL, 2 * N_KV_HEADS * HEAD_DIM), D_MODEL ** -0.5),
        "k_norm": 1.0 + nrm(ks[13], (HEAD_DIM,), 0.02),
        "ln_attn": 1.0 + nrm(ks[14], (N_ATTN_LAYERS, D_MODEL), 0.02),
        "w_q": nrm(ks[15], (N_ATTN_LAYERS, D_MODEL, N_HEADS * HEAD_DIM), D_MODEL ** -0.5),
        "q_norm": 1.0 + nrm(ks[16], (N_ATTN_LAYERS, HEAD_DIM), 0.02),
        "sinks": nrm(ks[17], (N_ATTN_LAYERS, N_HEADS), 0.5),
        "w_o": nrm(ks[18], (N_ATTN_LAYERS, N_HEADS * HEAD_DIM, D_MODEL), res * (N_HEADS * HEAD_DIM) ** -0.5),
    }


def reference(x_prompt, x_sample, state_ret, cache_k, cache_v, ln_ret, w_ret_in, w_ret_out, ln_ffn, w_ffn_in,
              w_ffn_out, ln_kv, w_kv, k_norm, ln_attn, w_q, q_norm, sinks, w_o):
    B, L, _ = x_prompt.shape
    T = x_sample.shape[1]
    s0_prompt = jnp.zeros((N_RET_LAYERS, B, RET_HEADS, RET_DK, RET_DV), jnp.float32)
    y_prompt, state_ret_prompt, cache_k_prompt, cache_v_prompt = _forward(
        x_prompt, jnp.arange(L), s0_prompt, None, None, ln_ret, w_ret_in, w_ret_out, ln_ffn, w_ffn_in,
        w_ffn_out, ln_kv, w_kv, k_norm, ln_attn, w_q, q_norm, sinks, w_o)
    y_sample, state_ret_sample, cache_k_sample, cache_v_sample = _forward(
        x_sample, PAST_LEN + jnp.arange(T), state_ret, cache_k, cache_v, ln_ret, w_ret_in, w_ret_out, ln_ffn,
        w_ffn_in, w_ffn_out, ln_kv, w_kv, k_norm, ln_attn, w_q, q_norm, sinks, w_o)
    return (y_prompt, y_sample, state_ret_prompt, state_ret_sample, cache_k_prompt, cache_v_prompt,
            cache_k_sample, cache_v_sample)
```

```python
import functools
import math

import numpy as np
import jax
import jax.numpy as jnp
from jax import lax
from jax.experimental import pallas as pl
from jax.experimental.pallas import tpu as pltpu

D_MODEL = 1024
PAST_LEN = 8192
RET_HEADS = 4
RET_DK = 256
RET_DV = 512
RET_CHUNK = 128
N_HEADS = 16
N_KV_HEADS = 4
HEAD_DIM = 64
GROUP = N_HEADS // N_KV_HEADS
WINDOW = 128
D_FF = 2816
ROPE_THETA = 10000.0
EPS = 1e-6
NEG_INF = -1e30

F32 = jnp.float32
BF16 = jnp.bfloat16

V7X_VMEM_LIMIT_BYTES = 56 * 1024 * 1024
DEC_ROWS = 16
ROW_BLOCK = 512


def _dot(a, b):
    return jnp.dot(a, b, preferred_element_type=F32)


def _dot_nt(a, b):
    return lax.dot_general(a, b, (((1,), (1,)), ((), ())), preferred_element_type=F32)


def _dot_tn(a, b):
    return lax.dot_general(a, b, (((0,), (0,)), ((), ())), preferred_element_type=F32)


def _params(n_axes):
    return pltpu.CompilerParams(dimension_semantics=("arbitrary",) * n_axes,
                                vmem_limit_bytes=V7X_VMEM_LIMIT_BYTES)


def _row_tile(rows, prefs):
    for t in prefs:
        if rows % t == 0:
            return t
    raise ValueError(f"no row tile for {rows}")


def _rms_to_scratch(x_ref, g_ref, xn_ref):
    x = x_ref[...]
    ms = jnp.mean(x * x, axis=-1, keepdims=True)
    xn_ref[...] = ((x * lax.rsqrt(ms + EPS)) * g_ref[...]).astype(BF16)


def _rms_mm_kernel(x_ref, g_ref, w_ref, o_ref, xn_ref):
    @pl.when(pl.program_id(1) == 0)
    def _():
        _rms_to_scratch(x_ref, g_ref, xn_ref)

    o_ref[...] = _dot(xn_ref[...], w_ref[...].astype(BF16)).astype(o_ref.dtype)


def _rms_matmul(x, g, w, layer, tn, name):
    T, K = x.shape
    N = w.shape[-1]
    tm = _row_tile(T, (1536, 1024, 512))
    return pl.pallas_call(
        _rms_mm_kernel,
        grid=(T // tm, N // tn),
        in_specs=[pl.BlockSpec((tm, K), lambda i, j: (i, 0)),
                  pl.BlockSpec((None, 1, K), lambda i, j: (layer, 0, 0)),
                  pl.BlockSpec((None, K, tn), lambda i, j: (layer, 0, j))],
        out_specs=pl.BlockSpec((tm, tn), lambda i, j: (i, j)),
        out_shape=jax.ShapeDtypeStruct((T, N), F32),
        scratch_shapes=[pltpu.VMEM((tm, K), BF16)],
        compiler_params=_params(2),
        name=name,
    )(x, g, w)


def _rms_swiglu_kernel(x_ref, g_ref, wa_ref, wb_ref, o_ref, xn_ref):
    @pl.when(pl.program_id(1) == 0)
    def _():
        _rms_to_scratch(x_ref, g_ref, xn_ref)

    xn = xn_ref[...]
    a = _dot(xn, wa_ref[...].astype(BF16))
    b = _dot(xn, wb_ref[...].astype(BF16))
    o_ref[...] = ((a * jax.nn.sigmoid(a)) * b).astype(o_ref.dtype)


def _rms_swiglu(x, g, w, layer, name):
    T, K = x.shape
    tn = 256
    nb = D_FF // tn
    tm = _row_tile(T, (1536, 1024, 512))
    return pl.pallas_call(
        _rms_swiglu_kernel,
        grid=(T // tm, nb),
        in_specs=[pl.BlockSpec((tm, K), lambda i, j: (i, 0)),
                  pl.BlockSpec((None, 1, K), lambda i, j: (layer, 0, 0)),
                  pl.BlockSpec((None, K, tn), lambda i, j: (layer, 0, j)),
                  pl.BlockSpec((None, K, tn), lambda i, j: (layer, 0, nb + j))],
        out_specs=pl.BlockSpec((tm, tn), lambda i, j: (i, j)),
        out_shape=jax.ShapeDtypeStruct((T, D_FF), BF16),
        scratch_shapes=[pltpu.VMEM((tm, K), BF16)],
        compiler_params=_params(2),
        name=name,
    )(x, g, w, w)


def _mm_res_kernel(a_ref, w_ref, r_ref, o_ref, wb_ref):
    @pl.when(pl.program_id(0) == 0)
    def _():
        wb_ref[...] = w_ref[...].astype(BF16)

    o_ref[...] = r_ref[...] + _dot(a_ref[...], wb_ref[...])


def _matmul_residual(a, w, layer, res, name):
    T, K = a.shape
    N = w.shape[-1]
    tm = ROW_BLOCK
    return pl.pallas_call(
        _mm_res_kernel,
        grid=(T // tm,),
        in_specs=[pl.BlockSpec((tm, K), lambda i: (i, 0)),
                  pl.BlockSpec((None, K, N), lambda i: (layer, 0, 0)),
                  pl.BlockSpec((tm, N), lambda i: (i, 0))],
        out_specs=pl.BlockSpec((tm, N), lambda i: (i, 0)),
        out_shape=jax.ShapeDtypeStruct((T, N), F32),
        scratch_shapes=[pltpu.VMEM((K, N), BF16)],
        input_output_aliases={2: 0},
        compiler_params=_params(1),
        name=name,
    )(a, w, res)


def _rope_full_head(x, cos, sin):
    x1, x2 = x[:, :RET_DK // 2], x[:, RET_DK // 2:]
    return jnp.concatenate([x1 * cos - x2 * sin, x2 * cos + x1 * sin], axis=1)


def _groupnorm_gate(o, gate):
    mu = jnp.mean(o, axis=-1, keepdims=True)
    d = o - mu
    var = jnp.mean(d * d, axis=-1, keepdims=True)
    on = d * lax.rsqrt(var + EPS)
    return (gate * jax.nn.sigmoid(gate)) * on


def _ret_prompt_kernel(q_ref, k_ref, v_ref, g_ref, cos_ref, sin_ref, dmask_ref, qdec_ref, kdec_ref,
                       cdec_ref, go_ref, sfin_ref, s_ref, *, n_chunks):
    c = pl.program_id(2)

    @pl.when(c == 0)
    def _():
        s_ref[...] = jnp.zeros_like(s_ref)

    dmask = dmask_ref[...]
    qdec = qdec_ref[...]
    kdec = kdec_ref[...]
    cdec = cdec_ref[...]

    def chunk(ci, carry):
        rows = pl.ds(pl.multiple_of(ci * RET_CHUNK, RET_CHUNK), RET_CHUNK)
        cos = cos_ref[rows, :]
        sin = sin_ref[rows, :]
        q = _rope_full_head(q_ref[rows, :], cos, sin)
        k = _rope_full_head(k_ref[rows, :], cos, sin) * (RET_DK ** -0.5)
        qb = q.astype(BF16)
        kb = k.astype(BF16)
        vb = v_ref[rows, :].astype(BF16)
        s = s_ref[...]
        scores = _dot_nt(qb, kb) * dmask
        inner = _dot(scores.astype(BF16), vb)
        cross = _dot(qb, s.astype(BF16)) * qdec
        s_ref[...] = s * cdec + _dot_tn((k * kdec).astype(BF16), vb)
        go_ref[rows, :] = _groupnorm_gate(inner + cross, g_ref[rows, :]).astype(go_ref.dtype)
        return carry

    lax.fori_loop(0, n_chunks, chunk, 0)

    @pl.when(c == pl.num_programs(2) - 1)
    def _():
        sfin_ref[...] = s_ref[...]


def _ret_prompt(proj, tabs, B, L, T, name):
    rb = _row_tile(L, (1024, 512, 128))
    nblk = L // rb
    qcol, vcol = RET_HEADS, RET_HEADS
    gcol = 2 * RET_HEADS
    row = lambda b, h, c: b * nblk + c
    kern = functools.partial(_ret_prompt_kernel, n_chunks=rb // RET_CHUNK)
    return pl.pallas_call(
        kern,
        grid=(B, RET_HEADS, nblk),
        in_specs=[pl.BlockSpec((rb, RET_DK), lambda b, h, c: (row(b, h, c), h)),
                  pl.BlockSpec((rb, RET_DK), lambda b, h, c: (row(b, h, c), qcol + h)),
                  pl.BlockSpec((rb, RET_DV), lambda b, h, c: (row(b, h, c), vcol + h)),
                  pl.BlockSpec((rb, RET_DV), lambda b, h, c: (row(b, h, c), gcol + h)),
                  pl.BlockSpec((rb, RET_DK // 2), lambda b, h, c: (c, 0)),
                  pl.BlockSpec((rb, RET_DK // 2), lambda b, h, c: (c, 0)),
                  pl.BlockSpec((None, RET_CHUNK, RET_CHUNK), lambda b, h, c: (h, 0, 0)),
                  pl.BlockSpec((None, RET_CHUNK, 1), lambda b, h, c: (h, 0, 0)),
                  pl.BlockSpec((None, RET_CHUNK, 1), lambda b, h, c: (h, 0, 0)),
                  pl.BlockSpec((None, 1, 1), lambda b, h, c: (h, 0, 0))],
        out_specs=[pl.BlockSpec((rb, RET_DV), lambda b, h, c: (row(b, h, c), h)),
                   pl.BlockSpec((None, None, RET_DK, RET_DV), lambda b, h, c: (b, h, 0, 0))],
        out_shape=[jax.ShapeDtypeStruct((T, RET_HEADS * RET_DV), BF16),
                   jax.ShapeDtypeStruct((B, RET_HEADS, RET_DK, RET_DV), F32)],
        scratch_shapes=[pltpu.VMEM((RET_DK, RET_DV), F32)],
        compiler_params=_params(3),
        name=name,
    )(proj, proj, proj, proj, tabs["ret_cos"], tabs["ret_sin"], tabs["dmask_p"], tabs["qdec_p"],
      tabs["kdec_p"], tabs["cdec_p"])


def _ret_decode_kernel(q_ref, k_ref, v_ref, g_ref, cos_ref, sin_ref, dmask_ref, qdec_ref, kdec_ref,
                       cdec_ref, s_ref, *rest, t_dec):
    go_ref, snew_ref = rest[-2:]
    cos = cos_ref[...]
    sin = sin_ref[...]
    q = _rope_full_head(q_ref[...], cos, sin)
    k = _rope_full_head(k_ref[...], cos, sin) * (RET_DK ** -0.5)
    qb = q.astype(BF16)
    kb = k.astype(BF16)
    vb = v_ref[...].astype(BF16)
    cdec = cdec_ref[...]
    scores = _dot_nt(qb, kb) * dmask_ref[...]
    inner = _dot(scores.astype(BF16), vb)
    kd = k * kdec_ref[...]
    seq_of_row = lax.broadcasted_iota(jnp.int32, (DEC_ROWS, 1), 0) // t_dec
    cross = jnp.zeros((DEC_ROWS, RET_DV), F32)
    for bi in range(DEC_ROWS // t_dec):
        mine = seq_of_row == bi
        s = s_ref[bi]
        cross = cross + jnp.where(mine, _dot(qb, s.astype(BF16)), 0.0)
        snew_ref[bi] = s * cdec + _dot_tn(jnp.where(mine, kd, 0.0).astype(BF16), vb)
    cross = cross * qdec_ref[...]
    go_ref[...] = _groupnorm_gate(inner + cross, g_ref[...]).astype(go_ref.dtype)


def _ret_decode(proj, state, layer, go_joint, snew_joint, tabs, L, TP, DB, t_dec, name):
    nseq = DEC_ROWS // t_dec
    r0 = TP // DEC_ROWS
    qcol = vcol = RET_HEADS
    gcol = 2 * RET_HEADS
    kern = functools.partial(_ret_decode_kernel, t_dec=t_dec)
    s_spec = pl.BlockSpec((None, nseq, None, RET_DK, RET_DV), lambda i, h: (layer, i, h, 0, 0))
    any_spec = pl.BlockSpec(memory_space=pl.ANY)
    aliased = [go_joint] if snew_joint is None else [go_joint, snew_joint]
    return pl.pallas_call(
        kern,
        grid=(DB // nseq, RET_HEADS),
        in_specs=[pl.BlockSpec((DEC_ROWS, RET_DK), lambda i, h: (r0 + i, h)),
                  pl.BlockSpec((DEC_ROWS, RET_DK), lambda i, h: (r0 + i, qcol + h)),
                  pl.BlockSpec((DEC_ROWS, RET_DV), lambda i, h: (r0 + i, vcol + h)),
                  pl.BlockSpec((DEC_ROWS, RET_DV), lambda i, h: (r0 + i, gcol + h)),
                  pl.BlockSpec((DEC_ROWS, RET_DK // 2), lambda i, h: (L // DEC_ROWS, 0)),
                  pl.BlockSpec((DEC_ROWS, RET_DK // 2), lambda i, h: (L // DEC_ROWS, 0)),
                  pl.BlockSpec((None, DEC_ROWS, DEC_ROWS), lambda i, h: (h, 0, 0)),
                  pl.BlockSpec((None, DEC_ROWS, 1), lambda i, h: (h, 0, 0)),
                  pl.BlockSpec((None, DEC_ROWS, 1), lambda i, h: (h, 0, 0)),
                  pl.BlockSpec((None, 1, 1), lambda i, h: (h, 0, 0)),
                  s_spec] + [any_spec] * len(aliased),
        out_specs=[pl.BlockSpec((DEC_ROWS, RET_DV), lambda i, h: (r0 + i, h)), s_spec],
        out_shape=[jax.ShapeDtypeStruct(go_joint.shape, go_joint.dtype),
                   jax.ShapeDtypeStruct(state.shape, state.dtype)],
        input_output_aliases={11 + n: n for n in range(len(aliased))},
        compiler_params=_params(2),
        name=name,
    )(proj, proj, proj, proj, tabs["ret_cos"], tabs["ret_sin"], tabs["dmask_d"], tabs["qdec_d"],
      tabs["kdec_d"], tabs["cdec_d"], state, *aliased)


def _headnorm_rope(x, w, cos2, sin2, bd):
    width = x.shape[1]
    sq = x * x
    hi = sq.astype(BF16)
    lo = (sq - hi.astype(F32)).astype(BF16)
    parts = []
    for c in range(width // 256):
        sl = slice(256 * c, 256 * (c + 1))
        parts.append(_dot(hi[:, sl], bd) + _dot(lo[:, sl], bd))
    ss = parts[0] if len(parts) == 1 else jnp.concatenate(parts, axis=1)
    y = (x * lax.rsqrt(ss * (1.0 / HEAD_DIM) + EPS)) * w
    lane = lax.broadcasted_iota(jnp.int32, y.shape, 1)
    first_half = (lane & (HEAD_DIM - 1)) < HEAD_DIM // 2
    rot = jnp.where(first_half, pltpu.roll(y, width - HEAD_DIM // 2, 1), pltpu.roll(y, HEAD_DIM // 2, 1))
    nrep = width // 128
    cosw = jnp.concatenate([cos2] * nrep, axis=1)
    sinw = jnp.concatenate([sin2] * nrep, axis=1)
    return y * cosw + rot * sinw


def _kv_post_kernel(k_ref, w_ref, cos_ref, sin_ref, bd_ref, o_ref):
    o_ref[...] = _headnorm_rope(k_ref[...], w_ref[...], cos_ref[...], sin_ref[...], bd_ref[...])


def _table_block(i, n_prompt_blocks, blocks_per_seq):
    return jnp.where(i < n_prompt_blocks, i % blocks_per_seq, blocks_per_seq + i - n_prompt_blocks)


def _kv_post(kv, k_norm_w, tabs, L, TP, name):
    T = kv.shape[0]
    kw = N_KV_HEADS * HEAD_DIM
    npb, bps = TP // ROW_BLOCK, L // ROW_BLOCK
    tab = lambda i: (_table_block(i, npb, bps), 0)
    return pl.pallas_call(
        _kv_post_kernel,
        grid=(T // ROW_BLOCK,),
        in_specs=[pl.BlockSpec((ROW_BLOCK, kw), lambda i: (i, 0)),
                  pl.BlockSpec((1, kw), lambda i: (0, 0)),
                  pl.BlockSpec((ROW_BLOCK, 128), tab),
                  pl.BlockSpec((ROW_BLOCK, 128), tab),
                  pl.BlockSpec((256, 256), lambda i: (0, 0))],
        out_specs=pl.BlockSpec((ROW_BLOCK, kw), lambda i: (i, 0)),
        out_shape=jax.ShapeDtypeStruct((T, kw), F32),
        compiler_params=_params(1),
        name=name,
    )(kv, k_norm_w, tabs["att_cos"], tabs["att_sin"], tabs["bd"])


def _sink_column(sink_ref, layer, g, rows, rows_per_head):
    head_in_group = lax.broadcasted_iota(jnp.int32, (rows, 1), 0) // rows_per_head
    col = jnp.full((rows, 1), sink_ref[layer, g * GROUP], F32)
    for r in range(1, GROUP):
        col = jnp.where(head_in_group == r, sink_ref[layer, g * GROUP + r], col)
    return col


def _attn_prompt_kernel(sink_ref, q_ref, kp_ref, kc_ref, vp_ref, vc_ref, qw_ref, cos_ref, sin_ref, bd_ref,
                        o_ref, *, layer):
    blk = pl.program_id(1)
    q = _headnorm_rope(q_ref[...], qw_ref[...], cos_ref[...], sin_ref[...], bd_ref[...])
    qb = q.astype(BF16)
    kcat = jnp.concatenate([kp_ref[...], kc_ref[...]], axis=0).astype(BF16)
    vcat = jnp.concatenate([vp_ref[...], vc_ref[...]], axis=0).astype(BF16)
    rows = GROUP * WINDOW
    qi = lax.broadcasted_iota(jnp.int32, (rows, 2 * WINDOW), 0) & (WINDOW - 1)
    kj = lax.broadcasted_iota(jnp.int32, (rows, 2 * WINDOW), 1)
    visible = (kj > qi) & (kj <= qi + WINDOW) & ((kj >= WINDOW) | (blk > 0))
    for g in range(N_KV_HEADS):
        heads = [g * GROUP + r for r in range(GROUP)]
        qs = jnp.concatenate([qb[:, HEAD_DIM * h:HEAD_DIM * (h + 1)] for h in heads], axis=0)
        kg = kcat[:, HEAD_DIM * g:HEAD_DIM * (g + 1)]
        vg = vcat[:, HEAD_DIM * g:HEAD_DIM * (g + 1)]
        s = _dot_nt(qs, kg) * (HEAD_DIM ** -0.5)
        s = jnp.where(visible, s, NEG_INF)
        sk = _sink_column(sink_ref, layer, g, rows, WINDOW)
        m = jnp.maximum(jnp.max(s, axis=-1, keepdims=True), sk)
        p = jnp.exp(s - m)
        denom = jnp.sum(p, axis=-1, keepdims=True) + jnp.exp(sk - m)
        o = _dot((p * (1.0 / denom)).astype(BF16), vg)
        for r, h in enumerate(heads):
            o_ref[:, HEAD_DIM * h:HEAD_DIM * (h + 1)] = o[WINDOW * r:WINDOW * (r + 1)].astype(o_ref.dtype)


def _attn_prompt(q, kn, kv, sinks, q_norm_w, layer, tabs, B, L, T, name):
    nb = L // WINDOW
    kw = N_KV_HEADS * HEAD_DIM
    cur = lambda b, i: b * nb + i
    prev = lambda b, i: b * nb + jnp.maximum(i - 1, 0)
    kern = functools.partial(_attn_prompt_kernel, layer=layer)
    return pl.pallas_call(
        kern,
        grid=(B, nb),
        in_specs=[pl.BlockSpec(memory_space=pltpu.SMEM),
                  pl.BlockSpec((WINDOW, D_MODEL), lambda b, i: (cur(b, i), 0)),
                  pl.BlockSpec((WINDOW, kw), lambda b, i: (prev(b, i), 0)),
                  pl.BlockSpec((WINDOW, kw), lambda b, i: (cur(b, i), 0)),
                  pl.BlockSpec((WINDOW, kw), lambda b, i: (prev(b, i), 1)),
                  pl.BlockSpec((WINDOW, kw), lambda b, i: (cur(b, i), 1)),
                  pl.BlockSpec((None, 1, D_MODEL), lambda b, i: (layer, 0, 0)),
                  pl.BlockSpec((WINDOW, 128), lambda b, i: (i, 0)),
                  pl.BlockSpec((WINDOW, 128), lambda b, i: (i, 0)),
                  pl.BlockSpec((256, 256), lambda b, i: (0, 0))],
        out_specs=pl.BlockSpec((WINDOW, D_MODEL), lambda b, i: (cur(b, i), 0)),
        out_shape=jax.ShapeDtypeStruct((T, D_MODEL), BF16),
        compiler_params=_params(2),
        name=name,
    )(sinks, q, kn, kn, kv, kv, q_norm_w, tabs["att_cos"], tabs["att_sin"], tabs["bd"])


def _attn_decode_kernel(sink_ref, q_ref, kn_ref, vn_ref, ck_ref, cv_ref, qw_ref, cos_ref, sin_ref, bd_ref,
                        ao_in_ref, o_ref, *, layer, t_dec):
    del ao_in_ref
    nseq = DEC_ROWS // t_dec
    wc = ck_ref.shape[1]
    q = _headnorm_rope(q_ref[...], qw_ref[...], cos_ref[...], sin_ref[...], bd_ref[...])
    qb = q.astype(BF16)
    knew = kn_ref[...].astype(BF16)
    vnew = vn_ref[...].astype(BF16)
    rows = GROUP * DEC_ROWS
    row = lax.broadcasted_iota(jnp.int32, (rows, 1), 0)
    row_seq = (row % DEC_ROWS) // t_dec
    row_tok = row % t_dec
    jold = lax.broadcasted_iota(jnp.int32, (rows, wc), 1)
    vis_old = (jold > row_tok + (wc - WINDOW)) & (jold <= row_tok + wc)
    cnew = lax.broadcasted_iota(jnp.int32, (rows, DEC_ROWS), 1)
    vis_new = (cnew // t_dec == row_seq) & (cnew % t_dec <= row_tok)
    for g in range(N_KV_HEADS):
        heads = [g * GROUP + r for r in range(GROUP)]
        hs = slice(HEAD_DIM * g, HEAD_DIM * (g + 1))
        qs = jnp.concatenate([qb[:, HEAD_DIM * h:HEAD_DIM * (h + 1)] for h in heads], axis=0)
        s_old = jnp.zeros((rows, wc), F32)
        for bi in range(nseq):
            kc = ck_ref[bi][:, hs].astype(BF16)
            s_old = s_old + jnp.where(row_seq == bi, _dot_nt(qs, kc), 0.0)
        s_old = jnp.where(vis_old, s_old * (HEAD_DIM ** -0.5), NEG_INF)
        s_new = jnp.where(vis_new, _dot_nt(qs, knew[:, hs]) * (HEAD_DIM ** -0.5), NEG_INF)
        sk = _sink_column(sink_ref, layer, g, rows, DEC_ROWS)
        m = jnp.maximum(jnp.maximum(jnp.max(s_old, axis=-1, keepdims=True),
                                    jnp.max(s_new, axis=-1, keepdims=True)), sk)
        p_old = jnp.exp(s_old - m)
        p_new = jnp.exp(s_new - m)
        denom = (jnp.sum(p_old, axis=-1, keepdims=True) + jnp.sum(p_new, axis=-1, keepdims=True)
                 + jnp.exp(sk - m))
        inv = 1.0 / denom
        o = _dot((p_new * inv).astype(BF16), vnew[:, hs])
        pn_old = p_old * inv
        for bi in range(nseq):
            vc = cv_ref[bi][:, hs].astype(BF16)
            o = o + _dot(jnp.where(row_seq == bi, pn_old, 0.0).astype(BF16), vc)
        for r, h in enumerate(heads):
            o_ref[:, HEAD_DIM * h:HEAD_DIM * (h + 1)] = o[DEC_ROWS * r:DEC_ROWS * (r + 1)].astype(o_ref.dtype)


def _attn_decode(q, kn, kv, cache_k, cache_v, sinks, q_norm_w, layer, ao_joint, tabs, L, TP, DB, t_dec, name):
    nseq = DEC_ROWS // t_dec
    r0 = TP // DEC_ROWS
    kw = N_KV_HEADS * HEAD_DIM
    wc = cache_k.shape[1]
    kern = functools.partial(_attn_decode_kernel, layer=layer, t_dec=t_dec)
    return pl.pallas_call(
        kern,
        grid=(DB // nseq,),
        in_specs=[pl.BlockSpec(memory_space=pltpu.SMEM),
                  pl.BlockSpec((DEC_ROWS, D_MODEL), lambda i: (r0 + i, 0)),
                  pl.BlockSpec((DEC_ROWS, kw), lambda i: (r0 + i, 0)),
                  pl.BlockSpec((DEC_ROWS, kw), lambda i: (r0 + i, 1)),
                  pl.BlockSpec((nseq, wc, kw), lambda i: (i, 0, 0)),
                  pl.BlockSpec((nseq, wc, kw), lambda i: (i, 0, 0)),
                  pl.BlockSpec((None, 1, D_MODEL), lambda i: (layer, 0, 0)),
                  pl.BlockSpec((DEC_ROWS, 128), lambda i: (L // DEC_ROWS, 0)),
                  pl.BlockSpec((DEC_ROWS, 128), lambda i: (L // DEC_ROWS, 0)),
                  pl.BlockSpec((256, 256), lambda i: (0, 0)),
                  pl.BlockSpec(memory_space=pl.ANY)],
        out_specs=pl.BlockSpec((DEC_ROWS, D_MODEL), lambda i: (r0 + i, 0)),
        out_shape=jax.ShapeDtypeStruct(ao_joint.shape, ao_joint.dtype),
        input_output_aliases={10: 0},
        compiler_params=_params(1),
        name=name,
    )(sinks, q, kn, kv, cache_k, cache_v, q_norm_w, tabs["att_cos"], tabs["att_sin"], tabs["bd"], ao_joint)


def _rope_cos_sin(pos, half):
    inv = 1.0 / (ROPE_THETA ** (jnp.arange(half, dtype=F32) / half))
    ang = pos.astype(F32)[:, None] * inv[None, :]
    return jnp.cos(ang), jnp.sin(ang)


def _decay_tables(chunk, reps):
    lg = jnp.log(1.0 - 2.0 ** (-5.0 - jnp.arange(RET_HEADS, dtype=F32)))
    idx = jnp.arange(chunk, dtype=F32)
    rel = idx[:, None] - idx[None, :]
    dmask = jnp.where(rel >= 0, jnp.exp(lg[:, None, None] * jnp.maximum(rel, 0.0)), 0.0)
    qdec = jnp.exp(lg[:, None] * (idx + 1.0))[:, :, None]
    kdec = jnp.exp(lg[:, None] * (chunk - 1.0 - idx))[:, :, None]
    cdec = jnp.exp(lg * chunk)[:, None, None]
    if reps > 1:
        eye = jnp.eye(reps, dtype=F32)
        dmask = jnp.einsum("ab,hij->haibj", eye, dmask).reshape(RET_HEADS, reps * chunk, reps * chunk)
        qdec = jnp.tile(qdec, (1, reps, 1))
        kdec = jnp.tile(kdec, (1, reps, 1))
    return dmask, qdec, kdec, cdec


def _tables(L, TD, t_dec, chunk_p):
    pos = jnp.concatenate([jnp.arange(L), PAST_LEN + jnp.arange(TD) % t_dec])
    ret_cos, ret_sin = _rope_cos_sin(pos, RET_DK // 2)
    c, s = _rope_cos_sin(pos, HEAD_DIM // 2)
    att_cos = jnp.concatenate([c, c, c, c], axis=1)
    att_sin = jnp.concatenate([-s, s, -s, s], axis=1)
    dmask_p, qdec_p, kdec_p, cdec_p = _decay_tables(chunk_p, 1)
    dmask_d, qdec_d, kdec_d, cdec_d = _decay_tables(math.gcd(t_dec, RET_CHUNK), DEC_ROWS // t_dec)
    head_of_lane = np.arange(256) // HEAD_DIM
    bd = jnp.asarray((head_of_lane[:, None] == head_of_lane[None, :]).astype(np.float32), dtype=BF16)
    return dict(ret_cos=ret_cos, ret_sin=ret_sin, att_cos=att_cos, att_sin=att_sin,
                dmask_p=dmask_p, qdec_p=qdec_p, kdec_p=kdec_p, cdec_p=cdec_p,
                dmask_d=dmask_d, qdec_d=qdec_d, kdec_d=kdec_d, cdec_d=cdec_d, bd=bd)


def kernel(x_prompt, x_sample, state_ret, cache_k, cache_v, ln_ret, w_ret_in, w_ret_out, ln_ffn, w_ffn_in,
           w_ffn_out, ln_kv, w_kv, k_norm, ln_attn, w_q, q_norm, sinks, w_o):
    B, L, D = x_prompt.shape
    DB, t_dec, _ = x_sample.shape
    n_ret = w_ret_in.shape[0]
    n_attn = w_q.shape[0]
    TP, TD = B * L, DB * t_dec
    T = TP + TD
    wc = cache_k.shape[1]
    kw = N_KV_HEADS * HEAD_DIM
    assert D == D_MODEL and L % RET_CHUNK == 0 and L % ROW_BLOCK == 0 and TD % ROW_BLOCK == 0
    assert t_dec == math.gcd(t_dec, RET_CHUNK) and DEC_ROWS % t_dec == 0 and wc == WINDOW and PAST_LEN >= WINDOW

    tabs = _tables(L, TD, t_dec, RET_CHUNK)
    x = jnp.concatenate([x_prompt.reshape(TP, D), x_sample.reshape(TD, D)], axis=0)
    ck = cache_k.reshape(DB, wc, kw)
    cv = cache_v.reshape(DB, wc, kw)
    q_norm_w = jnp.tile(q_norm, (1, N_HEADS))[:, None, :]
    k_norm_w = jnp.tile(k_norm[None, :], (1, N_KV_HEADS))
    ln_ret, ln_ffn, ln_attn = ln_ret[:, None, :], ln_ffn[:, None, :], ln_attn[:, None, :]

    prompt_states = []
    dec_states = None
    for l in range(n_ret):
        proj = _rms_matmul(x, ln_ret, w_ret_in, l, 512, f"ret_in_{l}")
        go, s_p = _ret_prompt(proj, tabs, B, L, T, f"ret_prompt_{l}")
        go, dec_states = _ret_decode(proj, state_ret, l, go, dec_states, tabs, L, TP, DB, t_dec, f"ret_decode_{l}")
        prompt_states.append(s_p)
        x = _matmul_residual(go, w_ret_out, l, x, f"ret_out_{l}")
        h = _rms_swiglu(x, ln_ffn, w_ffn_in, l, f"ffn_in_{l}")
        x = _matmul_residual(h, w_ffn_out, l, x, f"ffn_out_{l}")

    kv = _rms_matmul(x, ln_kv[None, None, :], w_kv[None], 0, 2 * kw, "kv_proj")
    kn = _kv_post(kv, k_norm_w, tabs, L, TP, "kv_post")
    for j in range(n_attn):
        layer = n_ret + j
        q = _rms_matmul(x, ln_attn, w_q, j, 512, f"q_proj_{j}")
        ao = _attn_prompt(q, kn, kv, sinks, q_norm_w, j, tabs, B, L, T, f"attn_prompt_{j}")
        ao = _attn_decode(q, kn, kv, ck, cv, sinks, q_norm_w, j, ao, tabs, L, TP, DB, t_dec, f"attn_decode_{j}")
        x = _matmul_residual(ao, w_o, j, x, f"attn_out_{j}")
        h = _rms_swiglu(x, ln_ffn, w_ffn_in, layer, f"ffn_in_{layer}")
        x = _matmul_residual(h, w_ffn_out, layer, x, f"ffn_out_{layer}")

    y_prompt = x[:TP].reshape(B, L, D)
    y_sample = x[TP:].reshape(DB, t_dec, D)
    state_prompt = jnp.stack(prompt_states)
    w_keep = min(WINDOW, L)
    kn_p = kn[:TP].reshape(B, L, N_KV_HEADS, HEAD_DIM)
    v_p = kv[:TP, kw:].reshape(B, L, N_KV_HEADS, HEAD_DIM)
    kn_d = kn[TP:].reshape(DB, t_dec, N_KV_HEADS, HEAD_DIM)
    v_d = kv[TP:, kw:].reshape(DB, t_dec, N_KV_HEADS, HEAD_DIM)
    cache_k_sample = jnp.concatenate([cache_k, kn_d], axis=1)[:, -wc:]
    cache_v_sample = jnp.concatenate([cache_v, v_d], axis=1)[:, -wc:]
    return (y_prompt, y_sample, state_prompt, dec_states, kn_p[:, -w_keep:], v_p[:, -w_keep:],
            cache_k_sample, cache_v_sample)
```

```python
import functools
import math

import numpy as np
import jax
import jax.numpy as jnp
from jax import lax
from jax.experimental import pallas as pl
from jax.experimental.pallas import tpu as pltpu

D_MODEL = 1024
PAST_LEN = 8192
RET_HEADS = 4
RET_DK = 256
RET_DV = 512
RET_CHUNK = 128
N_HEADS = 16
N_KV_HEADS = 4
HEAD_DIM = 64
GROUP = N_HEADS // N_KV_HEADS
WINDOW = 128
D_FF = 2816
ROPE_THETA = 10000.0
EPS = 1e-6
NEG_INF = -1e30

F32 = jnp.float32
BF16 = jnp.bfloat16

V7X_VMEM_LIMIT_BYTES = 56 * 1024 * 1024
DEC_ROWS = 16
ROW_BLOCK = 512
RET_UNROLL = 4
ATTN_Q_BLOCKS = 2


def _dot(a, b):
    return jnp.dot(a, b, preferred_element_type=F32)


def _dot_nt(a, b):
    return lax.dot_general(a, b, (((1,), (1,)), ((), ())), preferred_element_type=F32)


def _dot_tn(a, b):
    return lax.dot_general(a, b, (((0,), (0,)), ((), ())), preferred_element_type=F32)


def _params(n_axes):
    return pltpu.CompilerParams(dimension_semantics=("arbitrary",) * n_axes,
                                vmem_limit_bytes=V7X_VMEM_LIMIT_BYTES)


def _row_tile(rows, prefs):
    for t in prefs:
        if rows % t == 0:
            return t
    raise ValueError(f"no row tile for {rows}")


def _rms_to_scratch(x_ref, g_ref, xn_ref):
    x = x_ref[...]
    ms = jnp.mean(x * x, axis=-1, keepdims=True)
    xn_ref[...] = ((x * lax.rsqrt(ms + EPS)) * g_ref[...]).astype(BF16)


def _rms_mm_kernel(x_ref, g_ref, w_ref, o_ref, xn_ref):
    @pl.when(pl.program_id(1) == 0)
    def _():
        _rms_to_scratch(x_ref, g_ref, xn_ref)

    o_ref[...] = _dot(xn_ref[...], w_ref[...].astype(BF16)).astype(o_ref.dtype)


def _rms_matmul(x, g, w, layer, tn, out_dtype, name):
    T, K = x.shape
    N = w.shape[-1]
    tm = _row_tile(T, (1536, 1024, 512))
    return pl.pallas_call(
        _rms_mm_kernel,
        grid=(T // tm, N // tn),
        in_specs=[pl.BlockSpec((tm, K), lambda i, j: (i, 0)),
                  pl.BlockSpec((None, 1, K), lambda i, j: (layer, 0, 0)),
                  pl.BlockSpec((None, K, tn), lambda i, j: (layer, 0, j))],
        out_specs=pl.BlockSpec((tm, tn), lambda i, j: (i, j)),
        out_shape=jax.ShapeDtypeStruct((T, N), out_dtype),
        scratch_shapes=[pltpu.VMEM((tm, K), BF16)],
        compiler_params=_params(2),
        name=name,
    )(x, g, w)


def _rms_swiglu_kernel(x_ref, g_ref, wa_ref, wb_ref, o_ref, xn_ref):
    @pl.when(pl.program_id(1) == 0)
    def _():
        _rms_to_scratch(x_ref, g_ref, xn_ref)

    xn = xn_ref[...]
    a = _dot(xn, wa_ref[...].astype(BF16))
    b = _dot(xn, wb_ref[...].astype(BF16))
    o_ref[...] = ((a * jax.nn.sigmoid(a)) * b).astype(o_ref.dtype)


def _rms_swiglu(x, g, w, layer, name):
    T, K = x.shape
    tn = 256
    nb = D_FF // tn
    tm = _row_tile(T, (1536, 1024, 512))
    return pl.pallas_call(
        _rms_swiglu_kernel,
        grid=(T // tm, nb),
        in_specs=[pl.BlockSpec((tm, K), lambda i, j: (i, 0)),
                  pl.BlockSpec((None, 1, K), lambda i, j: (layer, 0, 0)),
                  pl.BlockSpec((None, K, tn), lambda i, j: (layer, 0, j)),
                  pl.BlockSpec((None, K, tn), lambda i, j: (layer, 0, nb + j))],
        out_specs=pl.BlockSpec((tm, tn), lambda i, j: (i, j)),
        out_shape=jax.ShapeDtypeStruct((T, D_FF), BF16),
        scratch_shapes=[pltpu.VMEM((tm, K), BF16)],
        compiler_params=_params(2),
        name=name,
    )(x, g, w, w)


def _mm_res_kernel(a_ref, w_ref, r_ref, o_ref, wb_ref):
    @pl.when(pl.program_id(0) == 0)
    def _():
        wb_ref[...] = w_ref[...].astype(BF16)

    o_ref[...] = r_ref[...] + _dot(a_ref[...], wb_ref[...])


def _matmul_residual(a, w, layer, res, name):
    T, K = a.shape
    N = w.shape[-1]
    tm = ROW_BLOCK
    return pl.pallas_call(
        _mm_res_kernel,
        grid=(T // tm,),
        in_specs=[pl.BlockSpec((tm, K), lambda i: (i, 0)),
                  pl.BlockSpec((None, K, N), lambda i: (layer, 0, 0)),
                  pl.BlockSpec((tm, N), lambda i: (i, 0))],
        out_specs=pl.BlockSpec((tm, N), lambda i: (i, 0)),
        out_shape=jax.ShapeDtypeStruct((T, N), F32),
        scratch_shapes=[pltpu.VMEM((K, N), BF16)],
        input_output_aliases={2: 0},
        compiler_params=_params(1),
        name=name,
    )(a, w, res)


def _rope_full_head(x, cos, sin):
    x1, x2 = x[:, :RET_DK // 2], x[:, RET_DK // 2:]
    return jnp.concatenate([x1 * cos - x2 * sin, x2 * cos + x1 * sin], axis=1)


def _groupnorm_gate(o, gate):
    mu = jnp.mean(o, axis=-1, keepdims=True)
    d = o - mu
    var = jnp.mean(d * d, axis=-1, keepdims=True)
    on = d * lax.rsqrt(var + EPS)
    return (gate * jax.nn.sigmoid(gate)) * on


def _ret_prompt_kernel(q_ref, k_ref, v_ref, g_ref, cos_ref, sin_ref, dmask_ref, qdec_ref, kdec_ref,
                       cdec_ref, go_ref, sfin_ref, s_ref, *, n_chunks):
    c = pl.program_id(2)

    @pl.when(c == 0)
    def _():
        s_ref[...] = jnp.zeros_like(s_ref)

    dmask = dmask_ref[...]
    qdec = qdec_ref[...]
    kdec = kdec_ref[...]
    cdec = cdec_ref[...]

    def chunk(ci, carry):
        rows = pl.ds(pl.multiple_of(ci * RET_CHUNK, RET_CHUNK), RET_CHUNK)
        cos = cos_ref[rows, :]
        sin = sin_ref[rows, :]
        q = _rope_full_head(q_ref[rows, :].astype(F32), cos, sin)
        k = _rope_full_head(k_ref[rows, :].astype(F32), cos, sin) * (RET_DK ** -0.5)
        qb = q.astype(BF16)
        kb = k.astype(BF16)
        vb = v_ref[rows, :].astype(BF16)
        s = s_ref[...]
        scores = _dot_nt(qb, kb) * dmask
        inner = _dot(scores.astype(BF16), vb)
        cross = _dot(qb, s.astype(BF16)) * qdec
        s_ref[...] = s * cdec + _dot_tn((k * kdec).astype(BF16), vb)
        go_ref[rows, :] = _groupnorm_gate(inner + cross, g_ref[rows, :].astype(F32)).astype(go_ref.dtype)
        return carry

    lax.fori_loop(0, n_chunks, chunk, 0, unroll=min(n_chunks, RET_UNROLL))

    @pl.when(c == pl.num_programs(2) - 1)
    def _():
        sfin_ref[...] = s_ref[...]


def _ret_prompt(proj, tabs, B, L, T, name):
    rb = _row_tile(L, (1024, 512, 128))
    nblk = L // rb
    qcol, vcol = RET_HEADS, RET_HEADS
    gcol = 2 * RET_HEADS
    row = lambda b, h, c: b * nblk + c
    kern = functools.partial(_ret_prompt_kernel, n_chunks=rb // RET_CHUNK)
    return pl.pallas_call(
        kern,
        grid=(B, RET_HEADS, nblk),
        in_specs=[pl.BlockSpec((rb, RET_DK), lambda b, h, c: (row(b, h, c), h)),
                  pl.BlockSpec((rb, RET_DK), lambda b, h, c: (row(b, h, c), qcol + h)),
                  pl.BlockSpec((rb, RET_DV), lambda b, h, c: (row(b, h, c), vcol + h)),
                  pl.BlockSpec((rb, RET_DV), lambda b, h, c: (row(b, h, c), gcol + h)),
                  pl.BlockSpec((rb, RET_DK // 2), lambda b, h, c: (c, 0)),
                  pl.BlockSpec((rb, RET_DK // 2), lambda b, h, c: (c, 0)),
                  pl.BlockSpec((None, RET_CHUNK, RET_CHUNK), lambda b, h, c: (h, 0, 0)),
                  pl.BlockSpec((None, RET_CHUNK, 1), lambda b, h, c: (h, 0, 0)),
                  pl.BlockSpec((None, RET_CHUNK, 1), lambda b, h, c: (h, 0, 0)),
                  pl.BlockSpec((None, 1, 1), lambda b, h, c: (h, 0, 0))],
        out_specs=[pl.BlockSpec((rb, RET_DV), lambda b, h, c: (row(b, h, c), h)),
                   pl.BlockSpec((None, None, RET_DK, RET_DV), lambda b, h, c: (b, h, 0, 0))],
        out_shape=[jax.ShapeDtypeStruct((T, RET_HEADS * RET_DV), BF16),
                   jax.ShapeDtypeStruct((B, RET_HEADS, RET_DK, RET_DV), F32)],
        scratch_shapes=[pltpu.VMEM((RET_DK, RET_DV), F32)],
        compiler_params=_params(3),
        name=name,
    )(proj, proj, proj, proj, tabs["ret_cos"], tabs["ret_sin"], tabs["dmask_p"], tabs["qdec_p"],
      tabs["kdec_p"], tabs["cdec_p"])


def _ret_decode_kernel(q_ref, k_ref, v_ref, g_ref, cos_ref, sin_ref, dmask_ref, qdec_ref, kdec_ref,
                       cdec_ref, s_ref, *rest, t_dec):
    go_ref, snew_ref = rest[-2:]
    cos = cos_ref[...]
    sin = sin_ref[...]
    q = _rope_full_head(q_ref[...].astype(F32), cos, sin)
    k = _rope_full_head(k_ref[...].astype(F32), cos, sin) * (RET_DK ** -0.5)
    qb = q.astype(BF16)
    kb = k.astype(BF16)
    vb = v_ref[...].astype(BF16)
    cdec = cdec_ref[...]
    scores = _dot_nt(qb, kb) * dmask_ref[...]
    inner = _dot(scores.astype(BF16), vb)
    kd = k * kdec_ref[...]
    seq_of_row = lax.broadcasted_iota(jnp.int32, (DEC_ROWS, 1), 0) // t_dec
    cross = jnp.zeros((DEC_ROWS, RET_DV), F32)
    for bi in range(DEC_ROWS // t_dec):
        mine = seq_of_row == bi
        s = s_ref[bi]
        cross = cross + jnp.where(mine, _dot(qb, s.astype(BF16)), 0.0)
        snew_ref[bi] = s * cdec + _dot_tn(jnp.where(mine, kd, 0.0).astype(BF16), vb)
    cross = cross * qdec_ref[...]
    go_ref[...] = _groupnorm_gate(inner + cross, g_ref[...].astype(F32)).astype(go_ref.dtype)


def _ret_decode(proj, state, layer, go_joint, snew_joint, tabs, L, TP, DB, t_dec, name):
    nseq = DEC_ROWS // t_dec
    r0 = TP // DEC_ROWS
    qcol = vcol = RET_HEADS
    gcol = 2 * RET_HEADS
    kern = functools.partial(_ret_decode_kernel, t_dec=t_dec)
    s_spec = pl.BlockSpec((None, nseq, None, RET_DK, RET_DV), lambda i, h: (layer, i, h, 0, 0))
    any_spec = pl.BlockSpec(memory_space=pl.ANY)
    aliased = [go_joint] if snew_joint is None else [go_joint, snew_joint]
    return pl.pallas_call(
        kern,
        grid=(DB // nseq, RET_HEADS),
        in_specs=[pl.BlockSpec((DEC_ROWS, RET_DK), lambda i, h: (r0 + i, h)),
                  pl.BlockSpec((DEC_ROWS, RET_DK), lambda i, h: (r0 + i, qcol + h)),
                  pl.BlockSpec((DEC_ROWS, RET_DV), lambda i, h: (r0 + i, vcol + h)),
                  pl.BlockSpec((DEC_ROWS, RET_DV), lambda i, h: (r0 + i, gcol + h)),
                  pl.BlockSpec((DEC_ROWS, RET_DK // 2), lambda i, h: (L // DEC_ROWS, 0)),
                  pl.BlockSpec((DEC_ROWS, RET_DK // 2), lambda i, h: (L // DEC_ROWS, 0)),
                  pl.BlockSpec((None, DEC_ROWS, DEC_ROWS), lambda i, h: (h, 0, 0)),
                  pl.BlockSpec((None, DEC_ROWS, 1), lambda i, h: (h, 0, 0)),
                  pl.BlockSpec((None, DEC_ROWS, 1), lambda i, h: (h, 0, 0)),
                  pl.BlockSpec((None, 1, 1), lambda i, h: (h, 0, 0)),
                  s_spec] + [any_spec] * len(aliased),
        out_specs=[pl.BlockSpec((DEC_ROWS, RET_DV), lambda i, h: (r0 + i, h)), s_spec],
        out_shape=[jax.ShapeDtypeStruct(go_joint.shape, go_joint.dtype),
                   jax.ShapeDtypeStruct(state.shape, state.dtype)],
        input_output_aliases={11 + n: n for n in range(len(aliased))},
        compiler_params=_params(2),
        name=name,
    )(proj, proj, proj, proj, tabs["ret_cos"], tabs["ret_sin"], tabs["dmask_d"], tabs["qdec_d"],
      tabs["kdec_d"], tabs["cdec_d"], state, *aliased)


def _headnorm_rope(x, w, cos2, sin2, bd):
    width = x.shape[1]
    sq = x * x
    hi = sq.astype(BF16)
    lo = (sq - hi.astype(F32)).astype(BF16)
    parts = []
    for c in range(width // 256):
        sl = slice(256 * c, 256 * (c + 1))
        parts.append(_dot(hi[:, sl], bd) + _dot(lo[:, sl], bd))
    ss = parts[0] if len(parts) == 1 else jnp.concatenate(parts, axis=1)
    y = (x * lax.rsqrt(ss * (1.0 / HEAD_DIM) + EPS)) * w
    lane = lax.broadcasted_iota(jnp.int32, y.shape, 1)
    first_half = (lane & (HEAD_DIM - 1)) < HEAD_DIM // 2
    rot = jnp.where(first_half, pltpu.roll(y, width - HEAD_DIM // 2, 1), pltpu.roll(y, HEAD_DIM // 2, 1))
    nrep = width // 128
    cosw = jnp.concatenate([cos2] * nrep, axis=1)
    sinw = jnp.concatenate([sin2] * nrep, axis=1)
    return y * cosw + rot * sinw


def _kv_post_kernel(k_ref, w_ref, cos_ref, sin_ref, bd_ref, o_ref, lo_ref, hi_ref):
    kn = _headnorm_rope(k_ref[...], w_ref[...], cos_ref[...], sin_ref[...], bd_ref[...])
    o_ref[...] = kn
    low = lax.broadcasted_iota(jnp.int32, (kn.shape[0], 128), 1) < HEAD_DIM
    zero = jnp.zeros((kn.shape[0], 128), F32)
    for c in range(N_KV_HEADS // 2):
        tile = kn[:, 128 * c:128 * (c + 1)]
        swapped = pltpu.roll(tile, HEAD_DIM, 1)
        even, odd = slice(256 * c, 256 * c + 128), slice(256 * c + 128, 256 * (c + 1))
        lo_ref[:, even] = jnp.where(low, tile, zero).astype(lo_ref.dtype)
        hi_ref[:, even] = jnp.where(low, zero, swapped).astype(hi_ref.dtype)
        lo_ref[:, odd] = jnp.where(low, swapped, zero).astype(lo_ref.dtype)
        hi_ref[:, odd] = jnp.where(low, zero, tile).astype(hi_ref.dtype)


def _table_block(i, n_prompt_blocks, blocks_per_seq):
    return jnp.where(i < n_prompt_blocks, i % blocks_per_seq, blocks_per_seq + i - n_prompt_blocks)


def _kv_post(kv, k_norm_w, tabs, L, TP, name):
    T = kv.shape[0]
    kw = N_KV_HEADS * HEAD_DIM
    npb, bps = TP // ROW_BLOCK, L // ROW_BLOCK
    tab = lambda i: (_table_block(i, npb, bps), 0)
    pad_spec = pl.BlockSpec((ROW_BLOCK, 128 * N_KV_HEADS), lambda i: (i, 0))
    pad_shape = jax.ShapeDtypeStruct((T, 128 * N_KV_HEADS), BF16)
    return pl.pallas_call(
        _kv_post_kernel,
        grid=(T // ROW_BLOCK,),
        in_specs=[pl.BlockSpec((ROW_BLOCK, kw), lambda i: (i, 0)),
                  pl.BlockSpec((1, kw), lambda i: (0, 0)),
                  pl.BlockSpec((ROW_BLOCK, 128), tab),
                  pl.BlockSpec((ROW_BLOCK, 128), tab),
                  pl.BlockSpec((256, 256), lambda i: (0, 0))],
        out_specs=[pl.BlockSpec((ROW_BLOCK, kw), lambda i: (i, 0)), pad_spec, pad_spec],
        out_shape=[jax.ShapeDtypeStruct((T, kw), F32), pad_shape, pad_shape],
        compiler_params=_params(1),
        name=name,
    )(kv, k_norm_w, tabs["att_cos"], tabs["att_sin"], tabs["bd"])


def _q_proj_kernel(x_ref, g_ref, w_ref, qw_ref, cos_ref, sin_ref, bd_ref, o_ref, wb_ref):
    @pl.when(pl.program_id(0) == 0)
    def _():
        wb_ref[...] = w_ref[...].astype(BF16)

    x = x_ref[...]
    ms = jnp.mean(x * x, axis=-1, keepdims=True)
    xn = ((x * lax.rsqrt(ms + EPS)) * g_ref[...]).astype(BF16)
    q = _headnorm_rope(_dot(xn, wb_ref[...]), qw_ref[...], cos_ref[...], sin_ref[...], bd_ref[...])
    o_ref[...] = (q * (HEAD_DIM ** -0.5)).astype(o_ref.dtype)


def _q_proj(x, g, w, q_norm_w, layer, tabs, L, TP, name):
    T, K = x.shape
    N = w.shape[-1]
    npb, bps = TP // ROW_BLOCK, L // ROW_BLOCK
    tab = lambda i: (_table_block(i, npb, bps), 0)
    return pl.pallas_call(
        _q_proj_kernel,
        grid=(T // ROW_BLOCK,),
        in_specs=[pl.BlockSpec((ROW_BLOCK, K), lambda i: (i, 0)),
                  pl.BlockSpec((None, 1, K), lambda i: (layer, 0, 0)),
                  pl.BlockSpec((None, K, N), lambda i: (layer, 0, 0)),
                  pl.BlockSpec((None, 1, N), lambda i: (layer, 0, 0)),
                  pl.BlockSpec((ROW_BLOCK, 128), tab),
                  pl.BlockSpec((ROW_BLOCK, 128), tab),
                  pl.BlockSpec((256, 256), lambda i: (0, 0))],
        out_specs=pl.BlockSpec((ROW_BLOCK, N), lambda i: (i, 0)),
        out_shape=jax.ShapeDtypeStruct((T, N), BF16),
        scratch_shapes=[pltpu.VMEM((K, N), BF16)],
        compiler_params=_params(1),
        name=name,
    )(x, g, w, q_norm_w, tabs["att_cos"], tabs["att_sin"], tabs["bd"])


def _sink_column(sink_ref, layer, g, rows, rows_per_head):
    head_in_group = lax.broadcasted_iota(jnp.int32, (rows, 1), 0) // rows_per_head
    col = jnp.full((rows, 1), sink_ref[layer, g * GROUP], F32)
    for r in range(1, GROUP):
        col = jnp.where(head_in_group == r, sink_ref[layer, g * GROUP + r], col)
    return col


def _attn_prompt_kernel(sink_ref, q_ref, lop_ref, loc_ref, hip_ref, hic_ref, vp_ref, vc_ref, o_ref, *,
                        layer, n_qblk):
    first = pl.program_id(1) == 0
    qb = q_ref[...]
    klo = jnp.concatenate([lop_ref[...], loc_ref[...]], axis=0)
    khi = jnp.concatenate([hip_ref[...], hic_ref[...]], axis=0)
    v_t = jnp.concatenate([vp_ref[...], vc_ref[...]], axis=0).T.astype(BF16)
    lanes = GROUP * WINDOW
    key = lax.broadcasted_iota(jnp.int32, (WINDOW, lanes), 0)
    qi = lax.broadcasted_iota(jnp.int32, (WINDOW, lanes), 1) & (WINDOW - 1)
    own = key <= qi
    head_of_lane = lax.broadcasted_iota(jnp.int32, (1, lanes), 1) // WINDOW
    for t in range(n_qblk):
        qrows = slice(WINDOW * t, WINDOW * (t + 1))
        krows = slice(WINDOW * t, WINDOW * (t + 2))
        pieces = []
        for g in range(N_KV_HEADS):
            gl = slice(128 * g, 128 * (g + 1))
            kk = jnp.concatenate([klo[krows, gl], khi[krows, gl]], axis=0)
            xq = jnp.concatenate([qb[qrows, 256 * g:256 * g + 128], qb[qrows, 256 * g + 128:256 * (g + 1)]],
                                 axis=0)
            s4 = _dot_nt(kk, xq)
            prev, cur = [], []
            for r in range(GROUP):
                blk = s4[2 * WINDOW * (r % 2):2 * WINDOW * (r % 2 + 1), WINDOW * (r // 2):WINDOW * (r // 2 + 1)]
                prev.append(blk[:WINDOW])
                cur.append(blk[WINDOW:])
            s_prev = jnp.concatenate(prev, axis=1)
            if t == 0:
                s_prev = jnp.where(first, NEG_INF, s_prev)
            s = jnp.where(own, jnp.concatenate(cur, axis=1), s_prev)
            sk = jnp.full((1, lanes), sink_ref[layer, g * GROUP], F32)
            for r in range(1, GROUP):
                sk = jnp.where(head_of_lane == r, sink_ref[layer, g * GROUP + r], sk)
            m = jnp.maximum(jnp.max(s, axis=0, keepdims=True), sk)
            p = jnp.exp(s - m)
            denom = jnp.sum(p, axis=0, keepdims=True) + jnp.exp(sk - m)
            p2 = jnp.concatenate([jnp.where(own, 0.0, p), jnp.where(own, p, 0.0)], axis=0).astype(BF16)
            o_t = _dot(v_t[HEAD_DIM * g:HEAD_DIM * (g + 1), krows], p2) * (1.0 / denom)
            pieces += [o_t[:, WINDOW * r:WINDOW * (r + 1)] for r in range(GROUP)]
        o_ref[qrows, :] = jnp.concatenate(pieces, axis=0).T.astype(o_ref.dtype)


def _attn_prompt(q, klo, khi, kv, sinks, layer, B, L, T, name):
    nb = L // WINDOW
    nq = ATTN_Q_BLOCKS if nb % ATTN_Q_BLOCKS == 0 else 1
    ns = nb // nq
    kw = N_KV_HEADS * HEAD_DIM
    kpad = 128 * N_KV_HEADS
    cur = lambda b, i: b * ns + i
    prev = lambda b, i: b * nb + jnp.maximum(i * nq - 1, 0)
    kern = functools.partial(_attn_prompt_kernel, layer=layer, n_qblk=nq)
    return pl.pallas_call(
        kern,
        grid=(B, ns),
        in_specs=[pl.BlockSpec(memory_space=pltpu.SMEM),
                  pl.BlockSpec((nq * WINDOW, D_MODEL), lambda b, i: (cur(b, i), 0)),
                  pl.BlockSpec((WINDOW, kpad), lambda b, i: (prev(b, i), 0)),
                  pl.BlockSpec((nq * WINDOW, kpad), lambda b, i: (cur(b, i), 0)),
                  pl.BlockSpec((WINDOW, kpad), lambda b, i: (prev(b, i), 0)),
                  pl.BlockSpec((nq * WINDOW, kpad), lambda b, i: (cur(b, i), 0)),
                  pl.BlockSpec((WINDOW, kw), lambda b, i: (prev(b, i), 1)),
                  pl.BlockSpec((nq * WINDOW, kw), lambda b, i: (cur(b, i), 1))],
        out_specs=pl.BlockSpec((nq * WINDOW, D_MODEL), lambda b, i: (cur(b, i), 0)),
        out_shape=jax.ShapeDtypeStruct((T, D_MODEL), BF16),
        compiler_params=_params(2),
        name=name,
    )(sinks, q, klo, klo, khi, khi, kv, kv)


def _attn_decode_kernel(sink_ref, q_ref, kn_ref, vn_ref, ck_ref, cv_ref, ao_in_ref, o_ref, *, layer, t_dec):
    del ao_in_ref
    nseq = DEC_ROWS // t_dec
    wc = ck_ref.shape[1]
    qb = q_ref[...]
    knew = kn_ref[...].astype(BF16)
    vnew = vn_ref[...].astype(BF16)
    rows = GROUP * DEC_ROWS
    row = lax.broadcasted_iota(jnp.int32, (rows, 1), 0)
    row_seq = (row % DEC_ROWS) // t_dec
    row_tok = row % t_dec
    jold = lax.broadcasted_iota(jnp.int32, (rows, wc), 1)
    vis_old = (jold > row_tok + (wc - WINDOW)) & (jold <= row_tok + wc)
    cnew = lax.broadcasted_iota(jnp.int32, (rows, DEC_ROWS), 1)
    vis_new = (cnew // t_dec == row_seq) & (cnew % t_dec <= row_tok)
    for g in range(N_KV_HEADS):
        heads = [g * GROUP + r for r in range(GROUP)]
        hs = slice(HEAD_DIM * g, HEAD_DIM * (g + 1))
        qs = jnp.concatenate([qb[:, HEAD_DIM * h:HEAD_DIM * (h + 1)] for h in heads], axis=0)
        s_old = jnp.zeros((rows, wc), F32)
        for bi in range(nseq):
            kc = ck_ref[bi][:, hs].astype(BF16)
            s_old = s_old + jnp.where(row_seq == bi, _dot_nt(qs, kc), 0.0)
        s_old = jnp.where(vis_old, s_old, NEG_INF)
        s_new = jnp.where(vis_new, _dot_nt(qs, knew[:, hs]), NEG_INF)
        sk = _sink_column(sink_ref, layer, g, rows, DEC_ROWS)
        m = jnp.maximum(jnp.maximum(jnp.max(s_old, axis=-1, keepdims=True),
                                    jnp.max(s_new, axis=-1, keepdims=True)), sk)
        p_old = jnp.exp(s_old - m)
        p_new = jnp.exp(s_new - m)
        denom = (jnp.sum(p_old, axis=-1, keepdims=True) + jnp.sum(p_new, axis=-1, keepdims=True)
                 + jnp.exp(sk - m))
        inv = 1.0 / denom
        o = _dot((p_new * inv).astype(BF16), vnew[:, hs])
        pn_old = p_old * inv
        for bi in range(nseq):
            vc = cv_ref[bi][:, hs].astype(BF16)
            o = o + _dot(jnp.where(row_seq == bi, pn_old, 0.0).astype(BF16), vc)
        for r, h in enumerate(heads):
            o_ref[:, HEAD_DIM * h:HEAD_DIM * (h + 1)] = o[DEC_ROWS * r:DEC_ROWS * (r + 1)].astype(o_ref.dtype)


def _attn_decode(q, kn, kv, cache_k, cache_v, sinks, layer, ao_joint, TP, DB, t_dec, name):
    nseq = DEC_ROWS // t_dec
    r0 = TP // DEC_ROWS
    kw = N_KV_HEADS * HEAD_DIM
    wc = cache_k.shape[1]
    kern = functools.partial(_attn_decode_kernel, layer=layer, t_dec=t_dec)
    return pl.pallas_call(
        kern,
        grid=(DB // nseq,),
        in_specs=[pl.BlockSpec(memory_space=pltpu.SMEM),
                  pl.BlockSpec((DEC_ROWS, D_MODEL), lambda i: (r0 + i, 0)),
                  pl.BlockSpec((DEC_ROWS, kw), lambda i: (r0 + i, 0)),
                  pl.BlockSpec((DEC_ROWS, kw), lambda i: (r0 + i, 1)),
                  pl.BlockSpec((nseq, wc, kw), lambda i: (i, 0, 0)),
                  pl.BlockSpec((nseq, wc, kw), lambda i: (i, 0, 0)),
                  pl.BlockSpec(memory_space=pl.ANY)],
        out_specs=pl.BlockSpec((DEC_ROWS, D_MODEL), lambda i: (r0 + i, 0)),
        out_shape=jax.ShapeDtypeStruct(ao_joint.shape, ao_joint.dtype),
        input_output_aliases={6: 0},
        compiler_params=_params(1),
        name=name,
    )(sinks, q, kn, kv, cache_k, cache_v, ao_joint)


def _rope_cos_sin(pos, half):
    inv = 1.0 / (ROPE_THETA ** (jnp.arange(half, dtype=F32) / half))
    ang = pos.astype(F32)[:, None] * inv[None, :]
    return jnp.cos(ang), jnp.sin(ang)


def _decay_tables(chunk, reps):
    lg = jnp.log(1.0 - 2.0 ** (-5.0 - jnp.arange(RET_HEADS, dtype=F32)))
    idx = jnp.arange(chunk, dtype=F32)
    rel = idx[:, None] - idx[None, :]
    dmask = jnp.where(rel >= 0, jnp.exp(lg[:, None, None] * jnp.maximum(rel, 0.0)), 0.0)
    qdec = jnp.exp(lg[:, None] * (idx + 1.0))[:, :, None]
    kdec = jnp.exp(lg[:, None] * (chunk - 1.0 - idx))[:, :, None]
    cdec = jnp.exp(lg * chunk)[:, None, None]
    if reps > 1:
        eye = jnp.eye(reps, dtype=F32)
        dmask = jnp.einsum("ab,hij->haibj", eye, dmask).reshape(RET_HEADS, reps * chunk, reps * chunk)
        qdec = jnp.tile(qdec, (1, reps, 1))
        kdec = jnp.tile(kdec, (1, reps, 1))
    return dmask, qdec, kdec, cdec


def _tables(L, TD, t_dec, chunk_p):
    pos = jnp.concatenate([jnp.arange(L), PAST_LEN + jnp.arange(TD) % t_dec])
    ret_cos, ret_sin = _rope_cos_sin(pos, RET_DK // 2)
    c, s = _rope_cos_sin(pos, HEAD_DIM // 2)
    att_cos = jnp.concatenate([c, c, c, c], axis=1)
    att_sin = jnp.concatenate([-s, s, -s, s], axis=1)
    dmask_p, qdec_p, kdec_p, cdec_p = _decay_tables(chunk_p, 1)
    dmask_d, qdec_d, kdec_d, cdec_d = _decay_tables(math.gcd(t_dec, RET_CHUNK), DEC_ROWS // t_dec)
    head_of_lane = np.arange(256) // HEAD_DIM
    bd = jnp.asarray((head_of_lane[:, None] == head_of_lane[None, :]).astype(np.float32), dtype=BF16)
    return dict(ret_cos=ret_cos, ret_sin=ret_sin, att_cos=att_cos, att_sin=att_sin,
                dmask_p=dmask_p, qdec_p=qdec_p, kdec_p=kdec_p, cdec_p=cdec_p,
                dmask_d=dmask_d, qdec_d=qdec_d, kdec_d=kdec_d, cdec_d=cdec_d, bd=bd)


def kernel(x_prompt, x_sample, state_ret, cache_k, cache_v, ln_ret, w_ret_in, w_ret_out, ln_ffn, w_ffn_in,
           w_ffn_out, ln_kv, w_kv, k_norm, ln_attn, w_q, q_norm, sinks, w_o):
    B, L, D = x_prompt.shape
    DB, t_dec, _ = x_sample.shape
    n_ret = w_ret_in.shape[0]
    n_attn = w_q.shape[0]
    TP, TD = B * L, DB * t_dec
    T = TP + TD
    wc = cache_k.shape[1]
    kw = N_KV_HEADS * HEAD_DIM
    assert D == D_MODEL and L % RET_CHUNK == 0 and L % ROW_BLOCK == 0 and TD % ROW_BLOCK == 0
    assert t_dec == math.gcd(t_dec, RET_CHUNK) and DEC_ROWS % t_dec == 0 and wc == WINDOW and PAST_LEN >= WINDOW

    tabs = _tables(L, TD, t_dec, RET_CHUNK)
    x = jnp.concatenate([x_prompt.reshape(TP, D), x_sample.reshape(TD, D)], axis=0)
    ck = cache_k.reshape(DB, wc, kw)
    cv = cache_v.reshape(DB, wc, kw)
    q_norm_w = jnp.tile(q_norm, (1, N_HEADS))[:, None, :]
    k_norm_w = jnp.tile(k_norm[None, :], (1, N_KV_HEADS))
    ln_ret, ln_ffn, ln_attn = ln_ret[:, None, :], ln_ffn[:, None, :], ln_attn[:, None, :]

    prompt_states = []
    dec_states = None
    for l in range(n_ret):
        proj = _rms_matmul(x, ln_ret, w_ret_in, l, 512, BF16, f"ret_in_{l}")
        go, s_p = _ret_prompt(proj, tabs, B, L, T, f"ret_prompt_{l}")
        go, dec_states = _ret_decode(proj, state_ret, l, go, dec_states, tabs, L, TP, DB, t_dec, f"ret_decode_{l}")
        prompt_states.append(s_p)
        x = _matmul_residual(go, w_ret_out, l, x, f"ret_out_{l}")
        h = _rms_swiglu(x, ln_ffn, w_ffn_in, l, f"ffn_in_{l}")
        x = _matmul_residual(h, w_ffn_out, l, x, f"ffn_out_{l}")

    kv = _rms_matmul(x, ln_kv[None, None, :], w_kv[None], 0, 2 * kw, F32, "kv_proj")
    kn, klo, khi = _kv_post(kv, k_norm_w, tabs, L, TP, "kv_post")
    for j in range(n_attn):
        layer = n_ret + j
        q = _q_proj(x, ln_attn, w_q, q_norm_w, j, tabs, L, TP, f"q_proj_{j}")
        ao = _attn_prompt(q, klo, khi, kv, sinks, j, B, L, T, f"attn_prompt_{j}")
        ao = _attn_decode(q, kn, kv, ck, cv, sinks, j, ao, TP, DB, t_dec, f"attn_decode_{j}")
        x = _matmul_residual(ao, w_o, j, x, f"attn_out_{j}")
        h = _rms_swiglu(x, ln_ffn, w_ffn_in, layer, f"ffn_in_{layer}")
        x = _matmul_residual(h, w_ffn_out, layer, x, f"ffn_out_{layer}")

    y_prompt = x[:TP].reshape(B, L, D)
    y_sample = x[TP:].reshape(DB, t_dec, D)
    state_prompt = jnp.stack(prompt_states)
    w_keep = min(WINDOW, L)
    kn_p = kn[:TP].reshape(B, L, N_KV_HEADS, HEAD_DIM)
    v_p = kv[:TP, kw:].reshape(B, L, N_KV_HEADS, HEAD_DIM)
    kn_d = kn[TP:].reshape(DB, t_dec, N_KV_HEADS, HEAD_DIM)
    v_d = kv[TP:, kw:].reshape(DB, t_dec, N_KV_HEADS, HEAD_DIM)
    cache_k_sample = jnp.concatenate([cache_k, kn_d], axis=1)[:, -wc:]
    cache_v_sample = jnp.concatenate([cache_v, v_d], axis=1)[:, -wc:]
    return (y_prompt, y_sample, state_prompt, dec_states, kn_p[:, -w_keep:], v_p[:, -w_keep:],
            cache_k_sample, cache_v_sample)
```

```python
import functools
import math

import numpy as np
import jax
import jax.numpy as jnp
from jax import lax
from jax.experimental import pallas as pl
from jax.experimental.pallas import tpu as pltpu

D_MODEL = 1024
PAST_LEN = 8192
RET_HEADS = 4
RET_DK = 256
RET_DV = 512
RET_CHUNK = 128
N_HEADS = 16
N_KV_HEADS = 4
HEAD_DIM = 64
GROUP = N_HEADS // N_KV_HEADS
WINDOW = 128
D_FF = 2816
ROPE_THETA = 10000.0
EPS = 1e-6
NEG_INF = -1e30

F32 = jnp.float32
BF16 = jnp.bfloat16

V7X_VMEM_LIMIT_BYTES = 56 * 1024 * 1024
DEC_ROWS = 16
RET_DEC_ROWS = 32
ROW_BLOCK = 512
SUB_ROWS = 256
RET_UNROLL = 4
ATTN_Q_BLOCKS = 2


def _dot(a, b):
    return jnp.dot(a, b, preferred_element_type=F32)


def _dot_nt(a, b):
    return lax.dot_general(a, b, (((1,), (1,)), ((), ())), preferred_element_type=F32)


def _dot_tn(a, b):
    return lax.dot_general(a, b, (((0,), (0,)), ((), ())), preferred_element_type=F32)


def _params(n_axes):
    return pltpu.CompilerParams(dimension_semantics=("arbitrary",) * n_axes,
                                vmem_limit_bytes=V7X_VMEM_LIMIT_BYTES)


def _row_tile(rows, prefs):
    for t in prefs:
        if rows % t == 0:
            return t
    raise ValueError(f"no row tile for {rows}")


def _rms_normed(x, gain_refs):
    ms = jnp.mean(x * x, axis=-1, keepdims=True)
    xh = x * lax.rsqrt(ms + EPS)
    return [(xh * g_ref[...]).astype(BF16) for g_ref in gain_refs]


def _embed_kernel(xp_ref, xs_ref, g_ref, x_ref, xn_ref, *, n_prompt_blocks):
    i = pl.program_id(0)

    @pl.when(i < n_prompt_blocks)
    def _():
        x_ref[...] = xp_ref[...]

    @pl.when(i >= n_prompt_blocks)
    def _():
        x_ref[...] = xs_ref[...]

    xn_ref[...] = _rms_normed(x_ref[...], [g_ref])[0]


def _embed(xp, xs, gain, name):
    TP, D = xp.shape
    TD = xs.shape[0]
    npb = TP // ROW_BLOCK
    T = TP + TD
    g_arr, g_idx = gain
    blk = lambda fn: pl.BlockSpec((ROW_BLOCK, D), fn)
    return pl.pallas_call(
        functools.partial(_embed_kernel, n_prompt_blocks=npb),
        grid=(T // ROW_BLOCK,),
        in_specs=[blk(lambda i: (jnp.minimum(i, npb - 1), 0)),
                  blk(lambda i: (jnp.maximum(i - npb, 0), 0)),
                  pl.BlockSpec((None, 1, D), lambda i: (g_idx, 0, 0))],
        out_specs=[blk(lambda i: (i, 0)), blk(lambda i: (i, 0))],
        out_shape=[jax.ShapeDtypeStruct((T, D), F32), jax.ShapeDtypeStruct((T, D), BF16)],
        compiler_params=_params(1),
        name=name,
    )(xp, xs, g_arr)


def _mm_kernel(x_ref, w_ref, o_ref):
    o_ref[...] = _dot(x_ref[...], w_ref[...].astype(BF16)).astype(o_ref.dtype)


def _matmul(xn, w, layer, tn, out_dtype, name):
    T, K = xn.shape
    N = w.shape[-1]
    tm = _row_tile(T, (1536, 1024, 512))
    return pl.pallas_call(
        _mm_kernel,
        grid=(T // tm, N // tn),
        in_specs=[pl.BlockSpec((tm, K), lambda i, j: (i, 0)),
                  pl.BlockSpec((None, K, tn), lambda i, j: (layer, 0, j))],
        out_specs=pl.BlockSpec((tm, tn), lambda i, j: (i, j)),
        out_shape=jax.ShapeDtypeStruct((T, N), out_dtype),
        compiler_params=_params(2),
        name=name,
    )(xn, w)


def _swiglu_kernel(x_ref, wa_ref, wb_ref, o_ref):
    wa = wa_ref[...].astype(BF16)
    wb = wb_ref[...].astype(BF16)
    for r0 in range(0, x_ref.shape[0], 2 * SUB_ROWS):
        rows = slice(r0, r0 + 2 * SUB_ROWS)
        a = _dot(x_ref[rows, :], wa)
        b = _dot(x_ref[rows, :], wb)
        o_ref[rows, :] = ((a * jax.nn.sigmoid(a)) * b).astype(o_ref.dtype)


def _swiglu(xn, w, layer, name):
    T, K = xn.shape
    tn = 256
    nb = D_FF // tn
    tm = _row_tile(T, (1536, 1024, 512))
    return pl.pallas_call(
        _swiglu_kernel,
        grid=(T // tm, nb),
        in_specs=[pl.BlockSpec((tm, K), lambda i, j: (i, 0)),
                  pl.BlockSpec((None, K, tn), lambda i, j: (layer, 0, j)),
                  pl.BlockSpec((None, K, tn), lambda i, j: (layer, 0, nb + j))],
        out_specs=pl.BlockSpec((tm, tn), lambda i, j: (i, j)),
        out_shape=jax.ShapeDtypeStruct((T, D_FF), BF16),
        compiler_params=_params(2),
        name=name,
    )(xn, w, w)


def _mm_res_kernel(a_ref, w_ref, r_ref, *rest, n_gains, n_prompt_blocks):
    gain_refs = rest[:n_gains]
    out_refs = rest[n_gains:-1]
    wb_ref = rest[-1]
    i = pl.program_id(0)

    @pl.when(i == 0)
    def _():
        wb_ref[...] = w_ref[...].astype(BF16)

    for r0 in range(0, a_ref.shape[0], SUB_ROWS):
        rows = slice(r0, r0 + SUB_ROWS)
        x = r_ref[rows, :] + _dot(a_ref[rows, :], wb_ref[...])
        if n_prompt_blocks is None:
            out_refs[0][rows, :] = x
            for o_ref, xn in zip(out_refs[1:], _rms_normed(x, gain_refs)):
                o_ref[rows, :] = xn
        else:
            @pl.when(i < n_prompt_blocks)
            def _():
                out_refs[0][rows, :] = x

            @pl.when(i >= n_prompt_blocks)
            def _():
                out_refs[1][rows, :] = x


def _matmul_residual(a, w, layer, res, gains, name, split_rows=None):
    T, K = a.shape
    N = w.shape[-1]
    tm = ROW_BLOCK
    row = pl.BlockSpec((tm, N), lambda i: (i, 0))
    in_specs = [pl.BlockSpec((tm, K), lambda i: (i, 0)),
                pl.BlockSpec((None, K, N), lambda i: (layer, 0, 0)),
                row]
    in_specs += [pl.BlockSpec((None, 1, N), lambda i, idx=idx: (idx, 0, 0)) for _, idx in gains]
    if split_rows is None:
        npb = None
        out_specs = [row] * (1 + len(gains))
        out_shape = [jax.ShapeDtypeStruct((T, N), F32)] + [jax.ShapeDtypeStruct((T, N), BF16)] * len(gains)
        aliases = {2: 0}
    else:
        assert not gains
        npb = split_rows // tm
        out_specs = [pl.BlockSpec((tm, N), lambda i: (jnp.minimum(i, npb - 1), 0)),
                     pl.BlockSpec((tm, N), lambda i: (jnp.maximum(i - npb, 0), 0))]
        out_shape = [jax.ShapeDtypeStruct((split_rows, N), F32), jax.ShapeDtypeStruct((T - split_rows, N), F32)]
        aliases = {}
    return pl.pallas_call(
        functools.partial(_mm_res_kernel, n_gains=len(gains), n_prompt_blocks=npb),
        grid=(T // tm,),
        in_specs=in_specs,
        out_specs=out_specs,
        out_shape=out_shape,
        scratch_shapes=[pltpu.VMEM((K, N), BF16)],
        input_output_aliases=aliases,
        compiler_params=_params(1),
        name=name,
    )(a, w, res, *[g for g, _ in gains])


def _rope_full_head(x, cos, sin):
    x1, x2 = x[:, :RET_DK // 2], x[:, RET_DK // 2:]
    return jnp.concatenate([x1 * cos - x2 * sin, x2 * cos + x1 * sin], axis=1)


def _groupnorm_gate(o, gate):
    mu = jnp.mean(o, axis=-1, keepdims=True)
    d = o - mu
    var = jnp.mean(d * d, axis=-1, keepdims=True)
    on = d * lax.rsqrt(var + EPS)
    return (gate * jax.nn.sigmoid(gate)) * on


def _ret_prompt_kernel(q_ref, k_ref, v_ref, g_ref, cos_ref, sin_ref, dmask_ref, qdec_ref, kdec_ref,
                       cdec_ref, go_ref, sfin_ref, s_ref, *, n_chunks):
    c = pl.program_id(2)

    @pl.when(c == 0)
    def _():
        s_ref[...] = jnp.zeros_like(s_ref)

    dmask = dmask_ref[...]
    qdec = qdec_ref[...]
    kdec = kdec_ref[...]
    cdec = cdec_ref[...]

    def chunk(ci, carry):
        rows = pl.ds(pl.multiple_of(ci * RET_CHUNK, RET_CHUNK), RET_CHUNK)
        cos = cos_ref[rows, :]
        sin = sin_ref[rows, :]
        q = _rope_full_head(q_ref[rows, :].astype(F32), cos, sin)
        k = _rope_full_head(k_ref[rows, :].astype(F32), cos, sin) * (RET_DK ** -0.5)
        qb = q.astype(BF16)
        kb = k.astype(BF16)
        vb = v_ref[rows, :].astype(BF16)
        s = s_ref[...]
        scores = _dot_nt(qb, kb) * dmask
        inner = _dot(scores.astype(BF16), vb)
        cross = _dot(qb, s.astype(BF16)) * qdec
        s_ref[...] = s * cdec + _dot_tn((k * kdec).astype(BF16), vb)
        go_ref[rows, :] = _groupnorm_gate(inner + cross, g_ref[rows, :].astype(F32)).astype(go_ref.dtype)
        return carry

    lax.fori_loop(0, n_chunks, chunk, 0, unroll=min(n_chunks, RET_UNROLL))

    @pl.when(c == pl.num_programs(2) - 1)
    def _():
        sfin_ref[...] = s_ref[...]


def _ret_prompt(proj, tabs, B, L, T, name):
    rb = _row_tile(L, (1024, 512, 128))
    nblk = L // rb
    qcol, vcol = RET_HEADS, RET_HEADS
    gcol = 2 * RET_HEADS
    row = lambda b, h, c: b * nblk + c
    kern = functools.partial(_ret_prompt_kernel, n_chunks=rb // RET_CHUNK)
    return pl.pallas_call(
        kern,
        grid=(B, RET_HEADS, nblk),
        in_specs=[pl.BlockSpec((rb, RET_DK), lambda b, h, c: (row(b, h, c), h)),
                  pl.BlockSpec((rb, RET_DK), lambda b, h, c: (row(b, h, c), qcol + h)),
                  pl.BlockSpec((rb, RET_DV), lambda b, h, c: (row(b, h, c), vcol + h)),
                  pl.BlockSpec((rb, RET_DV), lambda b, h, c: (row(b, h, c), gcol + h)),
                  pl.BlockSpec((rb, RET_DK // 2), lambda b, h, c: (c, 0)),
                  pl.BlockSpec((rb, RET_DK // 2), lambda b, h, c: (c, 0)),
                  pl.BlockSpec((None, RET_CHUNK, RET_CHUNK), lambda b, h, c: (h, 0, 0)),
                  pl.BlockSpec((None, RET_CHUNK, 1), lambda b, h, c: (h, 0, 0)),
                  pl.BlockSpec((None, RET_CHUNK, 1), lambda b, h, c: (h, 0, 0)),
                  pl.BlockSpec((None, 1, 1), lambda b, h, c: (h, 0, 0))],
        out_specs=[pl.BlockSpec((rb, RET_DV), lambda b, h, c: (row(b, h, c), h)),
                   pl.BlockSpec((None, None, RET_DK, RET_DV), lambda b, h, c: (b, h, 0, 0))],
        out_shape=[jax.ShapeDtypeStruct((T, RET_HEADS * RET_DV), BF16),
                   jax.ShapeDtypeStruct((B, RET_HEADS, RET_DK, RET_DV), F32)],
        scratch_shapes=[pltpu.VMEM((RET_DK, RET_DV), F32)],
        compiler_params=_params(3),
        name=name,
    )(proj, proj, proj, proj, tabs["ret_cos"], tabs["ret_sin"], tabs["dmask_p"], tabs["qdec_p"],
      tabs["kdec_p"], tabs["cdec_p"])


def _ret_decode_kernel(q_ref, k_ref, v_ref, g_ref, cos_ref, sin_ref, dmask_ref, qdec_ref, kdec_ref,
                       cdec_ref, s_ref, *rest, t_dec):
    go_ref, snew_ref = rest[-2:]
    cos = cos_ref[...]
    sin = sin_ref[...]
    q = _rope_full_head(q_ref[...].astype(F32), cos, sin)
    k = _rope_full_head(k_ref[...].astype(F32), cos, sin) * (RET_DK ** -0.5)
    qb = q.astype(BF16)
    kb = k.astype(BF16)
    vb = v_ref[...].astype(BF16)
    cdec = cdec_ref[...]
    scores = _dot_nt(qb, kb) * dmask_ref[...]
    inner = _dot(scores.astype(BF16), vb)
    kd = k * kdec_ref[...]
    seq_of_row = lax.broadcasted_iota(jnp.int32, (RET_DEC_ROWS, 1), 0) // t_dec
    cross = jnp.zeros((RET_DEC_ROWS, RET_DV), F32)
    for bi in range(RET_DEC_ROWS // t_dec):
        mine = seq_of_row == bi
        s = s_ref[bi]
        cross = cross + jnp.where(mine, _dot(qb, s.astype(BF16)), 0.0)
        snew_ref[bi] = s * cdec + _dot_tn(jnp.where(mine, kd, 0.0).astype(BF16), vb)
    cross = cross * qdec_ref[...]
    go_ref[...] = _groupnorm_gate(inner + cross, g_ref[...].astype(F32)).astype(go_ref.dtype)


def _ret_decode(proj, state, layer, go_joint, snew_joint, tabs, L, TP, DB, t_dec, name):
    nseq = RET_DEC_ROWS // t_dec
    r0 = TP // RET_DEC_ROWS
    qcol = vcol = RET_HEADS
    gcol = 2 * RET_HEADS
    kern = functools.partial(_ret_decode_kernel, t_dec=t_dec)
    s_spec = pl.BlockSpec((None, nseq, None, RET_DK, RET_DV), lambda i, h: (layer, i, h, 0, 0))
    any_spec = pl.BlockSpec(memory_space=pl.ANY)
    aliased = [go_joint] if snew_joint is None else [go_joint, snew_joint]
    return pl.pallas_call(
        kern,
        grid=(DB // nseq, RET_HEADS),
        in_specs=[pl.BlockSpec((RET_DEC_ROWS, RET_DK), lambda i, h: (r0 + i, h)),
                  pl.BlockSpec((RET_DEC_ROWS, RET_DK), lambda i, h: (r0 + i, qcol + h)),
                  pl.BlockSpec((RET_DEC_ROWS, RET_DV), lambda i, h: (r0 + i, vcol + h)),
                  pl.BlockSpec((RET_DEC_ROWS, RET_DV), lambda i, h: (r0 + i, gcol + h)),
                  pl.BlockSpec((RET_DEC_ROWS, RET_DK // 2), lambda i, h: (L // RET_DEC_ROWS, 0)),
                  pl.BlockSpec((RET_DEC_ROWS, RET_DK // 2), lambda i, h: (L // RET_DEC_ROWS, 0)),
                  pl.BlockSpec((None, RET_DEC_ROWS, RET_DEC_ROWS), lambda i, h: (h, 0, 0)),
                  pl.BlockSpec((None, RET_DEC_ROWS, 1), lambda i, h: (h, 0, 0)),
                  pl.BlockSpec((None, RET_DEC_ROWS, 1), lambda i, h: (h, 0, 0)),
                  pl.BlockSpec((None, 1, 1), lambda i, h: (h, 0, 0)),
                  s_spec] + [any_spec] * len(aliased),
        out_specs=[pl.BlockSpec((RET_DEC_ROWS, RET_DV), lambda i, h: (r0 + i, h)), s_spec],
        out_shape=[jax.ShapeDtypeStruct(go_joint.shape, go_joint.dtype),
                   jax.ShapeDtypeStruct(state.shape, state.dtype)],
        input_output_aliases={11 + n: n for n in range(len(aliased))},
        compiler_params=_params(2),
        name=name,
    )(proj, proj, proj, proj, tabs["ret_cos"], tabs["ret_sin"], tabs["dmask_d"], tabs["qdec_d"],
      tabs["kdec_d"], tabs["cdec_d"], state, *aliased)


def _headnorm_rope(x, w, cos2, sin2, bd):
    width = x.shape[1]
    sq = x * x
    hi = sq.astype(BF16)
    lo = (sq - hi.astype(F32)).astype(BF16)
    parts = []
    for c in range(width // 256):
        sl = slice(256 * c, 256 * (c + 1))
        parts.append(_dot(hi[:, sl], bd) + _dot(lo[:, sl], bd))
    ss = parts[0] if len(parts) == 1 else jnp.concatenate(parts, axis=1)
    y = (x * lax.rsqrt(ss * (1.0 / HEAD_DIM) + EPS)) * w
    lane = lax.broadcasted_iota(jnp.int32, y.shape, 1)
    first_half = (lane & (HEAD_DIM - 1)) < HEAD_DIM // 2
    rot = jnp.where(first_half, pltpu.roll(y, width - HEAD_DIM // 2, 1), pltpu.roll(y, HEAD_DIM // 2, 1))
    nrep = width // 128
    cosw = jnp.concatenate([cos2] * nrep, axis=1)
    sinw = jnp.concatenate([sin2] * nrep, axis=1)
    return y * cosw + rot * sinw


def _kv_post_kernel(k_ref, w_ref, cos_ref, sin_ref, bd_ref, o_ref, lo_ref, hi_ref):
    kn = _headnorm_rope(k_ref[...], w_ref[...], cos_ref[...], sin_ref[...], bd_ref[...])
    o_ref[...] = kn
    low = lax.broadcasted_iota(jnp.int32, (kn.shape[0], 128), 1) < HEAD_DIM
    zero = jnp.zeros((kn.shape[0], 128), F32)
    for c in range(N_KV_HEADS // 2):
        tile = kn[:, 128 * c:128 * (c + 1)]
        swapped = pltpu.roll(tile, HEAD_DIM, 1)
        even, odd = slice(256 * c, 256 * c + 128), slice(256 * c + 128, 256 * (c + 1))
        lo_ref[:, even] = jnp.where(low, tile, zero).astype(lo_ref.dtype)
        hi_ref[:, even] = jnp.where(low, zero, swapped).astype(hi_ref.dtype)
        lo_ref[:, odd] = jnp.where(low, swapped, zero).astype(lo_ref.dtype)
        hi_ref[:, odd] = jnp.where(low, zero, tile).astype(hi_ref.dtype)


def _table_block(i, n_prompt_blocks, blocks_per_seq):
    return jnp.where(i < n_prompt_blocks, i % blocks_per_seq, blocks_per_seq + i - n_prompt_blocks)


def _kv_post(kv, k_norm_w, tabs, L, TP, name):
    T = kv.shape[0]
    kw = N_KV_HEADS * HEAD_DIM
    npb, bps = TP // ROW_BLOCK, L // ROW_BLOCK
    tab = lambda i: (_table_block(i, npb, bps), 0)
    pad_spec = pl.BlockSpec((ROW_BLOCK, 128 * N_KV_HEADS), lambda i: (i, 0))
    pad_shape = jax.ShapeDtypeStruct((T, 128 * N_KV_HEADS), BF16)
    return pl.pallas_call(
        _kv_post_kernel,
        grid=(T // ROW_BLOCK,),
        in_specs=[pl.BlockSpec((ROW_BLOCK, kw), lambda i: (i, 0)),
                  pl.BlockSpec((1, kw), lambda i: (0, 0)),
                  pl.BlockSpec((ROW_BLOCK, 128), tab),
                  pl.BlockSpec((ROW_BLOCK, 128), tab),
                  pl.BlockSpec((256, 256), lambda i: (0, 0))],
        out_specs=[pl.BlockSpec((ROW_BLOCK, kw), lambda i: (i, 0)), pad_spec, pad_spec],
        out_shape=[jax.ShapeDtypeStruct((T, kw), F32), pad_shape, pad_shape],
        compiler_params=_params(1),
        name=name,
    )(kv, k_norm_w, tabs["att_cos"], tabs["att_sin"], tabs["bd"])


def _q_proj_kernel(x_ref, w_ref, qw_ref, cos_ref, sin_ref, bd_ref, o_ref, wb_ref):
    @pl.when(pl.program_id(0) == 0)
    def _():
        wb_ref[...] = w_ref[...].astype(BF16)

    bd = bd_ref[...]
    for r0 in range(0, x_ref.shape[0], 128):
        rows = slice(r0, r0 + 128)
        q = _dot(x_ref[rows, :], wb_ref[...])
        cos2, sin2 = cos_ref[rows, :], sin_ref[rows, :]
        for c0 in range(0, q.shape[1], 256):
            cols = slice(c0, c0 + 256)
            qc = _headnorm_rope(q[:, cols], qw_ref[:, cols], cos2, sin2, bd)
            o_ref[rows, cols] = (qc * (HEAD_DIM ** -0.5)).astype(o_ref.dtype)


def _q_proj(xn, w, q_norm_w, layer, tabs, L, TP, name):
    T, K = xn.shape
    N = w.shape[-1]
    npb, bps = TP // ROW_BLOCK, L // ROW_BLOCK
    tab = lambda i: (_table_block(i, npb, bps), 0)
    return pl.pallas_call(
        _q_proj_kernel,
        grid=(T // ROW_BLOCK,),
        in_specs=[pl.BlockSpec((ROW_BLOCK, K), lambda i: (i, 0)),
                  pl.BlockSpec((None, K, N), lambda i: (layer, 0, 0)),
                  pl.BlockSpec((None, 1, N), lambda i: (layer, 0, 0)),
                  pl.BlockSpec((ROW_BLOCK, 128), tab),
                  pl.BlockSpec((ROW_BLOCK, 128), tab),
                  pl.BlockSpec((256, 256), lambda i: (0, 0))],
        out_specs=pl.BlockSpec((ROW_BLOCK, N), lambda i: (i, 0)),
        out_shape=jax.ShapeDtypeStruct((T, N), BF16),
        scratch_shapes=[pltpu.VMEM((K, N), BF16)],
        compiler_params=_params(1),
        name=name,
    )(xn, w, q_norm_w, tabs["att_cos"], tabs["att_sin"], tabs["bd"])


def _sink_column(sink_ref, layer, g, rows, rows_per_head):
    head_in_group = lax.broadcasted_iota(jnp.int32, (rows, 1), 0) // rows_per_head
    col = jnp.full((rows, 1), sink_ref[layer, g * GROUP], F32)
    for r in range(1, GROUP):
        col = jnp.where(head_in_group == r, sink_ref[layer, g * GROUP + r], col)
    return col


def _attn_prompt_kernel(sink_ref, q_ref, lop_ref, loc_ref, hip_ref, hic_ref, vp_ref, vc_ref, o_ref, *,
                        layer, n_qblk):
    first = pl.program_id(1) == 0
    qb = q_ref[...]
    klo = jnp.concatenate([lop_ref[...], loc_ref[...]], axis=0)
    khi = jnp.concatenate([hip_ref[...], hic_ref[...]], axis=0)
    v_t = jnp.concatenate([vp_ref[...], vc_ref[...]], axis=0).T.astype(BF16)
    key = lax.broadcasted_iota(jnp.int32, (WINDOW, WINDOW), 0)
    qi = lax.broadcasted_iota(jnp.int32, (WINDOW, WINDOW), 1)
    own = key <= qi
    scores = {}
    for t in range(n_qblk):
        qrows = slice(WINDOW * t, WINDOW * (t + 1))
        krows = slice(WINDOW * t, WINDOW * (t + 2))
        for g in range(N_KV_HEADS):
            gl = slice(128 * g, 128 * (g + 1))
            kk = jnp.concatenate([klo[krows, gl], khi[krows, gl]], axis=0)
            xq = jnp.concatenate([qb[qrows, 256 * g:256 * g + 128], qb[qrows, 256 * g + 128:256 * (g + 1)]],
                                 axis=0)
            scores[t, g] = _dot_nt(kk, xq)
    for t in range(n_qblk):
        qrows = slice(WINDOW * t, WINDOW * (t + 1))
        krows = slice(WINDOW * t, WINDOW * (t + 2))
        pieces = []
        for g in range(N_KV_HEADS):
            s4 = scores[t, g]
            v_g = v_t[HEAD_DIM * g:HEAD_DIM * (g + 1), krows]
            for pair in range(GROUP // 2):
                p2s, invs = [], []
                for parity in range(2):
                    sk = sink_ref[layer, g * GROUP + 2 * pair + parity]
                    blk = s4[2 * WINDOW * parity:2 * WINDOW * (parity + 1), WINDOW * pair:WINDOW * (pair + 1)]
                    s_prev = blk[:WINDOW]
                    if t == 0:
                        s_prev = jnp.where(first, NEG_INF, s_prev)
                    s = jnp.where(own, blk[WINDOW:], s_prev)
                    m = jnp.maximum(jnp.max(s, axis=0, keepdims=True), sk)
                    p = jnp.exp(s - m)
                    invs.append(1.0 / (jnp.sum(p, axis=0, keepdims=True) + jnp.exp(sk - m)))
                    p2s.append(jnp.concatenate([jnp.where(own, 0.0, p), jnp.where(own, p, 0.0)], axis=0))
                p2 = jnp.concatenate(p2s, axis=1).astype(BF16)
                o_t = _dot(v_g, p2) * jnp.concatenate(invs, axis=1)
                pieces += [o_t[:, :WINDOW], o_t[:, WINDOW:]]
        o_ref[qrows, :] = jnp.concatenate(pieces, axis=0).T.astype(o_ref.dtype)


def _attn_prompt(q, klo, khi, kv, sinks, layer, B, L, T, name):
    nb = L // WINDOW
    nq = ATTN_Q_BLOCKS if nb % ATTN_Q_BLOCKS == 0 else 1
    ns = nb // nq
    kw = N_KV_HEADS * HEAD_DIM
    kpad = 128 * N_KV_HEADS
    cur = lambda b, i: b * ns + i
    prev = lambda b, i: b * nb + jnp.maximum(i * nq - 1, 0)
    kern = functools.partial(_attn_prompt_kernel, layer=layer, n_qblk=nq)
    return pl.pallas_call(
        kern,
        grid=(B, ns),
        in_specs=[pl.BlockSpec(memory_space=pltpu.SMEM),
                  pl.BlockSpec((nq * WINDOW, D_MODEL), lambda b, i: (cur(b, i), 0)),
                  pl.BlockSpec((WINDOW, kpad), lambda b, i: (prev(b, i), 0)),
                  pl.BlockSpec((nq * WINDOW, kpad), lambda b, i: (cur(b, i), 0)),
                  pl.BlockSpec((WINDOW, kpad), lambda b, i: (prev(b, i), 0)),
                  pl.BlockSpec((nq * WINDOW, kpad), lambda b, i: (cur(b, i), 0)),
                  pl.BlockSpec((WINDOW, kw), lambda b, i: (prev(b, i), 1)),
                  pl.BlockSpec((nq * WINDOW, kw), lambda b, i: (cur(b, i), 1))],
        out_specs=pl.BlockSpec((nq * WINDOW, D_MODEL), lambda b, i: (cur(b, i), 0)),
        out_shape=jax.ShapeDtypeStruct((T, D_MODEL), BF16),
        compiler_params=_params(2),
        name=name,
    )(sinks, q, klo, klo, khi, khi, kv, kv)


def _attn_decode_kernel(sink_ref, q_ref, kn_ref, vn_ref, ck_ref, cv_ref, ao_in_ref, o_ref, *, layer, t_dec):
    del ao_in_ref
    nseq = DEC_ROWS // t_dec
    wc = ck_ref.shape[1]
    qb = q_ref[...]
    knew = kn_ref[...].astype(BF16)
    vnew = vn_ref[...].astype(BF16)
    rows = GROUP * DEC_ROWS
    row = lax.broadcasted_iota(jnp.int32, (rows, 1), 0)
    row_seq = (row % DEC_ROWS) // t_dec
    row_tok = row % t_dec
    jold = lax.broadcasted_iota(jnp.int32, (rows, wc), 1)
    vis_old = (jold > row_tok + (wc - WINDOW)) & (jold <= row_tok + wc)
    cnew = lax.broadcasted_iota(jnp.int32, (rows, DEC_ROWS), 1)
    vis_new = (cnew // t_dec == row_seq) & (cnew % t_dec <= row_tok)
    for g in range(N_KV_HEADS):
        heads = [g * GROUP + r for r in range(GROUP)]
        hs = slice(HEAD_DIM * g, HEAD_DIM * (g + 1))
        qs = jnp.concatenate([qb[:, HEAD_DIM * h:HEAD_DIM * (h + 1)] for h in heads], axis=0)
        s_old = jnp.zeros((rows, wc), F32)
        for bi in range(nseq):
            kc = ck_ref[bi][:, hs].astype(BF16)
            s_old = s_old + jnp.where(row_seq == bi, _dot_nt(qs, kc), 0.0)
        s_old = jnp.where(vis_old, s_old, NEG_INF)
        s_new = jnp.where(vis_new, _dot_nt(qs, knew[:, hs]), NEG_INF)
        sk = _sink_column(sink_ref, layer, g, rows, DEC_ROWS)
        m = jnp.maximum(jnp.maximum(jnp.max(s_old, axis=-1, keepdims=True),
                                    jnp.max(s_new, axis=-1, keepdims=True)), sk)
        p_old = jnp.exp(s_old - m)
        p_new = jnp.exp(s_new - m)
        denom = (jnp.sum(p_old, axis=-1, keepdims=True) + jnp.sum(p_new, axis=-1, keepdims=True)
                 + jnp.exp(sk - m))
        inv = 1.0 / denom
        o = _dot((p_new * inv).astype(BF16), vnew[:, hs])
        pn_old = p_old * inv
        for bi in range(nseq):
            vc = cv_ref[bi][:, hs].astype(BF16)
            o = o + _dot(jnp.where(row_seq == bi, pn_old, 0.0).astype(BF16), vc)
        for r, h in enumerate(heads):
            o_ref[:, HEAD_DIM * h:HEAD_DIM * (h + 1)] = o[DEC_ROWS * r:DEC_ROWS * (r + 1)].astype(o_ref.dtype)


def _attn_decode(q, kn, kv, cache_k, cache_v, sinks, layer, ao_joint, TP, DB, t_dec, name):
    nseq = DEC_ROWS // t_dec
    r0 = TP // DEC_ROWS
    kw = N_KV_HEADS * HEAD_DIM
    wc = cache_k.shape[1]
    kern = functools.partial(_attn_decode_kernel, layer=layer, t_dec=t_dec)
    return pl.pallas_call(
        kern,
        grid=(DB // nseq,),
        in_specs=[pl.BlockSpec(memory_space=pltpu.SMEM),
                  pl.BlockSpec((DEC_ROWS, D_MODEL), lambda i: (r0 + i, 0)),
                  pl.BlockSpec((DEC_ROWS, kw), lambda i: (r0 + i, 0)),
                  pl.BlockSpec((DEC_ROWS, kw), lambda i: (r0 + i, 1)),
                  pl.BlockSpec((nseq, wc, kw), lambda i: (i, 0, 0)),
                  pl.BlockSpec((nseq, wc, kw), lambda i: (i, 0, 0)),
                  pl.BlockSpec(memory_space=pl.ANY)],
        out_specs=pl.BlockSpec((DEC_ROWS, D_MODEL), lambda i: (r0 + i, 0)),
        out_shape=jax.ShapeDtypeStruct(ao_joint.shape, ao_joint.dtype),
        input_output_aliases={6: 0},
        compiler_params=_params(1),
        name=name,
    )(sinks, q, kn, kv, cache_k, cache_v, ao_joint)


def _rope_cos_sin(pos, half):
    inv = 1.0 / (ROPE_THETA ** (jnp.arange(half, dtype=F32) / half))
    ang = pos.astype(F32)[:, None] * inv[None, :]
    return jnp.cos(ang), jnp.sin(ang)


def _decay_tables(chunk, reps):
    lg = jnp.log(1.0 - 2.0 ** (-5.0 - jnp.arange(RET_HEADS, dtype=F32)))
    idx = jnp.arange(chunk, dtype=F32)
    rel = idx[:, None] - idx[None, :]
    dmask = jnp.where(rel >= 0, jnp.exp(lg[:, None, None] * jnp.maximum(rel, 0.0)), 0.0)
    qdec = jnp.exp(lg[:, None] * (idx + 1.0))[:, :, None]
    kdec = jnp.exp(lg[:, None] * (chunk - 1.0 - idx))[:, :, None]
    cdec = jnp.exp(lg * chunk)[:, None, None]
    if reps > 1:
        eye = jnp.eye(reps, dtype=F32)
        dmask = jnp.einsum("ab,hij->haibj", eye, dmask).reshape(RET_HEADS, reps * chunk, reps * chunk)
        qdec = jnp.tile(qdec, (1, reps, 1))
        kdec = jnp.tile(kdec, (1, reps, 1))
    return dmask, qdec, kdec, cdec


def _tables(L, TD, t_dec, chunk_p):
    pos = jnp.concatenate([jnp.arange(L), PAST_LEN + jnp.arange(TD) % t_dec])
    ret_cos, ret_sin = _rope_cos_sin(pos, RET_DK // 2)
    c, s = _rope_cos_sin(pos, HEAD_DIM // 2)
    att_cos = jnp.concatenate([c, c, c, c], axis=1)
    att_sin = jnp.concatenate([-s, s, -s, s], axis=1)
    dmask_p, qdec_p, kdec_p, cdec_p = _decay_tables(chunk_p, 1)
    dmask_d, qdec_d, kdec_d, cdec_d = _decay_tables(math.gcd(t_dec, RET_CHUNK), RET_DEC_ROWS // t_dec)
    head_of_lane = np.arange(256) // HEAD_DIM
    bd = jnp.asarray((head_of_lane[:, None] == head_of_lane[None, :]).astype(np.float32), dtype=BF16)
    return dict(ret_cos=ret_cos, ret_sin=ret_sin, att_cos=att_cos, att_sin=att_sin,
                dmask_p=dmask_p, qdec_p=qdec_p, kdec_p=kdec_p, cdec_p=cdec_p,
                dmask_d=dmask_d, qdec_d=qdec_d, kdec_d=kdec_d, cdec_d=cdec_d, bd=bd)


def kernel(x_prompt, x_sample, state_ret, cache_k, cache_v, ln_ret, w_ret_in, w_ret_out, ln_ffn, w_ffn_in,
           w_ffn_out, ln_kv, w_kv, k_norm, ln_attn, w_q, q_norm, sinks, w_o):
    B, L, D = x_prompt.shape
    DB, t_dec, _ = x_sample.shape
    n_ret = w_ret_in.shape[0]
    n_attn = w_q.shape[0]
    TP, TD = B * L, DB * t_dec
    T = TP + TD
    wc = cache_k.shape[1]
    kw = N_KV_HEADS * HEAD_DIM
    assert D == D_MODEL and L % RET_CHUNK == 0 and L % ROW_BLOCK == 0 and TD % ROW_BLOCK == 0
    assert t_dec == math.gcd(t_dec, RET_CHUNK) and DEC_ROWS % t_dec == 0 and wc == WINDOW and PAST_LEN >= WINDOW

    tabs = _tables(L, TD, t_dec, RET_CHUNK)
    ck = cache_k.reshape(DB, wc, kw)
    cv = cache_v.reshape(DB, wc, kw)
    q_norm_w = jnp.tile(q_norm, (1, N_HEADS))[:, None, :]
    k_norm_w = jnp.tile(k_norm[None, :], (1, N_KV_HEADS))
    ln_ret, ln_ffn, ln_attn = ln_ret[:, None, :], ln_ffn[:, None, :], ln_attn[:, None, :]

    ln_kv3 = ln_kv[None, None, :]
    prompt_states = []
    dec_states = None
    x, xn = _embed(x_prompt.reshape(TP, D), x_sample.reshape(TD, D), (ln_ret, 0), "embed")
    for l in range(n_ret):
        proj = _matmul(xn, w_ret_in, l, 1024, BF16, f"ret_in_{l}")
        go, s_p = _ret_prompt(proj, tabs, B, L, T, f"ret_prompt_{l}")
        go, dec_states = _ret_decode(proj, state_ret, l, go, dec_states, tabs, L, TP, DB, t_dec, f"ret_decode_{l}")
        prompt_states.append(s_p)
        x, xn = _matmul_residual(go, w_ret_out, l, x, [(ln_ffn, l)], f"ret_out_{l}")
        h = _swiglu(xn, w_ffn_in, l, f"ffn_in_{l}")
        if l + 1 < n_ret:
            x, xn = _matmul_residual(h, w_ffn_out, l, x, [(ln_ret, l + 1)], f"ffn_out_{l}")
        else:
            x, xn_kv, xn = _matmul_residual(h, w_ffn_out, l, x, [(ln_kv3, 0), (ln_attn, 0)], f"ffn_out_{l}")

    kv = _matmul(xn_kv, w_kv[None], 0, 2 * kw, F32, "kv_proj")
    kn, klo, khi = _kv_post(kv, k_norm_w, tabs, L, TP, "kv_post")
    for j in range(n_attn):
        layer = n_ret + j
        q = _q_proj(xn, w_q, q_norm_w, j, tabs, L, TP, f"q_proj_{j}")
        ao = _attn_prompt(q, klo, khi, kv, sinks, j, B, L, T, f"attn_prompt_{j}")
        ao = _attn_decode(q, kn, kv, ck, cv, sinks, j, ao, TP, DB, t_dec, f"attn_decode_{j}")
        x, xn = _matmul_residual(ao, w_o, j, x, [(ln_ffn, layer)], f"attn_out_{j}")
        h = _swiglu(xn, w_ffn_in, layer, f"ffn_in_{layer}")
        if j + 1 < n_attn:
            x, xn = _matmul_residual(h, w_ffn_out, layer, x, [(ln_attn, j + 1)], f"ffn_out_{layer}")
        else:
            y_p, y_s = _matmul_residual(h, w_ffn_out, layer, x, [], f"ffn_out_{layer}", split_rows=TP)

    y_prompt = y_p.reshape(B, L, D)
    y_sample = y_s.reshape(DB, t_dec, D)
    state_prompt = jnp.stack(prompt_states)
    w_keep = min(WINDOW, L)
    kn_p = kn[:TP].reshape(B, L, N_KV_HEADS, HEAD_DIM)
    v_p = kv[:TP, kw:].reshape(B, L, N_KV_HEADS, HEAD_DIM)
    kn_d = kn[TP:].reshape(DB, t_dec, N_KV_HEADS, HEAD_DIM)
    v_d = kv[TP:, kw:].reshape(DB, t_dec, N_KV_HEADS, HEAD_DIM)
    cache_k_sample = jnp.concatenate([cache_k, kn_d], axis=1)[:, -wc:]
    cache_v_sample = jnp.concatenate([cache_v, v_d], axis=1)[:, -wc:]
    return (y_prompt, y_sample, state_prompt, dec_states, kn_p[:, -w_keep:], v_p[:, -w_keep:],
            cache_k_sample, cache_v_sample)
```

```python
import functools
import math

import numpy as np
import jax
import jax.numpy as jnp
from jax import lax
from jax.experimental import pallas as pl
from jax.experimental.pallas import tpu as pltpu

D_MODEL = 1024
PAST_LEN = 8192
RET_HEADS = 4
RET_DK = 256
RET_DV = 512
RET_CHUNK = 128
RET_CHUNK_PROMPT = 256
N_HEADS = 16
N_KV_HEADS = 4
HEAD_DIM = 64
GROUP = N_HEADS // N_KV_HEADS
WINDOW = 128
D_FF = 2816
ROPE_THETA = 10000.0
EPS = 1e-6
NEG_INF = -1e30

F32 = jnp.float32
BF16 = jnp.bfloat16

V7X_VMEM_LIMIT_BYTES = 56 * 1024 * 1024
DEC_ROWS = 16
RET_DEC_ROWS = 32
ROW_BLOCK = 512
SUB_ROWS = 256
RET_UNROLL = 4
ATTN_Q_BLOCKS = 2


def _dot(a, b):
    return jnp.dot(a, b, preferred_element_type=F32)


def _dot_nt(a, b):
    return lax.dot_general(a, b, (((1,), (1,)), ((), ())), preferred_element_type=F32)


def _dot_tn(a, b):
    return lax.dot_general(a, b, (((0,), (0,)), ((), ())), preferred_element_type=F32)


def _params(n_axes):
    return pltpu.CompilerParams(dimension_semantics=("arbitrary",) * n_axes,
                                vmem_limit_bytes=V7X_VMEM_LIMIT_BYTES)


def _row_tile(rows, prefs):
    for t in prefs:
        if rows % t == 0:
            return t
    raise ValueError(f"no row tile for {rows}")


def _rms_normed(x, gain_refs):
    ms = jnp.mean(x * x, axis=-1, keepdims=True)
    xh = x * lax.rsqrt(ms + EPS)
    return [(xh * g_ref[...]).astype(BF16) for g_ref in gain_refs]


def _embed_kernel(xp_ref, xs_ref, g_ref, x_ref, xn_ref, *, n_prompt_blocks):
    i = pl.program_id(0)

    @pl.when(i < n_prompt_blocks)
    def _():
        x_ref[...] = xp_ref[...]

    @pl.when(i >= n_prompt_blocks)
    def _():
        x_ref[...] = xs_ref[...]

    xn_ref[...] = _rms_normed(x_ref[...], [g_ref])[0]


def _embed(xp, xs, gain, name):
    TP, D = xp.shape
    TD = xs.shape[0]
    npb = TP // ROW_BLOCK
    T = TP + TD
    g_arr, g_idx = gain
    blk = lambda fn: pl.BlockSpec((ROW_BLOCK, D), fn)
    return pl.pallas_call(
        functools.partial(_embed_kernel, n_prompt_blocks=npb),
        grid=(T // ROW_BLOCK,),
        in_specs=[blk(lambda i: (jnp.minimum(i, npb - 1), 0)),
                  blk(lambda i: (jnp.maximum(i - npb, 0), 0)),
                  pl.BlockSpec((None, 1, D), lambda i: (g_idx, 0, 0))],
        out_specs=[blk(lambda i: (i, 0)), blk(lambda i: (i, 0))],
        out_shape=[jax.ShapeDtypeStruct((T, D), F32), jax.ShapeDtypeStruct((T, D), BF16)],
        compiler_params=_params(1),
        name=name,
    )(xp, xs, g_arr)


def _sub_rows(tm):
    return max(r for r in range(16, 3 * SUB_ROWS + 1, 16) if tm % r == 0)


def _mm_kernel(x_ref, w_ref, o_ref):
    o_ref[...] = _dot(x_ref[...], w_ref[...].astype(BF16)).astype(o_ref.dtype)


def _matmul(xn, w, layer, tn, out_dtype, name):
    T, K = xn.shape
    N = w.shape[-1]
    tm = _row_tile(T, (1536, 1024, 512))
    return pl.pallas_call(
        _mm_kernel,
        grid=(T // tm, N // tn),
        in_specs=[pl.BlockSpec((tm, K), lambda i, j: (i, 0)),
                  pl.BlockSpec((None, K, tn), lambda i, j: (layer, 0, j))],
        out_specs=pl.BlockSpec((tm, tn), lambda i, j: (i, j)),
        out_shape=jax.ShapeDtypeStruct((T, N), out_dtype),
        compiler_params=_params(2),
        name=name,
    )(xn, w)


def _ret_in_kernel(x_ref, w_ref, cos_ref, sin_ref, o_ref, *, n_q_tiles):
    j = pl.program_id(1)
    sub = _sub_rows(x_ref.shape[0])
    half = RET_DK // 2

    @pl.when(j >= 2 * n_q_tiles)
    def _():
        o_ref[...] = _dot(x_ref[...], w_ref[...].astype(BF16)).astype(o_ref.dtype)

    @pl.when(j < 2 * n_q_tiles)
    def _():
        scale = jnp.where(j < n_q_tiles, 1.0, RET_DK ** -0.5)
        w = w_ref[...].astype(BF16)
        for r0 in range(0, x_ref.shape[0], sub):
            rows = slice(r0, r0 + sub)
            acc = _dot(x_ref[rows, :], w)
            cos = cos_ref[rows, :]
            sin = sin_ref[rows, :]
            for c0 in range(0, acc.shape[1], RET_DK):
                x1, x2 = acc[:, c0:c0 + half], acc[:, c0 + half:c0 + RET_DK]
                o_ref[rows, c0:c0 + half] = ((x1 * cos - x2 * sin) * scale).astype(o_ref.dtype)
                o_ref[rows, c0 + half:c0 + RET_DK] = ((x2 * cos + x1 * sin) * scale).astype(o_ref.dtype)


def _ret_in(xn, w, layer, tabs, name):
    T, K = xn.shape
    N = w.shape[-1]
    tn = 1024
    tm = _row_tile(T, (1536, 1024, 512))
    return pl.pallas_call(
        functools.partial(_ret_in_kernel, n_q_tiles=RET_HEADS * RET_DK // tn),
        grid=(T // tm, N // tn),
        in_specs=[pl.BlockSpec((tm, K), lambda i, j: (i, 0)),
                  pl.BlockSpec((None, K, tn), lambda i, j: (layer, 0, j)),
                  pl.BlockSpec((tm, RET_DK // 2), lambda i, j: (i, 0)),
                  pl.BlockSpec((tm, RET_DK // 2), lambda i, j: (i, 0))],
        out_specs=pl.BlockSpec((tm, tn), lambda i, j: (i, j)),
        out_shape=jax.ShapeDtypeStruct((T, N), BF16),
        compiler_params=_params(2),
        name=name,
    )(xn, w, tabs["ret_cos_rows"], tabs["ret_sin_rows"])


def _swiglu_kernel(x_ref, wa_ref, wb_ref, o_ref):
    wa = wa_ref[...].astype(BF16)
    wb = wb_ref[...].astype(BF16)
    sub = _sub_rows(x_ref.shape[0])
    for r0 in range(0, x_ref.shape[0], sub):
        rows = slice(r0, r0 + sub)
        a = _dot(x_ref[rows, :], wa)
        b = _dot(x_ref[rows, :], wb)
        o_ref[rows, :] = ((a * jax.nn.sigmoid(a)) * b).astype(o_ref.dtype)


def _swiglu(xn, w, layer, name):
    T, K = xn.shape
    tn = 256
    nb = D_FF // tn
    tm = _row_tile(T, (2816, 1536, 1024, 512))
    return pl.pallas_call(
        _swiglu_kernel,
        grid=(T // tm, nb),
        in_specs=[pl.BlockSpec((tm, K), lambda i, j: (i, 0)),
                  pl.BlockSpec((None, K, tn), lambda i, j: (layer, 0, j)),
                  pl.BlockSpec((None, K, tn), lambda i, j: (layer, 0, nb + j))],
        out_specs=pl.BlockSpec((tm, tn), lambda i, j: (i, j)),
        out_shape=jax.ShapeDtypeStruct((T, D_FF), BF16),
        compiler_params=_params(2),
        name=name,
    )(xn, w, w)


def _mm_res_kernel(a_ref, w_ref, r_ref, *rest, n_gains, n_prompt_blocks):
    gain_refs = rest[:n_gains]
    out_refs = rest[n_gains:-1]
    wb_ref = rest[-1]
    i = pl.program_id(0)

    @pl.when(i == 0)
    def _():
        wb_ref[...] = w_ref[...].astype(BF16)

    for r0 in range(0, a_ref.shape[0], SUB_ROWS):
        rows = slice(r0, r0 + SUB_ROWS)
        x = r_ref[rows, :] + _dot(a_ref[rows, :], wb_ref[...])
        if n_prompt_blocks is None:
            out_refs[0][rows, :] = x
            for o_ref, xn in zip(out_refs[1:], _rms_normed(x, gain_refs)):
                o_ref[rows, :] = xn
        else:
            @pl.when(i < n_prompt_blocks)
            def _():
                out_refs[0][rows, :] = x

            @pl.when(i >= n_prompt_blocks)
            def _():
                out_refs[1][rows, :] = x


def _matmul_residual(a, w, layer, res, gains, name, split_rows=None):
    T, K = a.shape
    N = w.shape[-1]
    tm = ROW_BLOCK
    row = pl.BlockSpec((tm, N), lambda i: (i, 0))
    in_specs = [pl.BlockSpec((tm, K), lambda i: (i, 0)),
                pl.BlockSpec((None, K, N), lambda i: (layer, 0, 0)),
                row]
    in_specs += [pl.BlockSpec((None, 1, N), lambda i, idx=idx: (idx, 0, 0)) for _, idx in gains]
    if split_rows is None:
        npb = None
        out_specs = [row] * (1 + len(gains))
        out_shape = [jax.ShapeDtypeStruct((T, N), F32)] + [jax.ShapeDtypeStruct((T, N), BF16)] * len(gains)
        aliases = {2: 0}
    else:
        assert not gains
        npb = split_rows // tm
        out_specs = [pl.BlockSpec((tm, N), lambda i: (jnp.minimum(i, npb - 1), 0)),
                     pl.BlockSpec((tm, N), lambda i: (jnp.maximum(i - npb, 0), 0))]
        out_shape = [jax.ShapeDtypeStruct((split_rows, N), F32), jax.ShapeDtypeStruct((T - split_rows, N), F32)]
        aliases = {}
    return pl.pallas_call(
        functools.partial(_mm_res_kernel, n_gains=len(gains), n_prompt_blocks=npb),
        grid=(T // tm,),
        in_specs=in_specs,
        out_specs=out_specs,
        out_shape=out_shape,
        scratch_shapes=[pltpu.VMEM((K, N), BF16)],
        input_output_aliases=aliases,
        compiler_params=_params(1),
        name=name,
    )(a, w, res, *[g for g, _ in gains])


def _groupnorm_gate(o, gate):
    mu = jnp.mean(o, axis=-1, keepdims=True)
    d = o - mu
    var = jnp.mean(d * d, axis=-1, keepdims=True)
    on = d * lax.rsqrt(var + EPS)
    return (gate * jax.nn.sigmoid(gate)) * on


def _ret_prompt_kernel(q_ref, k_ref, v_ref, g_ref, dmask_ref, qdec_ref, kdec_ref, cdec_ref, go_ref, sfin_ref,
                       s_ref, *, chunk_rows, n_chunks):
    c = pl.program_id(2)

    @pl.when(c == 0)
    def _():
        s_ref[...] = jnp.zeros_like(s_ref)

    dmask = dmask_ref[...]
    qdec = qdec_ref[...]
    kdec = kdec_ref[...]
    cdec = cdec_ref[...]

    def chunk(ci, carry):
        rows = pl.ds(pl.multiple_of(ci * chunk_rows, chunk_rows), chunk_rows)
        qb = q_ref[rows, :]
        kb = k_ref[rows, :]
        vb = v_ref[rows, :]
        s = s_ref[...]
        scores = _dot_nt(qb, kb) * dmask
        inner = _dot(scores.astype(BF16), vb)
        cross = _dot(qb, s.astype(BF16)) * qdec
        s_ref[...] = s * cdec + _dot_tn((kb.astype(F32) * kdec).astype(BF16), vb)
        go_ref[rows, :] = _groupnorm_gate(inner + cross, g_ref[rows, :].astype(F32)).astype(go_ref.dtype)
        return carry

    lax.fori_loop(0, n_chunks, chunk, 0, unroll=min(n_chunks, RET_UNROLL))

    @pl.when(c == pl.num_programs(2) - 1)
    def _():
        sfin_ref[...] = s_ref[...]


def _ret_prompt(proj, tabs, B, L, T, name):
    rb = _row_tile(L, (1024, 512, 128))
    ck = tabs["chunk_p"]
    nblk = L // rb
    qcol, vcol = RET_HEADS, RET_HEADS
    gcol = 2 * RET_HEADS
    row = lambda b, h, c: b * nblk + c
    kern = functools.partial(_ret_prompt_kernel, chunk_rows=ck, n_chunks=rb // ck)
    return pl.pallas_call(
        kern,
        grid=(B, RET_HEADS, nblk),
        in_specs=[pl.BlockSpec((rb, RET_DK), lambda b, h, c: (row(b, h, c), h)),
                  pl.BlockSpec((rb, RET_DK), lambda b, h, c: (row(b, h, c), qcol + h)),
                  pl.BlockSpec((rb, RET_DV), lambda b, h, c: (row(b, h, c), vcol + h)),
                  pl.BlockSpec((rb, RET_DV), lambda b, h, c: (row(b, h, c), gcol + h)),
                  pl.BlockSpec((None, ck, ck), lambda b, h, c: (h, 0, 0)),
                  pl.BlockSpec((None, ck, 1), lambda b, h, c: (h, 0, 0)),
                  pl.BlockSpec((None, ck, 1), lambda b, h, c: (h, 0, 0)),
                  pl.BlockSpec((None, 1, 1), lambda b, h, c: (h, 0, 0))],
        out_specs=[pl.BlockSpec((rb, RET_DV), lambda b, h, c: (row(b, h, c), h)),
                   pl.BlockSpec((None, None, RET_DK, RET_DV), lambda b, h, c: (b, h, 0, 0))],
        out_shape=[jax.ShapeDtypeStruct((T, RET_HEADS * RET_DV), BF16),
                   jax.ShapeDtypeStruct((B, RET_HEADS, RET_DK, RET_DV), F32)],
        scratch_shapes=[pltpu.VMEM((RET_DK, RET_DV), F32)],
        compiler_params=_params(3),
        name=name,
    )(proj, proj, proj, proj, tabs["dmask_p"], tabs["qdec_p"], tabs["kdec_p"], tabs["cdec_p"])


def _ret_decode_kernel(q_ref, k_ref, v_ref, g_ref, dmask_ref, qdec_ref, kdec_ref, cdec_ref, s_ref, *rest,
                       t_dec):
    go_ref, snew_ref = rest[-2:]
    qb = q_ref[...]
    kb = k_ref[...]
    vb = v_ref[...]
    cdec = cdec_ref[...]
    scores = _dot_nt(qb, kb) * dmask_ref[...]
    inner = _dot(scores.astype(BF16), vb)
    kd = kb.astype(F32) * kdec_ref[...]
    seq_of_row = lax.broadcasted_iota(jnp.int32, (RET_DEC_ROWS, 1), 0) // t_dec
    cross = jnp.zeros((RET_DEC_ROWS, RET_DV), F32)
    for bi in range(RET_DEC_ROWS // t_dec):
        mine = seq_of_row == bi
        s = s_ref[bi]
        cross = cross + jnp.where(mine, _dot(qb, s.astype(BF16)), 0.0)
        snew_ref[bi] = s * cdec + _dot_tn(jnp.where(mine, kd, 0.0).astype(BF16), vb)
    cross = cross * qdec_ref[...]
    go_ref[...] = _groupnorm_gate(inner + cross, g_ref[...].astype(F32)).astype(go_ref.dtype)


def _ret_decode(proj, state, layer, go_joint, snew_joint, tabs, L, TP, DB, t_dec, name):
    nseq = RET_DEC_ROWS // t_dec
    r0 = TP // RET_DEC_ROWS
    qcol = vcol = RET_HEADS
    gcol = 2 * RET_HEADS
    kern = functools.partial(_ret_decode_kernel, t_dec=t_dec)
    s_spec = pl.BlockSpec((None, nseq, None, RET_DK, RET_DV), lambda i, h: (layer, i, h, 0, 0))
    any_spec = pl.BlockSpec(memory_space=pl.ANY)
    aliased = [go_joint] if snew_joint is None else [go_joint, snew_joint]
    return pl.pallas_call(
        kern,
        grid=(DB // nseq, RET_HEADS),
        in_specs=[pl.BlockSpec((RET_DEC_ROWS, RET_DK), lambda i, h: (r0 + i, h)),
                  pl.BlockSpec((RET_DEC_ROWS, RET_DK), lambda i, h: (r0 + i, qcol + h)),
                  pl.BlockSpec((RET_DEC_ROWS, RET_DV), lambda i, h: (r0 + i, vcol + h)),
                  pl.BlockSpec((RET_DEC_ROWS, RET_DV), lambda i, h: (r0 + i, gcol + h)),
                  pl.BlockSpec((None, RET_DEC_ROWS, RET_DEC_ROWS), lambda i, h: (h, 0, 0)),
                  pl.BlockSpec((None, RET_DEC_ROWS, 1), lambda i, h: (h, 0, 0)),
                  pl.BlockSpec((None, RET_DEC_ROWS, 1), lambda i, h: (h, 0, 0)),
                  pl.BlockSpec((None, 1, 1), lambda i, h: (h, 0, 0)),
                  s_spec] + [any_spec] * len(aliased),
        out_specs=[pl.BlockSpec((RET_DEC_ROWS, RET_DV), lambda i, h: (r0 + i, h)), s_spec],
        out_shape=[jax.ShapeDtypeStruct(go_joint.shape, go_joint.dtype),
                   jax.ShapeDtypeStruct(state.shape, state.dtype)],
        input_output_aliases={9 + n: n for n in range(len(aliased))},
        compiler_params=_params(2),
        name=name,
    )(proj, proj, proj, proj, tabs["dmask_d"], tabs["qdec_d"], tabs["kdec_d"], tabs["cdec_d"], state, *aliased)


def _headnorm_rope(x, w, cos2, sin2, bd):
    width = x.shape[1]
    sq = x * x
    hi = sq.astype(BF16)
    lo = (sq - hi.astype(F32)).astype(BF16)
    parts = []
    for c in range(width // 256):
        sl = slice(256 * c, 256 * (c + 1))
        parts.append(_dot(hi[:, sl], bd) + _dot(lo[:, sl], bd))
    ss = parts[0] if len(parts) == 1 else jnp.concatenate(parts, axis=1)
    y = (x * lax.rsqrt(ss * (1.0 / HEAD_DIM) + EPS)) * w
    lane = lax.broadcasted_iota(jnp.int32, y.shape, 1)
    first_half = (lane & (HEAD_DIM - 1)) < HEAD_DIM // 2
    rot = jnp.where(first_half, pltpu.roll(y, width - HEAD_DIM // 2, 1), pltpu.roll(y, HEAD_DIM // 2, 1))
    nrep = width // 128
    cosw = jnp.concatenate([cos2] * nrep, axis=1)
    sinw = jnp.concatenate([sin2] * nrep, axis=1)
    return y * cosw + rot * sinw


def _kv_post_kernel(k_ref, w_ref, cos_ref, sin_ref, bd_ref, o_ref, lo_ref, hi_ref):
    kn = _headnorm_rope(k_ref[...], w_ref[...], cos_ref[...], sin_ref[...], bd_ref[...])
    o_ref[...] = kn
    low = lax.broadcasted_iota(jnp.int32, (kn.shape[0], 128), 1) < HEAD_DIM
    zero = jnp.zeros((kn.shape[0], 128), F32)
    for c in range(N_KV_HEADS // 2):
        tile = kn[:, 128 * c:128 * (c + 1)]
        swapped = pltpu.roll(tile, HEAD_DIM, 1)
        even, odd = slice(256 * c, 256 * c + 128), slice(256 * c + 128, 256 * (c + 1))
        lo_ref[:, even] = jnp.where(low, tile, zero).astype(lo_ref.dtype)
        hi_ref[:, even] = jnp.where(low, zero, swapped).astype(hi_ref.dtype)
        lo_ref[:, odd] = jnp.where(low, swapped, zero).astype(lo_ref.dtype)
        hi_ref[:, odd] = jnp.where(low, zero, tile).astype(hi_ref.dtype)


def _table_block(i, n_prompt_blocks, blocks_per_seq):
    return jnp.where(i < n_prompt_blocks, i % blocks_per_seq, blocks_per_seq + i - n_prompt_blocks)


def _kv_post(kv, k_norm_w, tabs, L, TP, name):
    T = kv.shape[0]
    kw = N_KV_HEADS * HEAD_DIM
    npb, bps = TP // ROW_BLOCK, L // ROW_BLOCK
    tab = lambda i: (_table_block(i, npb, bps), 0)
    pad_spec = pl.BlockSpec((ROW_BLOCK, 128 * N_KV_HEADS), lambda i: (i, 0))
    pad_shape = jax.ShapeDtypeStruct((T, 128 * N_KV_HEADS), BF16)
    return pl.pallas_call(
        _kv_post_kernel,
        grid=(T // ROW_BLOCK,),
        in_specs=[pl.BlockSpec((ROW_BLOCK, kw), lambda i: (i, 0)),
                  pl.BlockSpec((1, kw), lambda i: (0, 0)),
                  pl.BlockSpec((ROW_BLOCK, 128), tab),
                  pl.BlockSpec((ROW_BLOCK, 128), tab),
                  pl.BlockSpec((256, 256), lambda i: (0, 0))],
        out_specs=[pl.BlockSpec((ROW_BLOCK, kw), lambda i: (i, 0)), pad_spec, pad_spec],
        out_shape=[jax.ShapeDtypeStruct((T, kw), F32), pad_shape, pad_shape],
        compiler_params=_params(1),
        name=name,
    )(kv, k_norm_w, tabs["att_cos"], tabs["att_sin"], tabs["bd"])


def _q_proj_kernel(x_ref, w_ref, qw_ref, cos_ref, sin_ref, bd_ref, o_ref, wb_ref):
    @pl.when(pl.program_id(0) == 0)
    def _():
        wb_ref[...] = w_ref[...].astype(BF16)

    bd = bd_ref[...]
    for r0 in range(0, x_ref.shape[0], 128):
        rows = slice(r0, r0 + 128)
        q = _dot(x_ref[rows, :], wb_ref[...])
        cos2, sin2 = cos_ref[rows, :], sin_ref[rows, :]
        for c0 in range(0, q.shape[1], 256):
            cols = slice(c0, c0 + 256)
            qc = _headnorm_rope(q[:, cols], qw_ref[:, cols], cos2, sin2, bd)
            o_ref[rows, cols] = (qc * (HEAD_DIM ** -0.5)).astype(o_ref.dtype)


def _q_proj(xn, w, q_norm_w, layer, tabs, L, TP, name):
    T, K = xn.shape
    N = w.shape[-1]
    npb, bps = TP // ROW_BLOCK, L // ROW_BLOCK
    tab = lambda i: (_table_block(i, npb, bps), 0)
    return pl.pallas_call(
        _q_proj_kernel,
        grid=(T // ROW_BLOCK,),
        in_specs=[pl.BlockSpec((ROW_BLOCK, K), lambda i: (i, 0)),
                  pl.BlockSpec((None, K, N), lambda i: (layer, 0, 0)),
                  pl.BlockSpec((None, 1, N), lambda i: (layer, 0, 0)),
                  pl.BlockSpec((ROW_BLOCK, 128), tab),
                  pl.BlockSpec((ROW_BLOCK, 128), tab),
                  pl.BlockSpec((256, 256), lambda i: (0, 0))],
        out_specs=pl.BlockSpec((ROW_BLOCK, N), lambda i: (i, 0)),
        out_shape=jax.ShapeDtypeStruct((T, N), BF16),
        scratch_shapes=[pltpu.VMEM((K, N), BF16)],
        compiler_params=_params(1),
        name=name,
    )(xn, w, q_norm_w, tabs["att_cos"], tabs["att_sin"], tabs["bd"])


def _sink_column(sink_ref, layer, g, rows, rows_per_head):
    head_in_group = lax.broadcasted_iota(jnp.int32, (rows, 1), 0) // rows_per_head
    col = jnp.full((rows, 1), sink_ref[layer, g * GROUP], F32)
    for r in range(1, GROUP):
        col = jnp.where(head_in_group == r, sink_ref[layer, g * GROUP + r], col)
    return col


def _attn_prompt_kernel(sink_ref, q_ref, lop_ref, loc_ref, hip_ref, hic_ref, vp_ref, vc_ref, o_ref, *,
                        layer, n_qblk):
    first = pl.program_id(1) == 0
    qb = q_ref[...]
    klo = jnp.concatenate([lop_ref[...], loc_ref[...]], axis=0)
    khi = jnp.concatenate([hip_ref[...], hic_ref[...]], axis=0)
    v_t = jnp.concatenate([vp_ref[...], vc_ref[...]], axis=0).T.astype(BF16)
    key = lax.broadcasted_iota(jnp.int32, (WINDOW, WINDOW), 0)
    qi = lax.broadcasted_iota(jnp.int32, (WINDOW, WINDOW), 1)
    own = key <= qi
    scores = {}
    for t in range(n_qblk):
        qrows = slice(WINDOW * t, WINDOW * (t + 1))
        krows = slice(WINDOW * t, WINDOW * (t + 2))
        for g in range(N_KV_HEADS):
            gl = slice(128 * g, 128 * (g + 1))
            kk = jnp.concatenate([klo[krows, gl], khi[krows, gl]], axis=0)
            xq = jnp.concatenate([qb[qrows, 256 * g:256 * g + 128], qb[qrows, 256 * g + 128:256 * (g + 1)]],
                                 axis=0)
            scores[t, g] = _dot_nt(kk, xq)
    for t in range(n_qblk):
        qrows = slice(WINDOW * t, WINDOW * (t + 1))
        krows = slice(WINDOW * t, WINDOW * (t + 2))
        pieces = []
        for g in range(N_KV_HEADS):
            s4 = scores[t, g]
            v_g = v_t[HEAD_DIM * g:HEAD_DIM * (g + 1), krows]
            for pair in range(GROUP // 2):
                p2s, invs = [], []
                for parity in range(2):
                    sk = sink_ref[layer, g * GROUP + 2 * pair + parity]
                    blk = s4[2 * WINDOW * parity:2 * WINDOW * (parity + 1), WINDOW * pair:WINDOW * (pair + 1)]
                    s_prev = blk[:WINDOW]
                    if t == 0:
                        s_prev = jnp.where(first, NEG_INF, s_prev)
                    s = jnp.where(own, blk[WINDOW:], s_prev)
                    m = jnp.maximum(jnp.max(s, axis=0, keepdims=True), sk)
                    p = jnp.exp(s - m)
                    invs.append(1.0 / (jnp.sum(p, axis=0, keepdims=True) + jnp.exp(sk - m)))
                    p2s.append(jnp.concatenate([jnp.where(own, 0.0, p), jnp.where(own, p, 0.0)], axis=0))
                p2 = jnp.concatenate(p2s, axis=1).astype(BF16)
                o_t = _dot(v_g, p2) * jnp.concatenate(invs, axis=1)
                pieces += [o_t[:, :WINDOW], o_t[:, WINDOW:]]
        o_ref[qrows, :] = jnp.concatenate(pieces, axis=0).T.astype(o_ref.dtype)


def _attn_prompt(q, klo, khi, kv, sinks, layer, B, L, T, name):
    nb = L // WINDOW
    nq = ATTN_Q_BLOCKS if nb % ATTN_Q_BLOCKS == 0 else 1
    ns = nb // nq
    kw = N_KV_HEADS * HEAD_DIM
    kpad = 128 * N_KV_HEADS
    cur = lambda b, i: b * ns + i
    prev = lambda b, i: b * nb + jnp.maximum(i * nq - 1, 0)
    kern = functools.partial(_attn_prompt_kernel, layer=layer, n_qblk=nq)
    return pl.pallas_call(
        kern,
        grid=(B, ns),
        in_specs=[pl.BlockSpec(memory_space=pltpu.SMEM),
                  pl.BlockSpec((nq * WINDOW, D_MODEL), lambda b, i: (cur(b, i), 0)),
                  pl.BlockSpec((WINDOW, kpad), lambda b, i: (prev(b, i), 0)),
                  pl.BlockSpec((nq * WINDOW, kpad), lambda b, i: (cur(b, i), 0)),
                  pl.BlockSpec((WINDOW, kpad), lambda b, i: (prev(b, i), 0)),
                  pl.BlockSpec((nq * WINDOW, kpad), lambda b, i: (cur(b, i), 0)),
                  pl.BlockSpec((WINDOW, kw), lambda b, i: (prev(b, i), 1)),
                  pl.BlockSpec((nq * WINDOW, kw), lambda b, i: (cur(b, i), 1))],
        out_specs=pl.BlockSpec((nq * WINDOW, D_MODEL), lambda b, i: (cur(b, i), 0)),
        out_shape=jax.ShapeDtypeStruct((T, D_MODEL), BF16),
        compiler_params=_params(2),
        name=name,
    )(sinks, q, klo, klo, khi, khi, kv, kv)


def _attn_decode_kernel(sink_ref, q_ref, kn_ref, vn_ref, ck_ref, cv_ref, ao_in_ref, o_ref, *, layer, t_dec):
    del ao_in_ref
    nseq = DEC_ROWS // t_dec
    wc = ck_ref.shape[1]
    qb = q_ref[...]
    knew = kn_ref[...].astype(BF16)
    vnew = vn_ref[...].astype(BF16)
    rows = GROUP * DEC_ROWS
    row = lax.broadcasted_iota(jnp.int32, (rows, 1), 0)
    row_seq = (row % DEC_ROWS) // t_dec
    row_tok = row % t_dec
    jold = lax.broadcasted_iota(jnp.int32, (rows, wc), 1)
    vis_old = (jold > row_tok + (wc - WINDOW)) & (jold <= row_tok + wc)
    cnew = lax.broadcasted_iota(jnp.int32, (rows, DEC_ROWS), 1)
    vis_new = (cnew // t_dec == row_seq) & (cnew % t_dec <= row_tok)
    for g in range(N_KV_HEADS):
        heads = [g * GROUP + r for r in range(GROUP)]
        hs = slice(HEAD_DIM * g, HEAD_DIM * (g + 1))
        qs = jnp.concatenate([qb[:, HEAD_DIM * h:HEAD_DIM * (h + 1)] for h in heads], axis=0)
        s_old = jnp.zeros((rows, wc), F32)
        for bi in range(nseq):
            kc = ck_ref[bi][:, hs].astype(BF16)
            s_old = s_old + jnp.where(row_seq == bi, _dot_nt(qs, kc), 0.0)
        s_old = jnp.where(vis_old, s_old, NEG_INF)
        s_new = jnp.where(vis_new, _dot_nt(qs, knew[:, hs]), NEG_INF)
        sk = _sink_column(sink_ref, layer, g, rows, DEC_ROWS)
        m = jnp.maximum(jnp.maximum(jnp.max(s_old, axis=-1, keepdims=True),
                                    jnp.max(s_new, axis=-1, keepdims=True)), sk)
        p_old = jnp.exp(s_old - m)
        p_new = jnp.exp(s_new - m)
        denom = (jnp.sum(p_old, axis=-1, keepdims=True) + jnp.sum(p_new, axis=-1, keepdims=True)
                 + jnp.exp(sk - m))
        inv = 1.0 / denom
        o = _dot((p_new * inv).astype(BF16), vnew[:, hs])
        pn_old = p_old * inv
        for bi in range(nseq):
            vc = cv_ref[bi][:, hs].astype(BF16)
            o = o + _dot(jnp.where(row_seq == bi, pn_old, 0.0).astype(BF16), vc)
        for r, h in enumerate(heads):
            o_ref[:, HEAD_DIM * h:HEAD_DIM * (h + 1)] = o[DEC_ROWS * r:DEC_ROWS * (r + 1)].astype(o_ref.dtype)


def _attn_decode(q, kn, kv, cache_k, cache_v, sinks, layer, ao_joint, TP, DB, t_dec, name):
    nseq = DEC_ROWS // t_dec
    r0 = TP // DEC_ROWS
    kw = N_KV_HEADS * HEAD_DIM
    wc = cache_k.shape[1]
    kern = functools.partial(_attn_decode_kernel, layer=layer, t_dec=t_dec)
    return pl.pallas_call(
        kern,
        grid=(DB // nseq,),
        in_specs=[pl.BlockSpec(memory_space=pltpu.SMEM),
                  pl.BlockSpec((DEC_ROWS, D_MODEL), lambda i: (r0 + i, 0)),
                  pl.BlockSpec((DEC_ROWS, kw), lambda i: (r0 + i, 0)),
                  pl.BlockSpec((DEC_ROWS, kw), lambda i: (r0 + i, 1)),
                  pl.BlockSpec((nseq, wc, kw), lambda i: (i, 0, 0)),
                  pl.BlockSpec((nseq, wc, kw), lambda i: (i, 0, 0)),
                  pl.BlockSpec(memory_space=pl.ANY)],
        out_specs=pl.BlockSpec((DEC_ROWS, D_MODEL), lambda i: (r0 + i, 0)),
        out_shape=jax.ShapeDtypeStruct(ao_joint.shape, ao_joint.dtype),
        input_output_aliases={6: 0},
        compiler_params=_params(1),
        name=name,
    )(sinks, q, kn, kv, cache_k, cache_v, ao_joint)


def _rope_cos_sin(pos, half):
    inv = 1.0 / (ROPE_THETA ** (jnp.arange(half, dtype=F32) / half))
    ang = pos.astype(F32)[:, None] * inv[None, :]
    return jnp.cos(ang), jnp.sin(ang)


def _decay_tables(chunk, reps):
    lg = jnp.log(1.0 - 2.0 ** (-5.0 - jnp.arange(RET_HEADS, dtype=F32)))
    idx = jnp.arange(chunk, dtype=F32)
    rel = idx[:, None] - idx[None, :]
    dmask = jnp.where(rel >= 0, jnp.exp(lg[:, None, None] * jnp.maximum(rel, 0.0)), 0.0)
    qdec = jnp.exp(lg[:, None] * (idx + 1.0))[:, :, None]
    kdec = jnp.exp(lg[:, None] * (chunk - 1.0 - idx))[:, :, None]
    cdec = jnp.exp(lg * chunk)[:, None, None]
    if reps > 1:
        eye = jnp.eye(reps, dtype=F32)
        dmask = jnp.einsum("ab,hij->haibj", eye, dmask).reshape(RET_HEADS, reps * chunk, reps * chunk)
        qdec = jnp.tile(qdec, (1, reps, 1))
        kdec = jnp.tile(kdec, (1, reps, 1))
    return dmask, qdec, kdec, cdec


def _tables(B, L, TD, t_dec, chunk_p):
    pos = jnp.concatenate([jnp.arange(L), PAST_LEN + jnp.arange(TD) % t_dec])
    ret_cos, ret_sin = _rope_cos_sin(pos, RET_DK // 2)
    per_row = lambda t: jnp.concatenate([jnp.tile(t[:L], (B, 1)), t[L:]], axis=0)
    c, s = _rope_cos_sin(pos, HEAD_DIM // 2)
    att_cos = jnp.concatenate([c, c, c, c], axis=1)
    att_sin = jnp.concatenate([-s, s, -s, s], axis=1)
    dmask_p, qdec_p, kdec_p, cdec_p = _decay_tables(chunk_p, 1)
    dmask_d, qdec_d, kdec_d, cdec_d = _decay_tables(math.gcd(t_dec, RET_CHUNK), RET_DEC_ROWS // t_dec)
    head_of_lane = np.arange(256) // HEAD_DIM
    bd = jnp.asarray((head_of_lane[:, None] == head_of_lane[None, :]).astype(np.float32), dtype=BF16)
    return dict(ret_cos_rows=per_row(ret_cos), ret_sin_rows=per_row(ret_sin), att_cos=att_cos, att_sin=att_sin,
                chunk_p=chunk_p, dmask_p=dmask_p, qdec_p=qdec_p, kdec_p=kdec_p, cdec_p=cdec_p,
                dmask_d=dmask_d, qdec_d=qdec_d, kdec_d=kdec_d, cdec_d=cdec_d, bd=bd)


def kernel(x_prompt, x_sample, state_ret, cache_k, cache_v, ln_ret, w_ret_in, w_ret_out, ln_ffn, w_ffn_in,
           w_ffn_out, ln_kv, w_kv, k_norm, ln_attn, w_q, q_norm, sinks, w_o):
    B, L, D = x_prompt.shape
    DB, t_dec, _ = x_sample.shape
    n_ret = w_ret_in.shape[0]
    n_attn = w_q.shape[0]
    TP, TD = B * L, DB * t_dec
    T = TP + TD
    wc = cache_k.shape[1]
    kw = N_KV_HEADS * HEAD_DIM
    assert D == D_MODEL and L % RET_CHUNK == 0 and L % ROW_BLOCK == 0 and TD % ROW_BLOCK == 0
    assert t_dec == math.gcd(t_dec, RET_CHUNK) and DEC_ROWS % t_dec == 0 and wc == WINDOW and PAST_LEN >= WINDOW

    tabs = _tables(B, L, TD, t_dec, RET_CHUNK_PROMPT if L % RET_CHUNK_PROMPT == 0 else RET_CHUNK)
    ck = cache_k.reshape(DB, wc, kw)
    cv = cache_v.reshape(DB, wc, kw)
    q_norm_w = jnp.tile(q_norm, (1, N_HEADS))[:, None, :]
    k_norm_w = jnp.tile(k_norm[None, :], (1, N_KV_HEADS))
    ln_ret, ln_ffn, ln_attn = ln_ret[:, None, :], ln_ffn[:, None, :], ln_attn[:, None, :]

    ln_kv3 = ln_kv[None, None, :]
    prompt_states = []
    dec_states = None
    x, xn = _embed(x_prompt.reshape(TP, D), x_sample.reshape(TD, D), (ln_ret, 0), "embed")
    for l in range(n_ret):
        proj = _ret_in(xn, w_ret_in, l, tabs, f"ret_in_{l}")
        go, s_p = _ret_prompt(proj, tabs, B, L, T, f"ret_prompt_{l}")
        go, dec_states = _ret_decode(proj, state_ret, l, go, dec_states, tabs, L, TP, DB, t_dec, f"ret_decode_{l}")
        prompt_states.append(s_p)
        x, xn = _matmul_residual(go, w_ret_out, l, x, [(ln_ffn, l)], f"ret_out_{l}")
        h = _swiglu(xn, w_ffn_in, l, f"ffn_in_{l}")
        if l + 1 < n_ret:
            x, xn = _matmul_residual(h, w_ffn_out, l, x, [(ln_ret, l + 1)], f"ffn_out_{l}")
        else:
            x, xn_kv, xn = _matmul_residual(h, w_ffn_out, l, x, [(ln_kv3, 0), (ln_attn, 0)], f"ffn_out_{l}")

    kv = _matmul(xn_kv, w_kv[None], 0, 2 * kw, F32, "kv_proj")
    kn, klo, khi = _kv_post(kv, k_norm_w, tabs, L, TP, "kv_post")
    for j in range(n_attn):
        layer = n_ret + j
        q = _q_proj(xn, w_q, q_norm_w, j, tabs, L, TP, f"q_proj_{j}")
        ao = _attn_prompt(q, klo, khi, kv, sinks, j, B, L, T, f"attn_prompt_{j}")
        ao = _attn_decode(q, kn, kv, ck, cv, sinks, j, ao, TP, DB, t_dec, f"attn_decode_{j}")
        x, xn = _matmul_residual(ao, w_o, j, x, [(ln_ffn, layer)], f"attn_out_{j}")
        h = _swiglu(xn, w_ffn_in, layer, f"ffn_in_{layer}")
        if j + 1 < n_attn:
            x, xn = _matmul_residual(h, w_ffn_out, layer, x, [(ln_attn, j + 1)], f"ffn_out_{layer}")
        else:
            y_p, y_s = _matmul_residual(h, w_ffn_out, layer, x, [], f"ffn_out_{layer}", split_rows=TP)

    y_prompt = y_p.reshape(B, L, D)
    y_sample = y_s.reshape(DB, t_dec, D)
    state_prompt = jnp.stack(prompt_states)
    w_keep = min(WINDOW, L)
    kn_p = kn[:TP].reshape(B, L, N_KV_HEADS, HEAD_DIM)
    v_p = kv[:TP, kw:].reshape(B, L, N_KV_HEADS, HEAD_DIM)
    kn_d = kn[TP:].reshape(DB, t_dec, N_KV_HEADS, HEAD_DIM)
    v_d = kv[TP:, kw:].reshape(DB, t_dec, N_KV_HEADS, HEAD_DIM)
    cache_k_sample = jnp.concatenate([cache_k, kn_d], axis=1)[:, -wc:]
    cache_v_sample = jnp.concatenate([cache_v, v_d], axis=1)[:, -wc:]
    return (y_prompt, y_sample, state_prompt, dec_states, kn_p[:, -w_keep:], v_p[:, -w_keep:],
            cache_k_sample, cache_v_sample)
```

```python
import functools
import math

import numpy as np
import jax
import jax.numpy as jnp
from jax import lax
from jax.experimental import pallas as pl
from jax.experimental.pallas import tpu as pltpu

D_MODEL = 1024
PAST_LEN = 8192
RET_HEADS = 4
RET_DK = 256
RET_DV = 512
RET_CHUNK = 128
RET_CHUNK_PROMPT = 256
N_HEADS = 16
N_KV_HEADS = 4
HEAD_DIM = 64
GROUP = N_HEADS // N_KV_HEADS
WINDOW = 128
D_FF = 2816
ROPE_THETA = 10000.0
EPS = 1e-6
NEG_INF = -1e30

F32 = jnp.float32
BF16 = jnp.bfloat16

V7X_VMEM_LIMIT_BYTES = 56 * 1024 * 1024
DEC_ROWS = 16
RET_DEC_ROWS = 32
ROW_BLOCK = 512
SUB_ROWS = 256
RET_UNROLL = 4
ATTN_Q_BLOCKS = 2


def _dot(a, b):
    return jnp.dot(a, b, preferred_element_type=F32)


def _dot_nt(a, b):
    return lax.dot_general(a, b, (((1,), (1,)), ((), ())), preferred_element_type=F32)


def _dot_tn(a, b):
    return lax.dot_general(a, b, (((0,), (0,)), ((), ())), preferred_element_type=F32)


def _params(n_axes):
    return pltpu.CompilerParams(dimension_semantics=("arbitrary",) * n_axes,
                                vmem_limit_bytes=V7X_VMEM_LIMIT_BYTES)


def _row_tile(rows, prefs):
    for t in prefs:
        if rows % t == 0:
            return t
    raise ValueError(f"no row tile for {rows}")


def _rms_normed(x, gain_refs):
    ms = jnp.mean(x * x, axis=-1, keepdims=True)
    xh = x * lax.rsqrt(ms + EPS)
    return [(xh * g_ref[...]).astype(BF16) for g_ref in gain_refs]


def _embed_kernel(xp_ref, xs_ref, g_ref, x_ref, xn_ref, *, n_prompt_blocks):
    i = pl.program_id(0)

    @pl.when(i < n_prompt_blocks)
    def _():
        x_ref[...] = xp_ref[...]

    @pl.when(i >= n_prompt_blocks)
    def _():
        x_ref[...] = xs_ref[...]

    xn_ref[...] = _rms_normed(x_ref[...], [g_ref])[0]


def _embed(xp, xs, gain, name):
    TP, D = xp.shape
    TD = xs.shape[0]
    npb = TP // ROW_BLOCK
    T = TP + TD
    g_arr, g_idx = gain
    blk = lambda fn: pl.BlockSpec((ROW_BLOCK, D), fn)
    return pl.pallas_call(
        functools.partial(_embed_kernel, n_prompt_blocks=npb),
        grid=(T // ROW_BLOCK,),
        in_specs=[blk(lambda i: (jnp.minimum(i, npb - 1), 0)),
                  blk(lambda i: (jnp.maximum(i - npb, 0), 0)),
                  pl.BlockSpec((None, 1, D), lambda i: (g_idx, 0, 0))],
        out_specs=[blk(lambda i: (i, 0)), blk(lambda i: (i, 0))],
        out_shape=[jax.ShapeDtypeStruct((T, D), F32), jax.ShapeDtypeStruct((T, D), BF16)],
        compiler_params=_params(1),
        name=name,
    )(xp, xs, g_arr)


def _sub_rows(tm):
    return max(r for r in range(16, 3 * SUB_ROWS + 1, 16) if tm % r == 0)


def _ret_in_kernel(x_ref, w_ref, cos_ref, sin_ref, o_ref, *, n_q_tiles):
    j = pl.program_id(1)
    sub = _sub_rows(x_ref.shape[0])
    half = RET_DK // 2

    @pl.when(j >= 2 * n_q_tiles)
    def _():
        o_ref[...] = _dot(x_ref[...], w_ref[...].astype(BF16)).astype(o_ref.dtype)

    @pl.when(j < 2 * n_q_tiles)
    def _():
        scale = jnp.where(j < n_q_tiles, 1.0, RET_DK ** -0.5)
        w = w_ref[...].astype(BF16)
        for r0 in range(0, x_ref.shape[0], sub):
            rows = slice(r0, r0 + sub)
            acc = _dot(x_ref[rows, :], w)
            cos = cos_ref[rows, :]
            sin = sin_ref[rows, :]
            for c0 in range(0, acc.shape[1], RET_DK):
                x1, x2 = acc[:, c0:c0 + half], acc[:, c0 + half:c0 + RET_DK]
                o_ref[rows, c0:c0 + half] = ((x1 * cos - x2 * sin) * scale).astype(o_ref.dtype)
                o_ref[rows, c0 + half:c0 + RET_DK] = ((x2 * cos + x1 * sin) * scale).astype(o_ref.dtype)


def _ret_in(xn, w, layer, tabs, name):
    T, K = xn.shape
    N = w.shape[-1]
    tn = 1024
    tm = _row_tile(T, (1536, 1024, 512))
    return pl.pallas_call(
        functools.partial(_ret_in_kernel, n_q_tiles=RET_HEADS * RET_DK // tn),
        grid=(T // tm, N // tn),
        in_specs=[pl.BlockSpec((tm, K), lambda i, j: (i, 0)),
                  pl.BlockSpec((None, K, tn), lambda i, j: (layer, 0, j)),
                  pl.BlockSpec((tm, RET_DK // 2), lambda i, j: (i, 0)),
                  pl.BlockSpec((tm, RET_DK // 2), lambda i, j: (i, 0))],
        out_specs=pl.BlockSpec((tm, tn), lambda i, j: (i, j)),
        out_shape=jax.ShapeDtypeStruct((T, N), BF16),
        compiler_params=_params(2),
        name=name,
    )(xn, w, tabs["ret_cos_rows"], tabs["ret_sin_rows"])


def _swiglu_kernel(x_ref, wa_ref, wb_ref, o_ref):
    wa = wa_ref[...].astype(BF16)
    wb = wb_ref[...].astype(BF16)
    sub = _sub_rows(x_ref.shape[0])
    for r0 in range(0, x_ref.shape[0], sub):
        rows = slice(r0, r0 + sub)
        a = _dot(x_ref[rows, :], wa)
        b = _dot(x_ref[rows, :], wb)
        o_ref[rows, :] = ((a * jax.nn.sigmoid(a)) * b).astype(o_ref.dtype)


def _swiglu(xn, w, layer, name):
    T, K = xn.shape
    tn = 256
    nb = D_FF // tn
    tm = _row_tile(T, (2816, 1536, 1024, 512))
    return pl.pallas_call(
        _swiglu_kernel,
        grid=(T // tm, nb),
        in_specs=[pl.BlockSpec((tm, K), lambda i, j: (i, 0)),
                  pl.BlockSpec((None, K, tn), lambda i, j: (layer, 0, j)),
                  pl.BlockSpec((None, K, tn), lambda i, j: (layer, 0, nb + j))],
        out_specs=pl.BlockSpec((tm, tn), lambda i, j: (i, j)),
        out_shape=jax.ShapeDtypeStruct((T, D_FF), BF16),
        compiler_params=_params(2),
        name=name,
    )(xn, w, w)


def _mm_res_kernel(*refs, n_a, n_gains, n_prompt_blocks, split_out):
    a_refs = refs[:n_a]
    w_ref, r_ref = refs[n_a:n_a + 2]
    rest = refs[n_a + 2:]
    gain_refs = rest[:n_gains]
    out_refs = rest[n_gains:-1]
    wb_ref = rest[-1]
    i = pl.program_id(0)

    @pl.when(i == 0)
    def _():
        wb_ref[...] = w_ref[...].astype(BF16)

    def step(a_ref, dst):
        for r0 in range(0, a_ref.shape[0], SUB_ROWS):
            rows = slice(r0, r0 + SUB_ROWS)
            x = r_ref[rows, :] + _dot(a_ref[rows, :], wb_ref[...])
            out_refs[dst][rows, :] = x
            if not split_out:
                for o_ref, xn in zip(out_refs[1:], _rms_normed(x, gain_refs)):
                    o_ref[rows, :] = xn

    if n_a == 1 and not split_out:
        step(a_refs[0], 0)
    else:
        @pl.when(i < n_prompt_blocks)
        def _():
            step(a_refs[0], 0)

        @pl.when(i >= n_prompt_blocks)
        def _():
            step(a_refs[-1], 1 if split_out else 0)


def _matmul_residual(a, w, layer, res, gains, name, TP, split_out=False):
    a = a if isinstance(a, (tuple, list)) else (a,)
    T, N = res.shape
    K = a[0].shape[1]
    tm = ROW_BLOCK
    npb = TP // tm
    row = pl.BlockSpec((tm, N), lambda i: (i, 0))
    head = pl.BlockSpec((tm, N), lambda i: (jnp.minimum(i, npb - 1), 0))
    tail = pl.BlockSpec((tm, N), lambda i: (jnp.maximum(i - npb, 0), 0))
    if len(a) == 1:
        in_specs = [pl.BlockSpec((tm, K), lambda i: (i, 0))]
    else:
        in_specs = [pl.BlockSpec((tm, K), lambda i: (jnp.minimum(i, npb - 1), 0)),
                    pl.BlockSpec((tm, K), lambda i: (jnp.maximum(i - npb, 0), 0))]
    in_specs += [pl.BlockSpec((None, K, N), lambda i: (layer, 0, 0)), row]
    in_specs += [pl.BlockSpec((None, 1, N), lambda i, idx=idx: (idx, 0, 0)) for _, idx in gains]
    if split_out:
        assert not gains
        out_specs = [head, tail]
        out_shape = [jax.ShapeDtypeStruct((TP, N), F32), jax.ShapeDtypeStruct((T - TP, N), F32)]
        aliases = {}
    else:
        out_specs = [row] * (1 + len(gains))
        out_shape = [jax.ShapeDtypeStruct((T, N), F32)] + [jax.ShapeDtypeStruct((T, N), BF16)] * len(gains)
        aliases = {len(a) + 1: 0}
    return pl.pallas_call(
        functools.partial(_mm_res_kernel, n_a=len(a), n_gains=len(gains), n_prompt_blocks=npb,
                          split_out=split_out),
        grid=(T // tm,),
        in_specs=in_specs,
        out_specs=out_specs,
        out_shape=out_shape,
        scratch_shapes=[pltpu.VMEM((K, N), BF16)],
        input_output_aliases=aliases,
        compiler_params=_params(1),
        name=name,
    )(*a, w, res, *[g for g, _ in gains])


def _groupnorm_gate(o, gate):
    mu = jnp.mean(o, axis=-1, keepdims=True)
    d = o - mu
    var = jnp.mean(d * d, axis=-1, keepdims=True)
    on = d * lax.rsqrt(var + EPS)
    return (gate * jax.nn.sigmoid(gate)) * on


def _retention_kernel(q_ref, k_ref, v_ref, g_ref, dmask_ref, qdec_ref, kdec_ref, cdec_ref,
                      qd_ref, kd_ref, vd_ref, gd_ref, dmaskd_ref, qdecd_ref, kdecd_ref, cdecd_ref, sd_ref,
                      *rest, chunk_rows, n_chunks, t_dec):
    go_ref, sfin_ref, god_ref, snew_ref, s_ref = rest[-5:]
    c = pl.program_id(2)

    qb = qd_ref[...]
    kb = kd_ref[...]
    vb = vd_ref[...]
    cdecd = cdecd_ref[...]
    inner = _dot((_dot_nt(qb, kb) * dmaskd_ref[...]).astype(BF16), vb)
    kdd = kb.astype(F32) * kdecd_ref[...]
    seq_of_row = lax.broadcasted_iota(jnp.int32, (RET_DEC_ROWS, 1), 0) // t_dec
    cross = jnp.zeros((RET_DEC_ROWS, RET_DV), F32)
    for bi in range(RET_DEC_ROWS // t_dec):
        mine = seq_of_row == bi
        s = sd_ref[bi]
        cross = cross + jnp.where(mine, _dot(qb, s.astype(BF16)), 0.0)
        snew_ref[bi] = s * cdecd + _dot_tn(jnp.where(mine, kdd, 0.0).astype(BF16), vb)
    cross = cross * qdecd_ref[...]
    god_ref[...] = _groupnorm_gate(inner + cross, gd_ref[...].astype(F32)).astype(god_ref.dtype)

    @pl.when(c == 0)
    def _():
        s_ref[...] = jnp.zeros_like(s_ref)

    dmask = dmask_ref[...]
    qdec = qdec_ref[...]
    kdec = kdec_ref[...]
    cdec = cdec_ref[...]

    def chunk(ci, carry):
        rows = pl.ds(pl.multiple_of(ci * chunk_rows, chunk_rows), chunk_rows)
        qb = q_ref[rows, :]
        kb = k_ref[rows, :]
        vb = v_ref[rows, :]
        s = s_ref[...]
        scores = _dot_nt(qb, kb) * dmask
        inner = _dot(scores.astype(BF16), vb)
        cross = _dot(qb, s.astype(BF16)) * qdec
        s_ref[...] = s * cdec + _dot_tn((kb.astype(F32) * kdec).astype(BF16), vb)
        go_ref[rows, :] = _groupnorm_gate(inner + cross, g_ref[rows, :].astype(F32)).astype(go_ref.dtype)
        return carry

    lax.fori_loop(0, n_chunks, chunk, 0, unroll=min(n_chunks, RET_UNROLL))

    @pl.when(c == pl.num_programs(2) - 1)
    def _():
        sfin_ref[...] = s_ref[...]


def _retention(proj, state, layer, snew_all, tabs, B, L, DB, t_dec, name):
    TP, TD = B * L, DB * t_dec
    nseq = RET_DEC_ROWS // t_dec
    steps = DB // nseq
    assert DB % nseq == 0 and steps % B == 0, (DB, nseq, B)
    nblk = steps // B
    rb = L // nblk
    ck = tabs["chunk_p"]
    assert L % nblk == 0 and rb % ck == 0, (L, nblk, ck)
    r0 = TP // RET_DEC_ROWS
    kcol = vcol = RET_HEADS
    gcol = 2 * RET_HEADS
    prow = lambda b, h, c: b * nblk + c
    drow = lambda b, h, c: b * nblk + c
    dec = lambda width, col0: pl.BlockSpec((RET_DEC_ROWS, width), lambda b, h, c: (r0 + drow(b, h, c), col0 + h))
    per_head = lambda *shape: pl.BlockSpec((None,) + shape, lambda b, h, c: (h,) + (0,) * len(shape))
    s_spec = pl.BlockSpec((None, nseq, None, RET_DK, RET_DV), lambda b, h, c: (layer, drow(b, h, c), h, 0, 0))
    aliased = [] if snew_all is None else [snew_all]
    kern = functools.partial(_retention_kernel, chunk_rows=ck, n_chunks=rb // ck, t_dec=t_dec)
    return pl.pallas_call(
        kern,
        grid=(B, RET_HEADS, nblk),
        in_specs=[pl.BlockSpec((rb, RET_DK), lambda b, h, c: (prow(b, h, c), h)),
                  pl.BlockSpec((rb, RET_DK), lambda b, h, c: (prow(b, h, c), kcol + h)),
                  pl.BlockSpec((rb, RET_DV), lambda b, h, c: (prow(b, h, c), vcol + h)),
                  pl.BlockSpec((rb, RET_DV), lambda b, h, c: (prow(b, h, c), gcol + h)),
                  per_head(ck, ck), per_head(ck, 1), per_head(ck, 1), per_head(1, 1),
                  dec(RET_DK, 0), dec(RET_DK, kcol), dec(RET_DV, vcol), dec(RET_DV, gcol),
                  per_head(RET_DEC_ROWS, RET_DEC_ROWS), per_head(RET_DEC_ROWS, 1), per_head(RET_DEC_ROWS, 1),
                  per_head(1, 1), s_spec] + [pl.BlockSpec(memory_space=pl.ANY)] * len(aliased),
        out_specs=[pl.BlockSpec((rb, RET_DV), lambda b, h, c: (prow(b, h, c), h)),
                   pl.BlockSpec((None, None, RET_DK, RET_DV), lambda b, h, c: (b, h, 0, 0)),
                   pl.BlockSpec((RET_DEC_ROWS, RET_DV), lambda b, h, c: (drow(b, h, c), h)),
                   s_spec],
        out_shape=[jax.ShapeDtypeStruct((TP, RET_HEADS * RET_DV), BF16),
                   jax.ShapeDtypeStruct((B, RET_HEADS, RET_DK, RET_DV), F32),
                   jax.ShapeDtypeStruct((TD, RET_HEADS * RET_DV), BF16),
                   jax.ShapeDtypeStruct(state.shape, state.dtype)],
        scratch_shapes=[pltpu.VMEM((RET_DK, RET_DV), F32)],
        input_output_aliases={17: 3} if aliased else {},
        compiler_params=_params(3),
        name=name,
    )(proj, proj, proj, proj, tabs["dmask_p"], tabs["qdec_p"], tabs["kdec_p"], tabs["cdec_p"],
      proj, proj, proj, proj, tabs["dmask_d"], tabs["qdec_d"], tabs["kdec_d"], tabs["cdec_d"], state, *aliased)


def _headnorm_rope(x, w, cos2, sin2, bd):
    width = x.shape[1]
    sq = (x * x).astype(BF16)
    parts = [_dot(sq[:, 256 * c:256 * (c + 1)], bd) for c in range(width // 256)]
    ss = parts[0] if len(parts) == 1 else jnp.concatenate(parts, axis=1)
    y = (x * lax.rsqrt(ss * (1.0 / HEAD_DIM) + EPS)) * w
    lane = lax.broadcasted_iota(jnp.int32, y.shape, 1)
    first_half = (lane & (HEAD_DIM - 1)) < HEAD_DIM // 2
    rot = jnp.where(first_half, pltpu.roll(y, width - HEAD_DIM // 2, 1), pltpu.roll(y, HEAD_DIM // 2, 1))
    nrep = width // 128
    cosw = jnp.concatenate([cos2] * nrep, axis=1)
    sinw = jnp.concatenate([sin2] * nrep, axis=1)
    return y * cosw + rot * sinw


def _kv_proj_kernel(x_ref, wkv_ref, w_ref, cos_ref, sin_ref, bd_ref, o_ref, v_ref, lo_ref, hi_ref, wb_ref):
    @pl.when(pl.program_id(0) == 0)
    def _():
        wb_ref[...] = wkv_ref[...].astype(BF16)

    kw = N_KV_HEADS * HEAD_DIM
    kv = _dot(x_ref[...], wb_ref[...])
    v_ref[...] = kv[:, kw:]
    kn = _headnorm_rope(kv[:, :kw], w_ref[...], cos_ref[...], sin_ref[...], bd_ref[...])
    o_ref[...] = kn
    low = lax.broadcasted_iota(jnp.int32, (kn.shape[0], 128), 1) < HEAD_DIM
    zero = jnp.zeros((kn.shape[0], 128), F32)
    for c in range(N_KV_HEADS // 2):
        tile = kn[:, 128 * c:128 * (c + 1)]
        swapped = pltpu.roll(tile, HEAD_DIM, 1)
        even, odd = slice(256 * c, 256 * c + 128), slice(256 * c + 128, 256 * (c + 1))
        lo_ref[:, even] = jnp.where(low, tile, zero).astype(lo_ref.dtype)
        hi_ref[:, even] = jnp.where(low, zero, swapped).astype(hi_ref.dtype)
        lo_ref[:, odd] = jnp.where(low, swapped, zero).astype(lo_ref.dtype)
        hi_ref[:, odd] = jnp.where(low, zero, tile).astype(hi_ref.dtype)


def _table_block(i, n_prompt_blocks, blocks_per_seq):
    return jnp.where(i < n_prompt_blocks, i % blocks_per_seq, blocks_per_seq + i - n_prompt_blocks)


def _kv_proj(xn, w_kv, k_norm_w, tabs, L, TP, name):
    T, K = xn.shape
    kw = N_KV_HEADS * HEAD_DIM
    npb, bps = TP // ROW_BLOCK, L // ROW_BLOCK
    tab = lambda i: (_table_block(i, npb, bps), 0)
    kv_spec = pl.BlockSpec((ROW_BLOCK, kw), lambda i: (i, 0))
    pad_spec = pl.BlockSpec((ROW_BLOCK, 128 * N_KV_HEADS), lambda i: (i, 0))
    pad_shape = jax.ShapeDtypeStruct((T, 128 * N_KV_HEADS), BF16)
    return pl.pallas_call(
        _kv_proj_kernel,
        grid=(T // ROW_BLOCK,),
        in_specs=[pl.BlockSpec((ROW_BLOCK, K), lambda i: (i, 0)),
                  pl.BlockSpec((K, 2 * kw), lambda i: (0, 0)),
                  pl.BlockSpec((1, kw), lambda i: (0, 0)),
                  pl.BlockSpec((ROW_BLOCK, 128), tab),
                  pl.BlockSpec((ROW_BLOCK, 128), tab),
                  pl.BlockSpec((256, 256), lambda i: (0, 0))],
        out_specs=[kv_spec, kv_spec, pad_spec, pad_spec],
        out_shape=[jax.ShapeDtypeStruct((T, kw), F32), jax.ShapeDtypeStruct((T, kw), F32), pad_shape, pad_shape],
        scratch_shapes=[pltpu.VMEM((K, 2 * kw), BF16)],
        compiler_params=_params(1),
        name=name,
    )(xn, w_kv, k_norm_w, tabs["att_cos"], tabs["att_sin"], tabs["bd"])


def _q_proj_kernel(x_ref, w_ref, qw_ref, cos_ref, sin_ref, bd_ref, o_ref, wb_ref):
    @pl.when(pl.program_id(0) == 0)
    def _():
        wb_ref[...] = w_ref[...].astype(BF16)

    bd = bd_ref[...]
    for r0 in range(0, x_ref.shape[0], 128):
        rows = slice(r0, r0 + 128)
        q = _dot(x_ref[rows, :], wb_ref[...])
        cos2, sin2 = cos_ref[rows, :], sin_ref[rows, :]
        for c0 in range(0, q.shape[1], 256):
            cols = slice(c0, c0 + 256)
            qc = _headnorm_rope(q[:, cols], qw_ref[:, cols], cos2, sin2, bd)
            o_ref[rows, cols] = (qc * (HEAD_DIM ** -0.5)).astype(o_ref.dtype)


def _q_proj(xn, w, q_norm_w, layer, tabs, L, TP, name):
    T, K = xn.shape
    N = w.shape[-1]
    npb, bps = TP // ROW_BLOCK, L // ROW_BLOCK
    tab = lambda i: (_table_block(i, npb, bps), 0)
    return pl.pallas_call(
        _q_proj_kernel,
        grid=(T // ROW_BLOCK,),
        in_specs=[pl.BlockSpec((ROW_BLOCK, K), lambda i: (i, 0)),
                  pl.BlockSpec((None, K, N), lambda i: (layer, 0, 0)),
                  pl.BlockSpec((None, 1, N), lambda i: (layer, 0, 0)),
                  pl.BlockSpec((ROW_BLOCK, 128), tab),
                  pl.BlockSpec((ROW_BLOCK, 128), tab),
                  pl.BlockSpec((256, 256), lambda i: (0, 0))],
        out_specs=pl.BlockSpec((ROW_BLOCK, N), lambda i: (i, 0)),
        out_shape=jax.ShapeDtypeStruct((T, N), BF16),
        scratch_shapes=[pltpu.VMEM((K, N), BF16)],
        compiler_params=_params(1),
        name=name,
    )(xn, w, q_norm_w, tabs["att_cos"], tabs["att_sin"], tabs["bd"])


def _sink_column(sink_ref, layer, g, rows, rows_per_head):
    head_in_group = lax.broadcasted_iota(jnp.int32, (rows, 1), 0) // rows_per_head
    col = jnp.full((rows, 1), sink_ref[layer, g * GROUP], F32)
    for r in range(1, GROUP):
        col = jnp.where(head_in_group == r, sink_ref[layer, g * GROUP + r], col)
    return col


def _attn_prompt_kernel(sink_ref, q_ref, lop_ref, loc_ref, hip_ref, hic_ref, vp_ref, vc_ref, o_ref, *,
                        layer, n_qblk):
    first = pl.program_id(1) == 0
    qb = q_ref[...]
    klo = jnp.concatenate([lop_ref[...], loc_ref[...]], axis=0)
    khi = jnp.concatenate([hip_ref[...], hic_ref[...]], axis=0)
    v_t = jnp.concatenate([vp_ref[...], vc_ref[...]], axis=0).T.astype(BF16)
    key = lax.broadcasted_iota(jnp.int32, (WINDOW, WINDOW), 0)
    qi = lax.broadcasted_iota(jnp.int32, (WINDOW, WINDOW), 1)
    own = key <= qi
    scores = {}
    for t in range(n_qblk):
        qrows = slice(WINDOW * t, WINDOW * (t + 1))
        krows = slice(WINDOW * t, WINDOW * (t + 2))
        for g in range(N_KV_HEADS):
            gl = slice(128 * g, 128 * (g + 1))
            kk = jnp.concatenate([klo[krows, gl], khi[krows, gl]], axis=0)
            xq = jnp.concatenate([qb[qrows, 256 * g:256 * g + 128], qb[qrows, 256 * g + 128:256 * (g + 1)]],
                                 axis=0)
            scores[t, g] = _dot_nt(kk, xq)
    for t in range(n_qblk):
        qrows = slice(WINDOW * t, WINDOW * (t + 1))
        krows = slice(WINDOW * t, WINDOW * (t + 2))
        pieces = []
        for g in range(N_KV_HEADS):
            s4 = scores[t, g]
            v_g = v_t[HEAD_DIM * g:HEAD_DIM * (g + 1), krows]
            for pair in range(GROUP // 2):
                p2s, invs = [], []
                for parity in range(2):
                    sk = sink_ref[layer, g * GROUP + 2 * pair + parity]
                    blk = s4[2 * WINDOW * parity:2 * WINDOW * (parity + 1), WINDOW * pair:WINDOW * (pair + 1)]
                    s_prev = blk[:WINDOW]
                    if t == 0:
                        s_prev = jnp.where(first, NEG_INF, s_prev)
                    s = jnp.where(own, blk[WINDOW:], s_prev)
                    m = jnp.maximum(jnp.max(s, axis=0, keepdims=True), sk)
                    p = jnp.exp(s - m)
                    invs.append(1.0 / (jnp.sum(p, axis=0, keepdims=True) + jnp.exp(sk - m)))
                    p2s.append(jnp.concatenate([jnp.where(own, 0.0, p), jnp.where(own, p, 0.0)], axis=0))
                p2 = jnp.concatenate(p2s, axis=1).astype(BF16)
                o_t = _dot(v_g, p2) * jnp.concatenate(invs, axis=1)
                pieces += [o_t[:, :WINDOW], o_t[:, WINDOW:]]
        o_ref[qrows, :] = jnp.concatenate(pieces, axis=0).T.astype(o_ref.dtype)


def _attn_prompt(q, klo, khi, v, sinks, layer, B, L, name):
    nb = L // WINDOW
    nq = ATTN_Q_BLOCKS if nb % ATTN_Q_BLOCKS == 0 else 1
    ns = nb // nq
    kw = N_KV_HEADS * HEAD_DIM
    kpad = 128 * N_KV_HEADS
    cur = lambda b, i: b * ns + i
    prev = lambda b, i: b * nb + jnp.maximum(i * nq - 1, 0)
    kern = functools.partial(_attn_prompt_kernel, layer=layer, n_qblk=nq)
    return pl.pallas_call(
        kern,
        grid=(B, ns),
        in_specs=[pl.BlockSpec(memory_space=pltpu.SMEM),
                  pl.BlockSpec((nq * WINDOW, D_MODEL), lambda b, i: (cur(b, i), 0)),
                  pl.BlockSpec((WINDOW, kpad), lambda b, i: (prev(b, i), 0)),
                  pl.BlockSpec((nq * WINDOW, kpad), lambda b, i: (cur(b, i), 0)),
                  pl.BlockSpec((WINDOW, kpad), lambda b, i: (prev(b, i), 0)),
                  pl.BlockSpec((nq * WINDOW, kpad), lambda b, i: (cur(b, i), 0)),
                  pl.BlockSpec((WINDOW, kw), lambda b, i: (prev(b, i), 0)),
                  pl.BlockSpec((nq * WINDOW, kw), lambda b, i: (cur(b, i), 0))],
        out_specs=pl.BlockSpec((nq * WINDOW, D_MODEL), lambda b, i: (cur(b, i), 0)),
        out_shape=jax.ShapeDtypeStruct((B * L, D_MODEL), BF16),
        compiler_params=_params(2),
        name=name,
    )(sinks, q, klo, klo, khi, khi, v, v)


def _attn_decode_kernel(sink_ref, q_ref, kn_ref, vn_ref, ck_ref, cv_ref, o_ref, *, layer, t_dec):
    nseq = DEC_ROWS // t_dec
    wc = ck_ref.shape[1]
    qb = q_ref[...]
    knew = kn_ref[...].astype(BF16)
    vnew = vn_ref[...].astype(BF16)
    rows = GROUP * DEC_ROWS
    row = lax.broadcasted_iota(jnp.int32, (rows, 1), 0)
    row_seq = (row % DEC_ROWS) // t_dec
    row_tok = row % t_dec
    jold = lax.broadcasted_iota(jnp.int32, (rows, wc), 1)
    vis_old = (jold > row_tok + (wc - WINDOW)) & (jold <= row_tok + wc)
    cnew = lax.broadcasted_iota(jnp.int32, (rows, DEC_ROWS), 1)
    vis_new = (cnew // t_dec == row_seq) & (cnew % t_dec <= row_tok)
    for g in range(N_KV_HEADS):
        heads = [g * GROUP + r for r in range(GROUP)]
        hs = slice(HEAD_DIM * g, HEAD_DIM * (g + 1))
        qs = jnp.concatenate([qb[:, HEAD_DIM * h:HEAD_DIM * (h + 1)] for h in heads], axis=0)
        s_old = jnp.zeros((rows, wc), F32)
        for bi in range(nseq):
            kc = ck_ref[bi][:, hs].astype(BF16)
            s_old = s_old + jnp.where(row_seq == bi, _dot_nt(qs, kc), 0.0)
        s_old = jnp.where(vis_old, s_old, NEG_INF)
        s_new = jnp.where(vis_new, _dot_nt(qs, knew[:, hs]), NEG_INF)
        sk = _sink_column(sink_ref, layer, g, rows, DEC_ROWS)
        m = jnp.maximum(jnp.maximum(jnp.max(s_old, axis=-1, keepdims=True),
                                    jnp.max(s_new, axis=-1, keepdims=True)), sk)
        p_old = jnp.exp(s_old - m)
        p_new = jnp.exp(s_new - m)
        denom = (jnp.sum(p_old, axis=-1, keepdims=True) + jnp.sum(p_new, axis=-1, keepdims=True)
                 + jnp.exp(sk - m))
        inv = 1.0 / denom
        o = _dot((p_new * inv).astype(BF16), vnew[:, hs])
        pn_old = p_old * inv
        for bi in range(nseq):
            vc = cv_ref[bi][:, hs].astype(BF16)
            o = o + _dot(jnp.where(row_seq == bi, pn_old, 0.0).astype(BF16), vc)
        for r, h in enumerate(heads):
            o_ref[:, HEAD_DIM * h:HEAD_DIM * (h + 1)] = o[DEC_ROWS * r:DEC_ROWS * (r + 1)].astype(o_ref.dtype)


def _attn_decode(q, kn, v, cache_k, cache_v, sinks, layer, TP, DB, t_dec, name):
    nseq = DEC_ROWS // t_dec
    r0 = TP // DEC_ROWS
    kw = N_KV_HEADS * HEAD_DIM
    wc = cache_k.shape[1]
    kern = functools.partial(_attn_decode_kernel, layer=layer, t_dec=t_dec)
    return pl.pallas_call(
        kern,
        grid=(DB // nseq,),
        in_specs=[pl.BlockSpec(memory_space=pltpu.SMEM),
                  pl.BlockSpec((DEC_ROWS, D_MODEL), lambda i: (r0 + i, 0)),
                  pl.BlockSpec((DEC_ROWS, kw), lambda i: (r0 + i, 0)),
                  pl.BlockSpec((DEC_ROWS, kw), lambda i: (r0 + i, 0)),
                  pl.BlockSpec((nseq, wc, kw), lambda i: (i, 0, 0)),
                  pl.BlockSpec((nseq, wc, kw), lambda i: (i, 0, 0))],
        out_specs=pl.BlockSpec((DEC_ROWS, D_MODEL), lambda i: (i, 0)),
        out_shape=jax.ShapeDtypeStruct((DB * t_dec, D_MODEL), BF16),
        compiler_params=_params(1),
        name=name,
    )(sinks, q, kn, v, cache_k, cache_v)


def _rope_cos_sin(pos, half):
    inv = 1.0 / (ROPE_THETA ** (jnp.arange(half, dtype=F32) / half))
    ang = pos.astype(F32)[:, None] * inv[None, :]
    return jnp.cos(ang), jnp.sin(ang)


def _decay_tables(chunk, reps):
    lg = jnp.log(1.0 - 2.0 ** (-5.0 - jnp.arange(RET_HEADS, dtype=F32)))
    idx = jnp.arange(chunk, dtype=F32)
    rel = idx[:, None] - idx[None, :]
    dmask = jnp.where(rel >= 0, jnp.exp(lg[:, None, None] * jnp.maximum(rel, 0.0)), 0.0)
    qdec = jnp.exp(lg[:, None] * (idx + 1.0))[:, :, None]
    kdec = jnp.exp(lg[:, None] * (chunk - 1.0 - idx))[:, :, None]
    cdec = jnp.exp(lg * chunk)[:, None, None]
    if reps > 1:
        eye = jnp.eye(reps, dtype=F32)
        dmask = jnp.einsum("ab,hij->haibj", eye, dmask).reshape(RET_HEADS, reps * chunk, reps * chunk)
        qdec = jnp.tile(qdec, (1, reps, 1))
        kdec = jnp.tile(kdec, (1, reps, 1))
    return dmask, qdec, kdec, cdec


def _tables(B, L, TD, t_dec, chunk_p):
    pos = jnp.concatenate([jnp.arange(L), PAST_LEN + jnp.arange(TD) % t_dec])
    ret_cos, ret_sin = _rope_cos_sin(pos, RET_DK // 2)
    per_row = lambda t: jnp.concatenate([jnp.tile(t[:L], (B, 1)), t[L:]], axis=0)
    c, s = _rope_cos_sin(pos, HEAD_DIM // 2)
    att_cos = jnp.concatenate([c, c, c, c], axis=1)
    att_sin = jnp.concatenate([-s, s, -s, s], axis=1)
    dmask_p, qdec_p, kdec_p, cdec_p = _decay_tables(chunk_p, 1)
    dmask_d, qdec_d, kdec_d, cdec_d = _decay_tables(math.gcd(t_dec, RET_CHUNK), RET_DEC_ROWS // t_dec)
    head_of_lane = np.arange(256) // HEAD_DIM
    bd = jnp.asarray((head_of_lane[:, None] == head_of_lane[None, :]).astype(np.float32), dtype=BF16)
    return dict(ret_cos_rows=per_row(ret_cos), ret_sin_rows=per_row(ret_sin), att_cos=att_cos, att_sin=att_sin,
                chunk_p=chunk_p, dmask_p=dmask_p, qdec_p=qdec_p, kdec_p=kdec_p, cdec_p=cdec_p,
                dmask_d=dmask_d, qdec_d=qdec_d, kdec_d=kdec_d, cdec_d=cdec_d, bd=bd)


def kernel(x_prompt, x_sample, state_ret, cache_k, cache_v, ln_ret, w_ret_in, w_ret_out, ln_ffn, w_ffn_in,
           w_ffn_out, ln_kv, w_kv, k_norm, ln_attn, w_q, q_norm, sinks, w_o):
    B, L, D = x_prompt.shape
    DB, t_dec, _ = x_sample.shape
    n_ret = w_ret_in.shape[0]
    n_attn = w_q.shape[0]
    TP, TD = B * L, DB * t_dec
    T = TP + TD
    wc = cache_k.shape[1]
    kw = N_KV_HEADS * HEAD_DIM
    assert D == D_MODEL and L % RET_CHUNK == 0 and L % ROW_BLOCK == 0 and TD % ROW_BLOCK == 0
    assert t_dec == math.gcd(t_dec, RET_CHUNK) and DEC_ROWS % t_dec == 0 and wc == WINDOW and PAST_LEN >= WINDOW

    tabs = _tables(B, L, TD, t_dec, RET_CHUNK_PROMPT if L % RET_CHUNK_PROMPT == 0 else RET_CHUNK)
    ck = cache_k.reshape(DB, wc, kw)
    cv = cache_v.reshape(DB, wc, kw)
    q_norm_w = jnp.tile(q_norm, (1, N_HEADS))[:, None, :]
    k_norm_w = jnp.tile(k_norm[None, :], (1, N_KV_HEADS))
    ln_ret, ln_ffn, ln_attn = ln_ret[:, None, :], ln_ffn[:, None, :], ln_attn[:, None, :]

    ln_kv3 = ln_kv[None, None, :]
    prompt_states = []
    dec_states = None
    x, xn = _embed(x_prompt.reshape(TP, D), x_sample.reshape(TD, D), (ln_ret, 0), "embed")
    for l in range(n_ret):
        proj = _ret_in(xn, w_ret_in, l, tabs, f"ret_in_{l}")
        go_p, s_p, go_d, dec_states = _retention(proj, state_ret, l, dec_states, tabs, B, L, DB, t_dec,
                                                 f"retention_{l}")
        prompt_states.append(s_p)
        x, xn = _matmul_residual((go_p, go_d), w_ret_out, l, x, [(ln_ffn, l)], f"ret_out_{l}", TP)
        h = _swiglu(xn, w_ffn_in, l, f"ffn_in_{l}")
        if l + 1 < n_ret:
            x, xn = _matmul_residual(h, w_ffn_out, l, x, [(ln_ret, l + 1)], f"ffn_out_{l}", TP)
        else:
            x, xn_kv, xn = _matmul_residual(h, w_ffn_out, l, x, [(ln_kv3, 0), (ln_attn, 0)], f"ffn_out_{l}", TP)

    kn, v, klo, khi = _kv_proj(xn_kv, w_kv, k_norm_w, tabs, L, TP, "kv_proj")
    for j in range(n_attn):
        layer = n_ret + j
        q = _q_proj(xn, w_q, q_norm_w, j, tabs, L, TP, f"q_proj_{j}")
        ao_p = _attn_prompt(q, klo, khi, v, sinks, j, B, L, f"attn_prompt_{j}")
        ao_d = _attn_decode(q, kn, v, ck, cv, sinks, j, TP, DB, t_dec, f"attn_decode_{j}")
        x, xn = _matmul_residual((ao_p, ao_d), w_o, j, x, [(ln_ffn, layer)], f"attn_out_{j}", TP)
        h = _swiglu(xn, w_ffn_in, layer, f"ffn_in_{layer}")
        if j + 1 < n_attn:
            x, xn = _matmul_residual(h, w_ffn_out, layer, x, [(ln_attn, j + 1)], f"ffn_out_{layer}", TP)
        else:
            y_p, y_s = _matmul_residual(h, w_ffn_out, layer, x, [], f"ffn_out_{layer}", TP, split_out=True)

    y_prompt = y_p.reshape(B, L, D)
    y_sample = y_s.reshape(DB, t_dec, D)
    state_prompt = jnp.stack(prompt_states)
    w_keep = min(WINDOW, L)
    kn_p = kn[:TP].reshape(B, L, N_KV_HEADS, HEAD_DIM)
    v_p = v[:TP].reshape(B, L, N_KV_HEADS, HEAD_DIM)
    kn_d = kn[TP:].reshape(DB, t_dec, N_KV_HEADS, HEAD_DIM)
    v_d = v[TP:].reshape(DB, t_dec, N_KV_HEADS, HEAD_DIM)
    cache_k_sample = jnp.concatenate([cache_k, kn_d], axis=1)[:, -wc:]
    cache_v_sample = jnp.concatenate([cache_v, v_d], axis=1)[:, -wc:]
    return (y_prompt, y_sample, state_prompt, dec_states, kn_p[:, -w_keep:], v_p[:, -w_keep:],
            cache_k_sample, cache_v_sample)
```

```python
import functools
import math

import numpy as np
import jax
import jax.numpy as jnp
from jax import lax
from jax.experimental import pallas as pl
from jax.experimental.pallas import tpu as pltpu

D_MODEL = 1024
PAST_LEN = 8192
RET_HEADS = 4
RET_DK = 256
RET_DV = 512
RET_CHUNK = 128
RET_CHUNK_PROMPT = 256
N_HEADS = 16
N_KV_HEADS = 4
HEAD_DIM = 64
GROUP = N_HEADS // N_KV_HEADS
WINDOW = 128
D_FF = 2816
ROPE_THETA = 10000.0
EPS = 1e-6
NEG_INF = -1e30

F32 = jnp.float32
BF16 = jnp.bfloat16

V7X_VMEM_LIMIT_BYTES = 56 * 1024 * 1024
DEC_ROWS = 16
RET_DEC_ROWS = 32
ROW_BLOCK = 512
SUB_ROWS = 256
RET_UNROLL = 4
ATTN_Q_BLOCKS = 2


def _dot(a, b):
    return jnp.dot(a, b, preferred_element_type=F32)


def _dot_nt(a, b):
    return lax.dot_general(a, b, (((1,), (1,)), ((), ())), preferred_element_type=F32)


def _dot_tn(a, b):
    return lax.dot_general(a, b, (((0,), (0,)), ((), ())), preferred_element_type=F32)


def _params(n_axes):
    return pltpu.CompilerParams(dimension_semantics=("arbitrary",) * n_axes,
                                vmem_limit_bytes=V7X_VMEM_LIMIT_BYTES)


def _row_tile(rows, prefs):
    for t in prefs:
        if rows % t == 0:
            return t
    raise ValueError(f"no row tile for {rows}")


def _rms_normed(x, gain_refs):
    ms = jnp.mean(x * x, axis=-1, keepdims=True)
    xh = x * lax.rsqrt(ms + EPS)
    return [(xh * g_ref[...]).astype(BF16) for g_ref in gain_refs]


def _embed_kernel(xp_ref, xs_ref, g_ref, xn_ref, *, n_prompt_blocks):
    i = pl.program_id(0)

    @pl.when(i < n_prompt_blocks)
    def _():
        xn_ref[...] = _rms_normed(xp_ref[...], [g_ref])[0]

    @pl.when(i >= n_prompt_blocks)
    def _():
        xn_ref[...] = _rms_normed(xs_ref[...], [g_ref])[0]


def _embed(xp, xs, gain, name):
    TP, D = xp.shape
    TD = xs.shape[0]
    npb = TP // ROW_BLOCK
    T = TP + TD
    g_arr, g_idx = gain
    blk = lambda fn: pl.BlockSpec((ROW_BLOCK, D), fn)
    return pl.pallas_call(
        functools.partial(_embed_kernel, n_prompt_blocks=npb),
        grid=(T // ROW_BLOCK,),
        in_specs=[blk(lambda i: (jnp.minimum(i, npb - 1), 0)),
                  blk(lambda i: (jnp.maximum(i - npb, 0), 0)),
                  pl.BlockSpec((None, 1, D), lambda i: (g_idx, 0, 0))],
        out_specs=blk(lambda i: (i, 0)),
        out_shape=jax.ShapeDtypeStruct((T, D), BF16),
        compiler_params=_params(1),
        name=name,
    )(xp, xs, g_arr)


def _sub_rows(tm):
    return max(r for r in range(16, 3 * SUB_ROWS + 1, 16) if tm % r == 0)


def _ret_in_kernel(x_ref, w_ref, cos_ref, sin_ref, o_ref, *, n_q_tiles):
    j = pl.program_id(1)
    sub = _sub_rows(x_ref.shape[0])
    half = RET_DK // 2

    @pl.when(j >= 2 * n_q_tiles)
    def _():
        o_ref[...] = _dot(x_ref[...], w_ref[...].astype(BF16)).astype(o_ref.dtype)

    @pl.when(j < 2 * n_q_tiles)
    def _():
        scale = jnp.where(j < n_q_tiles, 1.0, RET_DK ** -0.5)
        w = w_ref[...].astype(BF16)
        for r0 in range(0, x_ref.shape[0], sub):
            rows = slice(r0, r0 + sub)
            acc = _dot(x_ref[rows, :], w)
            cos = cos_ref[rows, :]
            sin = sin_ref[rows, :]
            for c0 in range(0, acc.shape[1], RET_DK):
                x1, x2 = acc[:, c0:c0 + half], acc[:, c0 + half:c0 + RET_DK]
                o_ref[rows, c0:c0 + half] = ((x1 * cos - x2 * sin) * scale).astype(o_ref.dtype)
                o_ref[rows, c0 + half:c0 + RET_DK] = ((x2 * cos + x1 * sin) * scale).astype(o_ref.dtype)


def _ret_in(xn, w, layer, tabs, name):
    T, K = xn.shape
    N = w.shape[-1]
    tn = 1024
    tm = _row_tile(T, (1536, 1024, 512))
    return pl.pallas_call(
        functools.partial(_ret_in_kernel, n_q_tiles=RET_HEADS * RET_DK // tn),
        grid=(T // tm, N // tn),
        in_specs=[pl.BlockSpec((tm, K), lambda i, j: (i, 0)),
                  pl.BlockSpec((None, K, tn), lambda i, j: (layer, 0, j)),
                  pl.BlockSpec((tm, RET_DK // 2), lambda i, j: (i, 0)),
                  pl.BlockSpec((tm, RET_DK // 2), lambda i, j: (i, 0))],
        out_specs=pl.BlockSpec((tm, tn), lambda i, j: (i, j)),
        out_shape=jax.ShapeDtypeStruct((T, N), BF16),
        compiler_params=_params(2),
        name=name,
    )(xn, w, tabs["ret_cos_rows"], tabs["ret_sin_rows"])


def _swiglu_kernel(x_ref, wa_ref, wb_ref, o_ref):
    wa = wa_ref[...].astype(BF16)
    wb = wb_ref[...].astype(BF16)
    sub = _sub_rows(x_ref.shape[0])
    for r0 in range(0, x_ref.shape[0], sub):
        rows = slice(r0, r0 + sub)
        a = _dot(x_ref[rows, :], wa)
        b = _dot(x_ref[rows, :], wb)
        o_ref[rows, :] = ((a * jax.nn.sigmoid(a)) * b).astype(o_ref.dtype)


def _swiglu(xn, w, layer, name):
    T, K = xn.shape
    tn = 256
    nb = D_FF // tn
    tm = _row_tile(T, (2816, 1536, 1024, 512))
    return pl.pallas_call(
        _swiglu_kernel,
        grid=(T // tm, nb),
        in_specs=[pl.BlockSpec((tm, K), lambda i, j: (i, 0)),
                  pl.BlockSpec((None, K, tn), lambda i, j: (layer, 0, j)),
                  pl.BlockSpec((None, K, tn), lambda i, j: (layer, 0, nb + j))],
        out_specs=pl.BlockSpec((tm, tn), lambda i, j: (i, j)),
        out_shape=jax.ShapeDtypeStruct((T, D_FF), BF16),
        compiler_params=_params(2),
        name=name,
    )(xn, w, w)


def _mm_res_kernel(*refs, n_a, n_r, n_gains, n_prompt_blocks, split_out):
    a_refs = refs[:n_a]
    w_ref = refs[n_a]
    r_refs = refs[n_a + 1:n_a + 1 + n_r]
    rest = refs[n_a + 1 + n_r:]
    gain_refs = rest[:n_gains]
    out_refs = rest[n_gains:-1]
    wb_ref = rest[-1]
    i = pl.program_id(0)

    @pl.when(i == 0)
    def _():
        wb_ref[...] = w_ref[...].astype(BF16)

    def step(a_ref, r_ref, dst):
        for r0 in range(0, a_ref.shape[0], SUB_ROWS):
            rows = slice(r0, r0 + SUB_ROWS)
            x = r_ref[rows, :] + _dot(a_ref[rows, :], wb_ref[...])
            out_refs[dst][rows, :] = x
            if not split_out:
                for o_ref, xn in zip(out_refs[1:], _rms_normed(x, gain_refs)):
                    o_ref[rows, :] = xn

    if n_a == 1 and n_r == 1 and not split_out:
        step(a_refs[0], r_refs[0], 0)
    else:
        @pl.when(i < n_prompt_blocks)
        def _():
            step(a_refs[0], r_refs[0], 0)

        @pl.when(i >= n_prompt_blocks)
        def _():
            step(a_refs[-1], r_refs[-1], 1 if split_out else 0)


def _matmul_residual(a, w, layer, res, gains, name, TP, split_out=False):
    a = a if isinstance(a, (tuple, list)) else (a,)
    res = res if isinstance(res, (tuple, list)) else (res,)
    T = sum(r.shape[0] for r in res)
    N = res[0].shape[1]
    K = a[0].shape[1]
    tm = ROW_BLOCK
    npb = TP // tm
    head = lambda i: (jnp.minimum(i, npb - 1), 0)
    tail = lambda i: (jnp.maximum(i - npb, 0), 0)
    whole = lambda i: (i, 0)
    rows_of = lambda arrs, width: [pl.BlockSpec((tm, width), fn)
                                   for fn in ((whole,) if len(arrs) == 1 else (head, tail))]
    in_specs = rows_of(a, K) + [pl.BlockSpec((None, K, N), lambda i: (layer, 0, 0))] + rows_of(res, N)
    in_specs += [pl.BlockSpec((None, 1, N), lambda i, idx=idx: (idx, 0, 0)) for _, idx in gains]
    if split_out:
        assert not gains
        out_specs = [pl.BlockSpec((tm, N), head), pl.BlockSpec((tm, N), tail)]
        out_shape = [jax.ShapeDtypeStruct((TP, N), F32), jax.ShapeDtypeStruct((T - TP, N), F32)]
    else:
        out_specs = [pl.BlockSpec((tm, N), whole)] * (1 + len(gains))
        out_shape = [jax.ShapeDtypeStruct((T, N), F32)] + [jax.ShapeDtypeStruct((T, N), BF16)] * len(gains)
    aliases = {len(a) + 1: 0} if len(res) == 1 and not split_out else {}
    return pl.pallas_call(
        functools.partial(_mm_res_kernel, n_a=len(a), n_r=len(res), n_gains=len(gains), n_prompt_blocks=npb,
                          split_out=split_out),
        grid=(T // tm,),
        in_specs=in_specs,
        out_specs=out_specs,
        out_shape=out_shape,
        scratch_shapes=[pltpu.VMEM((K, N), BF16)],
        input_output_aliases=aliases,
        compiler_params=_params(1),
        name=name,
    )(*a, w, *res, *[g for g, _ in gains])


def _groupnorm_gate(o, gate):
    mu = jnp.mean(o, axis=-1, keepdims=True)
    d = o - mu
    var = jnp.mean(d * d, axis=-1, keepdims=True)
    on = d * lax.rsqrt(var + EPS)
    return (gate * jax.nn.sigmoid(gate)) * on


def _retention_kernel(q_ref, k_ref, v_ref, g_ref, dmask_ref, qdec_ref, kdec_ref, cdec_ref,
                      qd_ref, kd_ref, vd_ref, gd_ref, dmaskd_ref, qdecd_ref, kdecd_ref, cdecd_ref, sd_ref,
                      *rest, chunk_rows, n_chunks, t_dec):
    go_ref, sfin_ref, god_ref, snew_ref, s_ref = rest[-5:]
    c = pl.program_id(2)

    qb = qd_ref[...]
    kb = kd_ref[...]
    vb = vd_ref[...]
    cdecd = cdecd_ref[...]
    inner = _dot((_dot_nt(qb, kb) * dmaskd_ref[...]).astype(BF16), vb)
    kdd = kb.astype(F32) * kdecd_ref[...]
    seq_of_row = lax.broadcasted_iota(jnp.int32, (RET_DEC_ROWS, 1), 0) // t_dec
    cross = jnp.zeros((RET_DEC_ROWS, RET_DV), F32)
    for bi in range(RET_DEC_ROWS // t_dec):
        mine = seq_of_row == bi
        s = sd_ref[bi]
        cross = cross + jnp.where(mine, _dot(qb, s.astype(BF16)), 0.0)
        snew_ref[bi] = s * cdecd + _dot_tn(jnp.where(mine, kdd, 0.0).astype(BF16), vb)
    cross = cross * qdecd_ref[...]
    god_ref[...] = _groupnorm_gate(inner + cross, gd_ref[...].astype(F32)).astype(god_ref.dtype)

    @pl.when(c == 0)
    def _():
        s_ref[...] = jnp.zeros_like(s_ref)

    dmask = dmask_ref[...]
    qdec = qdec_ref[...]
    kdec = kdec_ref[...]
    cdec = cdec_ref[...]

    def chunk(ci, carry):
        rows = pl.ds(pl.multiple_of(ci * chunk_rows, chunk_rows), chunk_rows)
        qb = q_ref[rows, :]
        kb = k_ref[rows, :]
        vb = v_ref[rows, :]
        s = s_ref[...]
        scores = _dot_nt(qb, kb) * dmask
        inner = _dot(scores.astype(BF16), vb)
        cross = _dot(qb, s.astype(BF16)) * qdec
        s_ref[...] = s * cdec + _dot_tn((kb.astype(F32) * kdec).astype(BF16), vb)
        go_ref[rows, :] = _groupnorm_gate(inner + cross, g_ref[rows, :].astype(F32)).astype(go_ref.dtype)
        return carry

    lax.fori_loop(0, n_chunks, chunk, 0, unroll=min(n_chunks, RET_UNROLL))

    @pl.when(c == pl.num_programs(2) - 1)
    def _():
        sfin_ref[...] = s_ref[...]


def _retention(proj, state, layer, snew_all, tabs, B, L, DB, t_dec, name):
    TP, TD = B * L, DB * t_dec
    nseq = RET_DEC_ROWS // t_dec
    steps = DB // nseq
    assert DB % nseq == 0 and steps % B == 0, (DB, nseq, B)
    nblk = steps // B
    rb = L // nblk
    ck = tabs["chunk_p"]
    assert L % nblk == 0 and rb % ck == 0, (L, nblk, ck)
    r0 = TP // RET_DEC_ROWS
    kcol = vcol = RET_HEADS
    gcol = 2 * RET_HEADS
    prow = lambda b, h, c: b * nblk + c
    drow = lambda b, h, c: b * nblk + c
    dec = lambda width, col0: pl.BlockSpec((RET_DEC_ROWS, width), lambda b, h, c: (r0 + drow(b, h, c), col0 + h))
    per_head = lambda *shape: pl.BlockSpec((None,) + shape, lambda b, h, c: (h,) + (0,) * len(shape))
    s_spec = pl.BlockSpec((None, nseq, None, RET_DK, RET_DV), lambda b, h, c: (layer, drow(b, h, c), h, 0, 0))
    aliased = [] if snew_all is None else [snew_all]
    kern = functools.partial(_retention_kernel, chunk_rows=ck, n_chunks=rb // ck, t_dec=t_dec)
    return pl.pallas_call(
        kern,
        grid=(B, RET_HEADS, nblk),
        in_specs=[pl.BlockSpec((rb, RET_DK), lambda b, h, c: (prow(b, h, c), h)),
                  pl.BlockSpec((rb, RET_DK), lambda b, h, c: (prow(b, h, c), kcol + h)),
                  pl.BlockSpec((rb, RET_DV), lambda b, h, c: (prow(b, h, c), vcol + h)),
                  pl.BlockSpec((rb, RET_DV), lambda b, h, c: (prow(b, h, c), gcol + h)),
                  per_head(ck, ck), per_head(ck, 1), per_head(ck, 1), per_head(1, 1),
                  dec(RET_DK, 0), dec(RET_DK, kcol), dec(RET_DV, vcol), dec(RET_DV, gcol),
                  per_head(RET_DEC_ROWS, RET_DEC_ROWS), per_head(RET_DEC_ROWS, 1), per_head(RET_DEC_ROWS, 1),
                  per_head(1, 1), s_spec] + [pl.BlockSpec(memory_space=pl.ANY)] * len(aliased),
        out_specs=[pl.BlockSpec((rb, RET_DV), lambda b, h, c: (prow(b, h, c), h)),
                   pl.BlockSpec((None, None, RET_DK, RET_DV), lambda b, h, c: (b, h, 0, 0)),
                   pl.BlockSpec((RET_DEC_ROWS, RET_DV), lambda b, h, c: (drow(b, h, c), h)),
                   s_spec],
        out_shape=[jax.ShapeDtypeStruct((TP, RET_HEADS * RET_DV), BF16),
                   jax.ShapeDtypeStruct((B, RET_HEADS, RET_DK, RET_DV), F32),
                   jax.ShapeDtypeStruct((TD, RET_HEADS * RET_DV), BF16),
                   jax.ShapeDtypeStruct(state.shape, state.dtype)],
        scratch_shapes=[pltpu.VMEM((RET_DK, RET_DV), F32)],
        input_output_aliases={17: 3} if aliased else {},
        compiler_params=_params(3),
        name=name,
    )(proj, proj, proj, proj, tabs["dmask_p"], tabs["qdec_p"], tabs["kdec_p"], tabs["cdec_p"],
      proj, proj, proj, proj, tabs["dmask_d"], tabs["qdec_d"], tabs["kdec_d"], tabs["cdec_d"], state, *aliased)


def _headnorm_rope(x, w, cos2, sin2, bd):
    width = x.shape[1]
    sq = (x * x).astype(BF16)
    parts = [_dot(sq[:, 256 * c:256 * (c + 1)], bd) for c in range(width // 256)]
    ss = parts[0] if len(parts) == 1 else jnp.concatenate(parts, axis=1)
    y = (x * lax.rsqrt(ss * (1.0 / HEAD_DIM) + EPS)) * w
    lane = lax.broadcasted_iota(jnp.int32, y.shape, 1)
    first_half = (lane & (HEAD_DIM - 1)) < HEAD_DIM // 2
    rot = jnp.where(first_half, pltpu.roll(y, width - HEAD_DIM // 2, 1), pltpu.roll(y, HEAD_DIM // 2, 1))
    nrep = width // 128
    cosw = jnp.concatenate([cos2] * nrep, axis=1)
    sinw = jnp.concatenate([sin2] * nrep, axis=1)
    return y * cosw + rot * sinw


def _kv_proj_kernel(x_ref, wkv_ref, w_ref, cos_ref, sin_ref, bd_ref, o_ref, v_ref, lo_ref, hi_ref, wb_ref):
    @pl.when(pl.program_id(0) == 0)
    def _():
        wb_ref[...] = wkv_ref[...].astype(BF16)

    kw = N_KV_HEADS * HEAD_DIM
    kv = _dot(x_ref[...], wb_ref[...])
    v_ref[...] = kv[:, kw:]
    kn = _headnorm_rope(kv[:, :kw], w_ref[...], cos_ref[...], sin_ref[...], bd_ref[...])
    o_ref[...] = kn
    low = lax.broadcasted_iota(jnp.int32, (kn.shape[0], 128), 1) < HEAD_DIM
    zero = jnp.zeros((kn.shape[0], 128), F32)
    for c in range(N_KV_HEADS // 2):
        tile = kn[:, 128 * c:128 * (c + 1)]
        swapped = pltpu.roll(tile, HEAD_DIM, 1)
        even, odd = slice(256 * c, 256 * c + 128), slice(256 * c + 128, 256 * (c + 1))
        lo_ref[:, even] = jnp.where(low, tile, zero).astype(lo_ref.dtype)
        hi_ref[:, even] = jnp.where(low, zero, swapped).astype(hi_ref.dtype)
        lo_ref[:, odd] = jnp.where(low, swapped, zero).astype(lo_ref.dtype)
        hi_ref[:, odd] = jnp.where(low, zero, tile).astype(hi_ref.dtype)


def _table_block(i, n_prompt_blocks, blocks_per_seq):
    return jnp.where(i < n_prompt_blocks, i % blocks_per_seq, blocks_per_seq + i - n_prompt_blocks)


def _kv_proj(xn, w_kv, k_norm_w, tabs, L, TP, name):
    T, K = xn.shape
    kw = N_KV_HEADS * HEAD_DIM
    npb, bps = TP // ROW_BLOCK, L // ROW_BLOCK
    tab = lambda i: (_table_block(i, npb, bps), 0)
    kv_spec = pl.BlockSpec((ROW_BLOCK, kw), lambda i: (i, 0))
    pad_spec = pl.BlockSpec((ROW_BLOCK, 128 * N_KV_HEADS), lambda i: (i, 0))
    pad_shape = jax.ShapeDtypeStruct((T, 128 * N_KV_HEADS), BF16)
    return pl.pallas_call(
        _kv_proj_kernel,
        grid=(T // ROW_BLOCK,),
        in_specs=[pl.BlockSpec((ROW_BLOCK, K), lambda i: (i, 0)),
                  pl.BlockSpec((K, 2 * kw), lambda i: (0, 0)),
                  pl.BlockSpec((1, kw), lambda i: (0, 0)),
                  pl.BlockSpec((ROW_BLOCK, 128), tab),
                  pl.BlockSpec((ROW_BLOCK, 128), tab),
                  pl.BlockSpec((256, 256), lambda i: (0, 0))],
        out_specs=[kv_spec, kv_spec, pad_spec, pad_spec],
        out_shape=[jax.ShapeDtypeStruct((T, kw), F32), jax.ShapeDtypeStruct((T, kw), F32), pad_shape, pad_shape],
        scratch_shapes=[pltpu.VMEM((K, 2 * kw), BF16)],
        compiler_params=_params(1),
        name=name,
    )(xn, w_kv, k_norm_w, tabs["att_cos"], tabs["att_sin"], tabs["bd"])


def _q_proj_kernel(x_ref, w_ref, qw_ref, cos_ref, sin_ref, bd_ref, o_ref, wb_ref):
    @pl.when(pl.program_id(0) == 0)
    def _():
        wb_ref[...] = w_ref[...].astype(BF16)

    bd = bd_ref[...]
    for r0 in range(0, x_ref.shape[0], 128):
        rows = slice(r0, r0 + 128)
        q = _dot(x_ref[rows, :], wb_ref[...])
        cos2, sin2 = cos_ref[rows, :], sin_ref[rows, :]
        for c0 in range(0, q.shape[1], 256):
            cols = slice(c0, c0 + 256)
            qc = _headnorm_rope(q[:, cols], qw_ref[:, cols], cos2, sin2, bd)
            o_ref[rows, cols] = (qc * (HEAD_DIM ** -0.5)).astype(o_ref.dtype)


def _q_proj(xn, w, q_norm_w, layer, tabs, L, TP, name):
    T, K = xn.shape
    N = w.shape[-1]
    npb, bps = TP // ROW_BLOCK, L // ROW_BLOCK
    tab = lambda i: (_table_block(i, npb, bps), 0)
    return pl.pallas_call(
        _q_proj_kernel,
        grid=(T // ROW_BLOCK,),
        in_specs=[pl.BlockSpec((ROW_BLOCK, K), lambda i: (i, 0)),
                  pl.BlockSpec((None, K, N), lambda i: (layer, 0, 0)),
                  pl.BlockSpec((None, 1, N), lambda i: (layer, 0, 0)),
                  pl.BlockSpec((ROW_BLOCK, 128), tab),
                  pl.BlockSpec((ROW_BLOCK, 128), tab),
                  pl.BlockSpec((256, 256), lambda i: (0, 0))],
        out_specs=pl.BlockSpec((ROW_BLOCK, N), lambda i: (i, 0)),
        out_shape=jax.ShapeDtypeStruct((T, N), BF16),
        scratch_shapes=[pltpu.VMEM((K, N), BF16)],
        compiler_params=_params(1),
        name=name,
    )(xn, w, q_norm_w, tabs["att_cos"], tabs["att_sin"], tabs["bd"])


def _sink_column(sink_ref, layer, g, rows, rows_per_head):
    head_in_group = lax.broadcasted_iota(jnp.int32, (rows, 1), 0) // rows_per_head
    col = jnp.full((rows, 1), sink_ref[layer, g * GROUP], F32)
    for r in range(1, GROUP):
        col = jnp.where(head_in_group == r, sink_ref[layer, g * GROUP + r], col)
    return col


def _attn_prompt_kernel(sink_ref, q_ref, lop_ref, loc_ref, hip_ref, hic_ref, vp_ref, vc_ref, o_ref, *,
                        layer, n_qblk):
    first = pl.program_id(1) == 0
    qb = q_ref[...]
    klo = jnp.concatenate([lop_ref[...], loc_ref[...]], axis=0)
    khi = jnp.concatenate([hip_ref[...], hic_ref[...]], axis=0)
    v_t = jnp.concatenate([vp_ref[...], vc_ref[...]], axis=0).T.astype(BF16)
    key = lax.broadcasted_iota(jnp.int32, (WINDOW, WINDOW), 0)
    qi = lax.broadcasted_iota(jnp.int32, (WINDOW, WINDOW), 1)
    own = key <= qi
    scores = {}
    for t in range(n_qblk):
        qrows = slice(WINDOW * t, WINDOW * (t + 1))
        krows = slice(WINDOW * t, WINDOW * (t + 2))
        for g in range(N_KV_HEADS):
            gl = slice(128 * g, 128 * (g + 1))
            kk = jnp.concatenate([klo[krows, gl], khi[krows, gl]], axis=0)
            xq = jnp.concatenate([qb[qrows, 256 * g:256 * g + 128], qb[qrows, 256 * g + 128:256 * (g + 1)]],
                                 axis=0)
            scores[t, g] = _dot_nt(kk, xq)
    for t in range(n_qblk):
        qrows = slice(WINDOW * t, WINDOW * (t + 1))
        krows = slice(WINDOW * t, WINDOW * (t + 2))
        pieces = []
        for g in range(N_KV_HEADS):
            s4 = scores[t, g]
            v_g = v_t[HEAD_DIM * g:HEAD_DIM * (g + 1), krows]
            for pair in range(GROUP // 2):
                p2s, invs = [], []
                for parity in range(2):
                    sk = sink_ref[layer, g * GROUP + 2 * pair + parity]
                    blk = s4[2 * WINDOW * parity:2 * WINDOW * (parity + 1), WINDOW * pair:WINDOW * (pair + 1)]
                    s_prev = blk[:WINDOW]
                    if t == 0:
                        s_prev = jnp.where(first, NEG_INF, s_prev)
                    s = jnp.where(own, blk[WINDOW:], s_prev)
                    m = jnp.maximum(jnp.max(s, axis=0, keepdims=True), sk)
                    p = jnp.exp(s - m)
                    invs.append(1.0 / (jnp.sum(p, axis=0, keepdims=True) + jnp.exp(sk - m)))
                    p2s.append(jnp.concatenate([jnp.where(own, 0.0, p), jnp.where(own, p, 0.0)], axis=0))
                p2 = jnp.concatenate(p2s, axis=1).astype(BF16)
                o_t = _dot(v_g, p2) * jnp.concatenate(invs, axis=1)
                pieces += [o_t[:, :WINDOW], o_t[:, WINDOW:]]
        o_ref[qrows, :] = jnp.concatenate(pieces, axis=0).T.astype(o_ref.dtype)


def _attn_prompt(q, klo, khi, v, sinks, layer, B, L, name):
    nb = L // WINDOW
    nq = ATTN_Q_BLOCKS if nb % ATTN_Q_BLOCKS == 0 else 1
    ns = nb // nq
    kw = N_KV_HEADS * HEAD_DIM
    kpad = 128 * N_KV_HEADS
    cur = lambda b, i: b * ns + i
    prev = lambda b, i: b * nb + jnp.maximum(i * nq - 1, 0)
    kern = functools.partial(_attn_prompt_kernel, layer=layer, n_qblk=nq)
    return pl.pallas_call(
        kern,
        grid=(B, ns),
        in_specs=[pl.BlockSpec(memory_space=pltpu.SMEM),
                  pl.BlockSpec((nq * WINDOW, D_MODEL), lambda b, i: (cur(b, i), 0)),
                  pl.BlockSpec((WINDOW, kpad), lambda b, i: (prev(b, i), 0)),
                  pl.BlockSpec((nq * WINDOW, kpad), lambda b, i: (cur(b, i), 0)),
                  pl.BlockSpec((WINDOW, kpad), lambda b, i: (prev(b, i), 0)),
                  pl.BlockSpec((nq * WINDOW, kpad), lambda b, i: (cur(b, i), 0)),
                  pl.BlockSpec((WINDOW, kw), lambda b, i: (prev(b, i), 0)),
                  pl.BlockSpec((nq * WINDOW, kw), lambda b, i: (cur(b, i), 0))],
        out_specs=pl.BlockSpec((nq * WINDOW, D_MODEL), lambda b, i: (cur(b, i), 0)),
        out_shape=jax.ShapeDtypeStruct((B * L, D_MODEL), BF16),
        compiler_params=_params(2),
        name=name,
    )(sinks, q, klo, klo, khi, khi, v, v)


def _attn_decode_kernel(sink_ref, q_ref, kn_ref, vn_ref, ck_ref, cv_ref, o_ref, *, layer, t_dec):
    nseq = DEC_ROWS // t_dec
    wc = ck_ref.shape[1]
    qb = q_ref[...]
    knew = kn_ref[...].astype(BF16)
    vnew = vn_ref[...].astype(BF16)
    rows = GROUP * DEC_ROWS
    row = lax.broadcasted_iota(jnp.int32, (rows, 1), 0)
    row_seq = (row % DEC_ROWS) // t_dec
    row_tok = row % t_dec
    jold = lax.broadcasted_iota(jnp.int32, (rows, wc), 1)
    vis_old = (jold > row_tok + (wc - WINDOW)) & (jold <= row_tok + wc)
    cnew = lax.broadcasted_iota(jnp.int32, (rows, DEC_ROWS), 1)
    vis_new = (cnew // t_dec == row_seq) & (cnew % t_dec <= row_tok)
    scores = []
    for g in range(N_KV_HEADS):
        heads = [g * GROUP + r for r in range(GROUP)]
        hs = slice(HEAD_DIM * g, HEAD_DIM * (g + 1))
        qs = jnp.concatenate([qb[:, HEAD_DIM * h:HEAD_DIM * (h + 1)] for h in heads], axis=0)
        s_old = jnp.zeros((rows, wc), F32)
        for bi in range(nseq):
            kc = ck_ref[bi][:, hs].astype(BF16)
            s_old = s_old + jnp.where(row_seq == bi, _dot_nt(qs, kc), 0.0)
        scores.append((s_old, _dot_nt(qs, knew[:, hs])))
    for g in range(N_KV_HEADS):
        heads = [g * GROUP + r for r in range(GROUP)]
        hs = slice(HEAD_DIM * g, HEAD_DIM * (g + 1))
        s_old = jnp.where(vis_old, scores[g][0], NEG_INF)
        s_new = jnp.where(vis_new, scores[g][1], NEG_INF)
        sk = _sink_column(sink_ref, layer, g, rows, DEC_ROWS)
        m = jnp.maximum(jnp.maximum(jnp.max(s_old, axis=-1, keepdims=True),
                                    jnp.max(s_new, axis=-1, keepdims=True)), sk)
        p_old = jnp.exp(s_old - m)
        p_new = jnp.exp(s_new - m)
        denom = (jnp.sum(p_old, axis=-1, keepdims=True) + jnp.sum(p_new, axis=-1, keepdims=True)
                 + jnp.exp(sk - m))
        inv = 1.0 / denom
        o = _dot((p_new * inv).astype(BF16), vnew[:, hs])
        pn_old = p_old * inv
        for bi in range(nseq):
            vc = cv_ref[bi][:, hs].astype(BF16)
            o = o + _dot(jnp.where(row_seq == bi, pn_old, 0.0).astype(BF16), vc)
        for r, h in enumerate(heads):
            o_ref[:, HEAD_DIM * h:HEAD_DIM * (h + 1)] = o[DEC_ROWS * r:DEC_ROWS * (r + 1)].astype(o_ref.dtype)


def _attn_decode(q, kn, v, cache_k, cache_v, sinks, layer, TP, DB, t_dec, name):
    nseq = DEC_ROWS // t_dec
    r0 = TP // DEC_ROWS
    kw = N_KV_HEADS * HEAD_DIM
    wc = cache_k.shape[1]
    kern = functools.partial(_attn_decode_kernel, layer=layer, t_dec=t_dec)
    return pl.pallas_call(
        kern,
        grid=(DB // nseq,),
        in_specs=[pl.BlockSpec(memory_space=pltpu.SMEM),
                  pl.BlockSpec((DEC_ROWS, D_MODEL), lambda i: (r0 + i, 0)),
                  pl.BlockSpec((DEC_ROWS, kw), lambda i: (r0 + i, 0)),
                  pl.BlockSpec((DEC_ROWS, kw), lambda i: (r0 + i, 0)),
                  pl.BlockSpec((nseq, wc, kw), lambda i: (i, 0, 0)),
                  pl.BlockSpec((nseq, wc, kw), lambda i: (i, 0, 0))],
        out_specs=pl.BlockSpec((DEC_ROWS, D_MODEL), lambda i: (i, 0)),
        out_shape=jax.ShapeDtypeStruct((DB * t_dec, D_MODEL), BF16),
        compiler_params=_params(1),
        name=name,
    )(sinks, q, kn, v, cache_k, cache_v)


def _rope_cos_sin(pos, half):
    inv = 1.0 / (ROPE_THETA ** (jnp.arange(half, dtype=F32) / half))
    ang = pos.astype(F32)[:, None] * inv[None, :]
    return jnp.cos(ang), jnp.sin(ang)


def _decay_tables(chunk, reps):
    lg = jnp.log(1.0 - 2.0 ** (-5.0 - jnp.arange(RET_HEADS, dtype=F32)))
    idx = jnp.arange(chunk, dtype=F32)
    rel = idx[:, None] - idx[None, :]
    dmask = jnp.where(rel >= 0, jnp.exp(lg[:, None, None] * jnp.maximum(rel, 0.0)), 0.0)
    qdec = jnp.exp(lg[:, None] * (idx + 1.0))[:, :, None]
    kdec = jnp.exp(lg[:, None] * (chunk - 1.0 - idx))[:, :, None]
    cdec = jnp.exp(lg * chunk)[:, None, None]
    if reps > 1:
        eye = jnp.eye(reps, dtype=F32)
        dmask = jnp.einsum("ab,hij->haibj", eye, dmask).reshape(RET_HEADS, reps * chunk, reps * chunk)
        qdec = jnp.tile(qdec, (1, reps, 1))
        kdec = jnp.tile(kdec, (1, reps, 1))
    return dmask, qdec, kdec, cdec


def _tables(B, L, TD, t_dec, chunk_p):
    pos = jnp.concatenate([jnp.arange(L), PAST_LEN + jnp.arange(TD) % t_dec])
    ret_cos, ret_sin = _rope_cos_sin(pos, RET_DK // 2)
    per_row = lambda t: jnp.concatenate([jnp.tile(t[:L], (B, 1)), t[L:]], axis=0)
    c, s = _rope_cos_sin(pos, HEAD_DIM // 2)
    att_cos = jnp.concatenate([c, c, c, c], axis=1)
    att_sin = jnp.concatenate([-s, s, -s, s], axis=1)
    dmask_p, qdec_p, kdec_p, cdec_p = _decay_tables(chunk_p, 1)
    dmask_d, qdec_d, kdec_d, cdec_d = _decay_tables(math.gcd(t_dec, RET_CHUNK), RET_DEC_ROWS // t_dec)
    head_of_lane = np.arange(256) // HEAD_DIM
    bd = jnp.asarray((head_of_lane[:, None] == head_of_lane[None, :]).astype(np.float32), dtype=BF16)
    return dict(ret_cos_rows=per_row(ret_cos), ret_sin_rows=per_row(ret_sin), att_cos=att_cos, att_sin=att_sin,
                chunk_p=chunk_p, dmask_p=dmask_p, qdec_p=qdec_p, kdec_p=kdec_p, cdec_p=cdec_p,
                dmask_d=dmask_d, qdec_d=qdec_d, kdec_d=kdec_d, cdec_d=cdec_d, bd=bd)


def kernel(x_prompt, x_sample, state_ret, cache_k, cache_v, ln_ret, w_ret_in, w_ret_out, ln_ffn, w_ffn_in,
           w_ffn_out, ln_kv, w_kv, k_norm, ln_attn, w_q, q_norm, sinks, w_o):
    B, L, D = x_prompt.shape
    DB, t_dec, _ = x_sample.shape
    n_ret = w_ret_in.shape[0]
    n_attn = w_q.shape[0]
    TP, TD = B * L, DB * t_dec
    T = TP + TD
    wc = cache_k.shape[1]
    kw = N_KV_HEADS * HEAD_DIM
    assert D == D_MODEL and L % RET_CHUNK == 0 and L % ROW_BLOCK == 0 and TD % ROW_BLOCK == 0
    assert t_dec == math.gcd(t_dec, RET_CHUNK) and DEC_ROWS % t_dec == 0 and wc == WINDOW and PAST_LEN >= WINDOW

    tabs = _tables(B, L, TD, t_dec, RET_CHUNK_PROMPT if L % RET_CHUNK_PROMPT == 0 else RET_CHUNK)
    ck = cache_k.reshape(DB, wc, kw)
    cv = cache_v.reshape(DB, wc, kw)
    q_norm_w = jnp.tile(q_norm, (1, N_HEADS))[:, None, :]
    k_norm_w = jnp.tile(k_norm[None, :], (1, N_KV_HEADS))
    ln_ret, ln_ffn, ln_attn = ln_ret[:, None, :], ln_ffn[:, None, :], ln_attn[:, None, :]

    ln_kv3 = ln_kv[None, None, :]
    prompt_states = []
    dec_states = None
    x = (x_prompt.reshape(TP, D), x_sample.reshape(TD, D))
    xn = _embed(*x, (ln_ret, 0), "embed")
    for l in range(n_ret):
        proj = _ret_in(xn, w_ret_in, l, tabs, f"ret_in_{l}")
        go_p, s_p, go_d, dec_states = _retention(proj, state_ret, l, dec_states, tabs, B, L, DB, t_dec,
                                                 f"retention_{l}")
        prompt_states.append(s_p)
        x, xn = _matmul_residual((go_p, go_d), w_ret_out, l, x, [(ln_ffn, l)], f"ret_out_{l}", TP)
        h = _swiglu(xn, w_ffn_in, l, f"ffn_in_{l}")
        if l + 1 < n_ret:
            x, xn = _matmul_residual(h, w_ffn_out, l, x, [(ln_ret, l + 1)], f"ffn_out_{l}", TP)
        else:
            x, xn_kv, xn = _matmul_residual(h, w_ffn_out, l, x, [(ln_kv3, 0), (ln_attn, 0)], f"ffn_out_{l}", TP)

    kn, v, klo, khi = _kv_proj(xn_kv, w_kv, k_norm_w, tabs, L, TP, "kv_proj")
    for j in range(n_attn):
        layer = n_ret + j
        q = _q_proj(xn, w_q, q_norm_w, j, tabs, L, TP, f"q_proj_{j}")
        ao_p = _attn_prompt(q, klo, khi, v, sinks, j, B, L, f"attn_prompt_{j}")
        ao_d = _attn_decode(q, kn, v, ck, cv, sinks, j, TP, DB, t_dec, f"attn_decode_{j}")
        x, xn = _matmul_residual((ao_p, ao_d), w_o, j, x, [(ln_ffn, layer)], f"attn_out_{j}", TP)
        h = _swiglu(xn, w_ffn_in, layer, f"ffn_in_{layer}")
        if j + 1 < n_attn:
            x, xn = _matmul_residual(h, w_ffn_out, layer, x, [(ln_attn, j + 1)], f"ffn_out_{layer}", TP)
        else:
            y_p, y_s = _matmul_residual(h, w_ffn_out, layer, x, [], f"ffn_out_{layer}", TP, split_out=True)

    y_prompt = y_p.reshape(B, L, D)
    y_sample = y_s.reshape(DB, t_dec, D)
    state_prompt = jnp.stack(prompt_states)
    w_keep = min(WINDOW, L)
    tail_rows = lambda t: t[:TP].reshape(B, L, kw)[:, L - w_keep:].reshape(B, w_keep, N_KV_HEADS, HEAD_DIM)
    kn_p, v_p = tail_rows(kn), tail_rows(v)
    kn_d = kn[TP:].reshape(DB, t_dec, N_KV_HEADS, HEAD_DIM)
    v_d = v[TP:].reshape(DB, t_dec, N_KV_HEADS, HEAD_DIM)
    cache_k_sample = jnp.concatenate([cache_k, kn_d], axis=1)[:, -wc:]
    cache_v_sample = jnp.concatenate([cache_v, v_d], axis=1)[:, -wc:]
    return (y_prompt, y_sample, state_prompt, dec_states, kn_p, v_p,
            cache_k_sample, cache_v_sample)
```

```python
import functools
import math

import numpy as np
import jax
import jax.numpy as jnp
from jax import lax
from jax.experimental import pallas as pl
from jax.experimental.pallas import tpu as pltpu

D_MODEL = 1024
PAST_LEN = 8192
RET_HEADS = 4
RET_DK = 256
RET_DV = 512
RET_CHUNK = 128
RET_CHUNK_PROMPT = 256
N_HEADS = 16
N_KV_HEADS = 4
HEAD_DIM = 64
GROUP = N_HEADS // N_KV_HEADS
WINDOW = 128
D_FF = 2816
ROPE_THETA = 10000.0
EPS = 1e-6
NEG_INF = -1e30

F32 = jnp.float32
BF16 = jnp.bfloat16

V7X_VMEM_LIMIT_BYTES = 56 * 1024 * 1024
DEC_ROWS = 16
RET_DEC_ROWS = 32
ROW_BLOCK = 512
SUB_ROWS = 256
RET_UNROLL = 4
ATTN_Q_BLOCKS = 4


def _dot(a, b):
    return jnp.dot(a, b, preferred_element_type=F32)


def _dot_nt(a, b):
    return lax.dot_general(a, b, (((1,), (1,)), ((), ())), preferred_element_type=F32)


def _dot_tn(a, b):
    return lax.dot_general(a, b, (((0,), (0,)), ((), ())), preferred_element_type=F32)


def _params(n_axes):
    return pltpu.CompilerParams(dimension_semantics=("arbitrary",) * n_axes,
                                vmem_limit_bytes=V7X_VMEM_LIMIT_BYTES)


def _row_tile(rows, prefs):
    for t in prefs:
        if rows % t == 0:
            return t
    raise ValueError(f"no row tile for {rows}")


def _rms_normed(x, gain_refs):
    ms = jnp.mean(x * x, axis=-1, keepdims=True)
    xh = x * lax.rsqrt(ms + EPS)
    return [(xh * g_ref[...]).astype(BF16) for g_ref in gain_refs]


def _embed_kernel(xp_ref, xs_ref, g_ref, xn_ref, *, n_prompt_blocks):
    i = pl.program_id(0)

    @pl.when(i < n_prompt_blocks)
    def _():
        xn_ref[...] = _rms_normed(xp_ref[...], [g_ref])[0]

    @pl.when(i >= n_prompt_blocks)
    def _():
        xn_ref[...] = _rms_normed(xs_ref[...], [g_ref])[0]


def _embed(xp, xs, gain, name):
    TP, D = xp.shape
    TD = xs.shape[0]
    npb = TP // ROW_BLOCK
    T = TP + TD
    g_arr, g_idx = gain
    blk = lambda fn: pl.BlockSpec((ROW_BLOCK, D), fn)
    return pl.pallas_call(
        functools.partial(_embed_kernel, n_prompt_blocks=npb),
        grid=(T // ROW_BLOCK,),
        in_specs=[blk(lambda i: (jnp.minimum(i, npb - 1), 0)),
                  blk(lambda i: (jnp.maximum(i - npb, 0), 0)),
                  pl.BlockSpec((None, 1, D), lambda i: (g_idx, 0, 0))],
        out_specs=blk(lambda i: (i, 0)),
        out_shape=jax.ShapeDtypeStruct((T, D), BF16),
        compiler_params=_params(1),
        name=name,
    )(xp, xs, g_arr)


def _sub_rows(tm):
    return max(r for r in range(16, 3 * SUB_ROWS + 1, 16) if tm % r == 0)


def _ret_in_kernel(x_ref, w_ref, cos_ref, sin_ref, o_ref, *, n_q_tiles):
    j = pl.program_id(1)
    sub = _sub_rows(x_ref.shape[0])
    half = RET_DK // 2

    @pl.when(j >= 2 * n_q_tiles)
    def _():
        o_ref[...] = _dot(x_ref[...], w_ref[...].astype(BF16)).astype(o_ref.dtype)

    @pl.when(j < 2 * n_q_tiles)
    def _():
        scale = jnp.where(j < n_q_tiles, 1.0, RET_DK ** -0.5)
        w = w_ref[...].astype(BF16)
        for r0 in range(0, x_ref.shape[0], sub):
            rows = slice(r0, r0 + sub)
            acc = _dot(x_ref[rows, :], w)
            cos = cos_ref[rows, :]
            sin = sin_ref[rows, :]
            for c0 in range(0, acc.shape[1], RET_DK):
                x1, x2 = acc[:, c0:c0 + half], acc[:, c0 + half:c0 + RET_DK]
                o_ref[rows, c0:c0 + half] = ((x1 * cos - x2 * sin) * scale).astype(o_ref.dtype)
                o_ref[rows, c0 + half:c0 + RET_DK] = ((x2 * cos + x1 * sin) * scale).astype(o_ref.dtype)


def _ret_in(xn, w, layer, tabs, name):
    T, K = xn.shape
    N = w.shape[-1]
    tn = 1024
    tm = _row_tile(T, (2816, 1536, 1024, 512))
    return pl.pallas_call(
        functools.partial(_ret_in_kernel, n_q_tiles=RET_HEADS * RET_DK // tn),
        grid=(T // tm, N // tn),
        in_specs=[pl.BlockSpec((tm, K), lambda i, j: (i, 0)),
                  pl.BlockSpec((None, K, tn), lambda i, j: (layer, 0, j)),
                  pl.BlockSpec((tm, RET_DK // 2), lambda i, j: (i, 0)),
                  pl.BlockSpec((tm, RET_DK // 2), lambda i, j: (i, 0))],
        out_specs=pl.BlockSpec((tm, tn), lambda i, j: (i, j)),
        out_shape=jax.ShapeDtypeStruct((T, N), BF16),
        compiler_params=_params(2),
        name=name,
    )(xn, w, tabs["ret_cos_rows"], tabs["ret_sin_rows"])


def _swiglu_kernel(x_ref, wa_ref, wb_ref, o_ref):
    wa = wa_ref[...].astype(BF16)
    wb = wb_ref[...].astype(BF16)
    sub = _sub_rows(x_ref.shape[0])
    for r0 in range(0, x_ref.shape[0], sub):
        rows = slice(r0, r0 + sub)
        a = _dot(x_ref[rows, :], wa)
        b = _dot(x_ref[rows, :], wb)
        o_ref[rows, :] = ((a * jax.nn.sigmoid(a)) * b).astype(o_ref.dtype)


def _swiglu(xn, w, layer, name):
    T, K = xn.shape
    tn = 256
    nb = D_FF // tn
    tm = _row_tile(T, (2816, 1536, 1024, 512))
    return pl.pallas_call(
        _swiglu_kernel,
        grid=(T // tm, nb),
        in_specs=[pl.BlockSpec((tm, K), lambda i, j: (i, 0)),
                  pl.BlockSpec((None, K, tn), lambda i, j: (layer, 0, j)),
                  pl.BlockSpec((None, K, tn), lambda i, j: (layer, 0, nb + j))],
        out_specs=pl.BlockSpec((tm, tn), lambda i, j: (i, j)),
        out_shape=jax.ShapeDtypeStruct((T, D_FF), BF16),
        compiler_params=_params(2),
        name=name,
    )(xn, w, w)


def _mm_res_kernel(*refs, n_a, n_r, n_gains, n_prompt_blocks, split_out):
    a_refs = refs[:n_a]
    w_ref = refs[n_a]
    r_refs = refs[n_a + 1:n_a + 1 + n_r]
    rest = refs[n_a + 1 + n_r:]
    gain_refs = rest[:n_gains]
    out_refs = rest[n_gains:-1]
    wb_ref = rest[-1]
    i = pl.program_id(0)

    @pl.when(i == 0)
    def _():
        wb_ref[...] = w_ref[...].astype(BF16)

    def step(a_ref, r_ref, dst):
        for r0 in range(0, a_ref.shape[0], SUB_ROWS):
            rows = slice(r0, r0 + SUB_ROWS)
            x = r_ref[rows, :] + _dot(a_ref[rows, :], wb_ref[...])
            out_refs[dst][rows, :] = x
            if not split_out:
                for o_ref, xn in zip(out_refs[1:], _rms_normed(x, gain_refs)):
                    o_ref[rows, :] = xn

    if n_a == 1 and n_r == 1 and not split_out:
        step(a_refs[0], r_refs[0], 0)
    else:
        @pl.when(i < n_prompt_blocks)
        def _():
            step(a_refs[0], r_refs[0], 0)

        @pl.when(i >= n_prompt_blocks)
        def _():
            step(a_refs[-1], r_refs[-1], 1 if split_out else 0)


def _matmul_residual(a, w, layer, res, gains, name, TP, split_out=False):
    a = a if isinstance(a, (tuple, list)) else (a,)
    res = res if isinstance(res, (tuple, list)) else (res,)
    T = sum(r.shape[0] for r in res)
    N = res[0].shape[1]
    K = a[0].shape[1]
    tm = ROW_BLOCK
    npb = TP // tm
    head = lambda i: (jnp.minimum(i, npb - 1), 0)
    tail = lambda i: (jnp.maximum(i - npb, 0), 0)
    whole = lambda i: (i, 0)
    rows_of = lambda arrs, width: [pl.BlockSpec((tm, width), fn)
                                   for fn in ((whole,) if len(arrs) == 1 else (head, tail))]
    in_specs = rows_of(a, K) + [pl.BlockSpec((None, K, N), lambda i: (layer, 0, 0))] + rows_of(res, N)
    in_specs += [pl.BlockSpec((None, 1, N), lambda i, idx=idx: (idx, 0, 0)) for _, idx in gains]
    if split_out:
        assert not gains
        out_specs = [pl.BlockSpec((tm, N), head), pl.BlockSpec((tm, N), tail)]
        out_shape = [jax.ShapeDtypeStruct((TP, N), F32), jax.ShapeDtypeStruct((T - TP, N), F32)]
    else:
        out_specs = [pl.BlockSpec((tm, N), whole)] * (1 + len(gains))
        out_shape = [jax.ShapeDtypeStruct((T, N), F32)] + [jax.ShapeDtypeStruct((T, N), BF16)] * len(gains)
    aliases = {len(a) + 1: 0} if len(res) == 1 and not split_out else {}
    return pl.pallas_call(
        functools.partial(_mm_res_kernel, n_a=len(a), n_r=len(res), n_gains=len(gains), n_prompt_blocks=npb,
                          split_out=split_out),
        grid=(T // tm,),
        in_specs=in_specs,
        out_specs=out_specs,
        out_shape=out_shape,
        scratch_shapes=[pltpu.VMEM((K, N), BF16)],
        input_output_aliases=aliases,
        compiler_params=_params(1),
        name=name,
    )(*a, w, *res, *[g for g, _ in gains])


def _groupnorm_gate(o, gate):
    mu = jnp.mean(o, axis=-1, keepdims=True)
    d = o - mu
    var = jnp.mean(d * d, axis=-1, keepdims=True)
    on = d * lax.rsqrt(var + EPS)
    return (gate * jax.nn.sigmoid(gate)) * on


def _retention_kernel(q_ref, k_ref, v_ref, g_ref, dmask_ref, qdec_ref, kdec_ref, cdec_ref,
                      qd_ref, kd_ref, vd_ref, gd_ref, dmaskd_ref, qdecd_ref, kdecd_ref, cdecd_ref, sd_ref,
                      *rest, chunk_rows, n_chunks, t_dec):
    go_ref, sfin_ref, god_ref, snew_ref, s_ref = rest[-5:]
    c = pl.program_id(2)

    qb = qd_ref[...]
    kb = kd_ref[...]
    vb = vd_ref[...]
    cdecd = cdecd_ref[...]
    inner = _dot((_dot_nt(qb, kb) * dmaskd_ref[...]).astype(BF16), vb)
    kdd = kb.astype(F32) * kdecd_ref[...]
    seq_of_row = lax.broadcasted_iota(jnp.int32, (RET_DEC_ROWS, 1), 0) // t_dec
    cross = jnp.zeros((RET_DEC_ROWS, RET_DV), F32)
    for bi in range(RET_DEC_ROWS // t_dec):
        mine = seq_of_row == bi
        s = sd_ref[bi]
        cross = cross + jnp.where(mine, _dot(qb, s.astype(BF16)), 0.0)
        snew_ref[bi] = s * cdecd + _dot_tn(jnp.where(mine, kdd, 0.0).astype(BF16), vb)
    cross = cross * qdecd_ref[...]
    god_ref[...] = _groupnorm_gate(inner + cross, gd_ref[...].astype(F32)).astype(god_ref.dtype)

    @pl.when(c == 0)
    def _():
        s_ref[...] = jnp.zeros_like(s_ref)

    dmask = dmask_ref[...]
    qdec = qdec_ref[...]
    kdec = kdec_ref[...]
    cdec = cdec_ref[...]

    def chunk(ci, carry):
        rows = pl.ds(pl.multiple_of(ci * chunk_rows, chunk_rows), chunk_rows)
        qb = q_ref[rows, :]
        kb = k_ref[rows, :]
        vb = v_ref[rows, :]
        s = s_ref[...]
        scores = _dot_nt(qb, kb) * dmask
        inner = _dot(scores.astype(BF16), vb)
        cross = _dot(qb, s.astype(BF16)) * qdec
        s_ref[...] = s * cdec + _dot_tn((kb.astype(F32) * kdec).astype(BF16), vb)
        go_ref[rows, :] = _groupnorm_gate(inner + cross, g_ref[rows, :].astype(F32)).astype(go_ref.dtype)
        return carry

    lax.fori_loop(0, n_chunks, chunk, 0, unroll=min(n_chunks, RET_UNROLL))

    @pl.when(c == pl.num_programs(2) - 1)
    def _():
        sfin_ref[...] = s_ref[...]


def _retention(proj, state, layer, snew_all, tabs, B, L, DB, t_dec, name):
    TP, TD = B * L, DB * t_dec
    nseq = RET_DEC_ROWS // t_dec
    steps = DB // nseq
    assert DB % nseq == 0 and steps % B == 0, (DB, nseq, B)
    nblk = steps // B
    rb = L // nblk
    ck = tabs["chunk_p"]
    assert L % nblk == 0 and rb % ck == 0, (L, nblk, ck)
    r0 = TP // RET_DEC_ROWS
    kcol = vcol = RET_HEADS
    gcol = 2 * RET_HEADS
    prow = lambda b, h, c: b * nblk + c
    drow = lambda b, h, c: b * nblk + c
    dec = lambda width, col0: pl.BlockSpec((RET_DEC_ROWS, width), lambda b, h, c: (r0 + drow(b, h, c), col0 + h))
    per_head = lambda *shape: pl.BlockSpec((None,) + shape, lambda b, h, c: (h,) + (0,) * len(shape))
    s_spec = pl.BlockSpec((None, nseq, None, RET_DK, RET_DV), lambda b, h, c: (layer, drow(b, h, c), h, 0, 0))
    aliased = [] if snew_all is None else [snew_all]
    kern = functools.partial(_retention_kernel, chunk_rows=ck, n_chunks=rb // ck, t_dec=t_dec)
    return pl.pallas_call(
        kern,
        grid=(B, RET_HEADS, nblk),
        in_specs=[pl.BlockSpec((rb, RET_DK), lambda b, h, c: (prow(b, h, c), h)),
                  pl.BlockSpec((rb, RET_DK), lambda b, h, c: (prow(b, h, c), kcol + h)),
                  pl.BlockSpec((rb, RET_DV), lambda b, h, c: (prow(b, h, c), vcol + h)),
                  pl.BlockSpec((rb, RET_DV), lambda b, h, c: (prow(b, h, c), gcol + h)),
                  per_head(ck, ck), per_head(ck, 1), per_head(ck, 1), per_head(1, 1),
                  dec(RET_DK, 0), dec(RET_DK, kcol), dec(RET_DV, vcol), dec(RET_DV, gcol),
                  per_head(RET_DEC_ROWS, RET_DEC_ROWS), per_head(RET_DEC_ROWS, 1), per_head(RET_DEC_ROWS, 1),
                  per_head(1, 1), s_spec] + [pl.BlockSpec(memory_space=pl.ANY)] * len(aliased),
        out_specs=[pl.BlockSpec((rb, RET_DV), lambda b, h, c: (prow(b, h, c), h)),
                   pl.BlockSpec((None, None, RET_DK, RET_DV), lambda b, h, c: (b, h, 0, 0)),
                   pl.BlockSpec((RET_DEC_ROWS, RET_DV), lambda b, h, c: (drow(b, h, c), h)),
                   s_spec],
        out_shape=[jax.ShapeDtypeStruct((TP, RET_HEADS * RET_DV), BF16),
                   jax.ShapeDtypeStruct((B, RET_HEADS, RET_DK, RET_DV), F32),
                   jax.ShapeDtypeStruct((TD, RET_HEADS * RET_DV), BF16),
                   jax.ShapeDtypeStruct(state.shape, state.dtype)],
        scratch_shapes=[pltpu.VMEM((RET_DK, RET_DV), F32)],
        input_output_aliases={17: 3} if aliased else {},
        compiler_params=_params(3),
        name=name,
    )(proj, proj, proj, proj, tabs["dmask_p"], tabs["qdec_p"], tabs["kdec_p"], tabs["cdec_p"],
      proj, proj, proj, proj, tabs["dmask_d"], tabs["qdec_d"], tabs["kdec_d"], tabs["cdec_d"], state, *aliased)


def _headnorm_rope(x, w, cos2, sin2, bd):
    width = x.shape[1]
    sq = (x * x).astype(BF16)
    parts = [_dot(sq[:, 256 * c:256 * (c + 1)], bd) for c in range(width // 256)]
    ss = parts[0] if len(parts) == 1 else jnp.concatenate(parts, axis=1)
    y = (x * lax.rsqrt(ss * (1.0 / HEAD_DIM) + EPS)) * w
    lane = lax.broadcasted_iota(jnp.int32, y.shape, 1)
    first_half = (lane & (HEAD_DIM - 1)) < HEAD_DIM // 2
    rot = jnp.where(first_half, pltpu.roll(y, width - HEAD_DIM // 2, 1), pltpu.roll(y, HEAD_DIM // 2, 1))
    nrep = width // 128
    cosw = jnp.concatenate([cos2] * nrep, axis=1)
    sinw = jnp.concatenate([sin2] * nrep, axis=1)
    return y * cosw + rot * sinw


def _kv_proj_kernel(x_ref, wkv_ref, w_ref, cos_ref, sin_ref, bd_ref, o_ref, v_ref, lo_ref, hi_ref, wb_ref):
    @pl.when(pl.program_id(0) == 0)
    def _():
        wb_ref[...] = wkv_ref[...].astype(BF16)

    kw = N_KV_HEADS * HEAD_DIM
    sub = SUB_ROWS
    low = lax.broadcasted_iota(jnp.int32, (sub, 128), 1) < HEAD_DIM
    zero = jnp.zeros((sub, 128), F32)
    for r0 in range(0, x_ref.shape[0], sub):
        rows = slice(r0, r0 + sub)
        kv = _dot(x_ref[rows, :], wb_ref[...])
        v_ref[rows, :] = kv[:, kw:]
        kn = _headnorm_rope(kv[:, :kw], w_ref[...], cos_ref[rows, :], sin_ref[rows, :], bd_ref[...])
        o_ref[rows, :] = kn
        for c in range(N_KV_HEADS // 2):
            tile = kn[:, 128 * c:128 * (c + 1)]
            swapped = pltpu.roll(tile, HEAD_DIM, 1)
            even, odd = slice(256 * c, 256 * c + 128), slice(256 * c + 128, 256 * (c + 1))
            lo_ref[rows, even] = jnp.where(low, tile, zero).astype(lo_ref.dtype)
            hi_ref[rows, even] = jnp.where(low, zero, swapped).astype(hi_ref.dtype)
            lo_ref[rows, odd] = jnp.where(low, swapped, zero).astype(lo_ref.dtype)
            hi_ref[rows, odd] = jnp.where(low, zero, tile).astype(hi_ref.dtype)


def _table_block(i, n_prompt_blocks, blocks_per_seq):
    return jnp.where(i < n_prompt_blocks, i % blocks_per_seq, blocks_per_seq + i - n_prompt_blocks)


def _kv_proj(xn, w_kv, k_norm_w, tabs, L, TP, name):
    T, K = xn.shape
    kw = N_KV_HEADS * HEAD_DIM
    npb, bps = TP // ROW_BLOCK, L // ROW_BLOCK
    tab = lambda i: (_table_block(i, npb, bps), 0)
    kv_spec = pl.BlockSpec((ROW_BLOCK, kw), lambda i: (i, 0))
    pad_spec = pl.BlockSpec((ROW_BLOCK, 128 * N_KV_HEADS), lambda i: (i, 0))
    pad_shape = jax.ShapeDtypeStruct((T, 128 * N_KV_HEADS), BF16)
    return pl.pallas_call(
        _kv_proj_kernel,
        grid=(T // ROW_BLOCK,),
        in_specs=[pl.BlockSpec((ROW_BLOCK, K), lambda i: (i, 0)),
                  pl.BlockSpec((K, 2 * kw), lambda i: (0, 0)),
                  pl.BlockSpec((1, kw), lambda i: (0, 0)),
                  pl.BlockSpec((ROW_BLOCK, 128), tab),
                  pl.BlockSpec((ROW_BLOCK, 128), tab),
                  pl.BlockSpec((256, 256), lambda i: (0, 0))],
        out_specs=[kv_spec, kv_spec, pad_spec, pad_spec],
        out_shape=[jax.ShapeDtypeStruct((T, kw), F32), jax.ShapeDtypeStruct((T, kw), F32), pad_shape, pad_shape],
        scratch_shapes=[pltpu.VMEM((K, 2 * kw), BF16)],
        compiler_params=_params(1),
        name=name,
    )(xn, w_kv, k_norm_w, tabs["att_cos"], tabs["att_sin"], tabs["bd"])


def _q_proj_kernel(x_ref, w_ref, qw_ref, cos_ref, sin_ref, bd_ref, o_ref, wb_ref):
    @pl.when(pl.program_id(0) == 0)
    def _():
        wb_ref[...] = w_ref[...].astype(BF16)

    bd = bd_ref[...]
    for r0 in range(0, x_ref.shape[0], 128):
        rows = slice(r0, r0 + 128)
        q = _dot(x_ref[rows, :], wb_ref[...])
        cos2, sin2 = cos_ref[rows, :], sin_ref[rows, :]
        for c0 in range(0, q.shape[1], 256):
            cols = slice(c0, c0 + 256)
            qc = _headnorm_rope(q[:, cols], qw_ref[:, cols], cos2, sin2, bd)
            o_ref[rows, cols] = (qc * (HEAD_DIM ** -0.5)).astype(o_ref.dtype)


def _q_proj(xn, w, q_norm_w, layer, tabs, L, TP, name):
    T, K = xn.shape
    N = w.shape[-1]
    npb, bps = TP // ROW_BLOCK, L // ROW_BLOCK
    tab = lambda i: (_table_block(i, npb, bps), 0)
    return pl.pallas_call(
        _q_proj_kernel,
        grid=(T // ROW_BLOCK,),
        in_specs=[pl.BlockSpec((ROW_BLOCK, K), lambda i: (i, 0)),
                  pl.BlockSpec((None, K, N), lambda i: (layer, 0, 0)),
                  pl.BlockSpec((None, 1, N), lambda i: (layer, 0, 0)),
                  pl.BlockSpec((ROW_BLOCK, 128), tab),
                  pl.BlockSpec((ROW_BLOCK, 128), tab),
                  pl.BlockSpec((256, 256), lambda i: (0, 0))],
        out_specs=pl.BlockSpec((ROW_BLOCK, N), lambda i: (i, 0)),
        out_shape=jax.ShapeDtypeStruct((T, N), BF16),
        scratch_shapes=[pltpu.VMEM((K, N), BF16)],
        compiler_params=_params(1),
        name=name,
    )(xn, w, q_norm_w, tabs["att_cos"], tabs["att_sin"], tabs["bd"])


def _sink_column(sink_ref, layer, g, rows, rows_per_head):
    head_in_group = lax.broadcasted_iota(jnp.int32, (rows, 1), 0) // rows_per_head
    col = jnp.full((rows, 1), sink_ref[layer, g * GROUP], F32)
    for r in range(1, GROUP):
        col = jnp.where(head_in_group == r, sink_ref[layer, g * GROUP + r], col)
    return col


def _attn_prompt_kernel(sink_ref, q_ref, lop_ref, loc_ref, hip_ref, hic_ref, vp_ref, vc_ref, o_ref, *,
                        layer, n_qblk):
    first = pl.program_id(1) == 0
    qb = q_ref[...]
    klo = jnp.concatenate([lop_ref[...], loc_ref[...]], axis=0)
    khi = jnp.concatenate([hip_ref[...], hic_ref[...]], axis=0)
    v_t = jnp.concatenate([vp_ref[...], vc_ref[...]], axis=0).T.astype(BF16)
    key = lax.broadcasted_iota(jnp.int32, (WINDOW, WINDOW), 0)
    qi = lax.broadcasted_iota(jnp.int32, (WINDOW, WINDOW), 1)
    own = key <= qi
    scores = {}
    for t in range(n_qblk):
        qrows = slice(WINDOW * t, WINDOW * (t + 1))
        krows = slice(WINDOW * t, WINDOW * (t + 2))
        for g in range(N_KV_HEADS):
            gl = slice(128 * g, 128 * (g + 1))
            kk = jnp.concatenate([klo[krows, gl], khi[krows, gl]], axis=0)
            xq = jnp.concatenate([qb[qrows, 256 * g:256 * g + 128], qb[qrows, 256 * g + 128:256 * (g + 1)]],
                                 axis=0)
            scores[t, g] = _dot_nt(kk, xq)
    for t in range(n_qblk):
        qrows = slice(WINDOW * t, WINDOW * (t + 1))
        krows = slice(WINDOW * t, WINDOW * (t + 2))
        pieces = []
        for g in range(N_KV_HEADS):
            s4 = scores[t, g]
            v_g = v_t[HEAD_DIM * g:HEAD_DIM * (g + 1), krows]
            for pair in range(GROUP // 2):
                p2s, invs = [], []
                for parity in range(2):
                    sk = sink_ref[layer, g * GROUP + 2 * pair + parity]
                    blk = s4[2 * WINDOW * parity:2 * WINDOW * (parity + 1), WINDOW * pair:WINDOW * (pair + 1)]
                    s_prev = blk[:WINDOW]
                    if t == 0:
                        s_prev = jnp.where(first, NEG_INF, s_prev)
                    s = jnp.where(own, blk[WINDOW:], s_prev)
                    m = jnp.maximum(jnp.max(s, axis=0, keepdims=True), sk)
                    p = jnp.exp(s - m)
                    invs.append(1.0 / (jnp.sum(p, axis=0, keepdims=True) + jnp.exp(sk - m)))
                    p2s.append(jnp.concatenate([jnp.where(own, 0.0, p), jnp.where(own, p, 0.0)], axis=0))
                p2 = jnp.concatenate(p2s, axis=1).astype(BF16)
                o_t = _dot(v_g, p2) * jnp.concatenate(invs, axis=1)
                pieces += [o_t[:, :WINDOW], o_t[:, WINDOW:]]
        o_ref[qrows, :] = jnp.concatenate(pieces, axis=0).T.astype(o_ref.dtype)


def _attn_prompt(q, klo, khi, v, sinks, layer, B, L, name):
    nb = L // WINDOW
    nq = ATTN_Q_BLOCKS if nb % ATTN_Q_BLOCKS == 0 else 1
    ns = nb // nq
    kw = N_KV_HEADS * HEAD_DIM
    kpad = 128 * N_KV_HEADS
    cur = lambda b, i: b * ns + i
    prev = lambda b, i: b * nb + jnp.maximum(i * nq - 1, 0)
    kern = functools.partial(_attn_prompt_kernel, layer=layer, n_qblk=nq)
    return pl.pallas_call(
        kern,
        grid=(B, ns),
        in_specs=[pl.BlockSpec(memory_space=pltpu.SMEM),
                  pl.BlockSpec((nq * WINDOW, D_MODEL), lambda b, i: (cur(b, i), 0)),
                  pl.BlockSpec((WINDOW, kpad), lambda b, i: (prev(b, i), 0)),
                  pl.BlockSpec((nq * WINDOW, kpad), lambda b, i: (cur(b, i), 0)),
                  pl.BlockSpec((WINDOW, kpad), lambda b, i: (prev(b, i), 0)),
                  pl.BlockSpec((nq * WINDOW, kpad), lambda b, i: (cur(b, i), 0)),
                  pl.BlockSpec((WINDOW, kw), lambda b, i: (prev(b, i), 0)),
                  pl.BlockSpec((nq * WINDOW, kw), lambda b, i: (cur(b, i), 0))],
        out_specs=pl.BlockSpec((nq * WINDOW, D_MODEL), lambda b, i: (cur(b, i), 0)),
        out_shape=jax.ShapeDtypeStruct((B * L, D_MODEL), BF16),
        compiler_params=_params(2),
        name=name,
    )(sinks, q, klo, klo, khi, khi, v, v)


def _attn_decode_kernel(sink_ref, q_ref, kn_ref, vn_ref, ck_ref, cv_ref, o_ref, *, layer, t_dec):
    nseq = DEC_ROWS // t_dec
    wc = ck_ref.shape[1]
    qb = q_ref[...]
    knew = kn_ref[...].astype(BF16)
    vnew = vn_ref[...].astype(BF16)
    rows = GROUP * DEC_ROWS
    row = lax.broadcasted_iota(jnp.int32, (rows, 1), 0)
    row_seq = (row % DEC_ROWS) // t_dec
    row_tok = row % t_dec
    jold = lax.broadcasted_iota(jnp.int32, (rows, wc), 1)
    vis_old = (jold > row_tok + (wc - WINDOW)) & (jold <= row_tok + wc)
    cnew = lax.broadcasted_iota(jnp.int32, (rows, DEC_ROWS), 1)
    vis_new = (cnew // t_dec == row_seq) & (cnew % t_dec <= row_tok)
    scores = []
    for g in range(N_KV_HEADS):
        heads = [g * GROUP + r for r in range(GROUP)]
        hs = slice(HEAD_DIM * g, HEAD_DIM * (g + 1))
        qs = jnp.concatenate([qb[:, HEAD_DIM * h:HEAD_DIM * (h + 1)] for h in heads], axis=0)
        s_old = jnp.zeros((rows, wc), F32)
        for bi in range(nseq):
            kc = ck_ref[bi][:, hs].astype(BF16)
            s_old = s_old + jnp.where(row_seq == bi, _dot_nt(qs, kc), 0.0)
        scores.append((s_old, _dot_nt(qs, knew[:, hs])))
    for g in range(N_KV_HEADS):
        heads = [g * GROUP + r for r in range(GROUP)]
        hs = slice(HEAD_DIM * g, HEAD_DIM * (g + 1))
        s_old = jnp.where(vis_old, scores[g][0], NEG_INF)
        s_new = jnp.where(vis_new, scores[g][1], NEG_INF)
        sk = _sink_column(sink_ref, layer, g, rows, DEC_ROWS)
        m = jnp.maximum(jnp.maximum(jnp.max(s_old, axis=-1, keepdims=True),
                                    jnp.max(s_new, axis=-1, keepdims=True)), sk)
        p_old = jnp.exp(s_old - m)
        p_new = jnp.exp(s_new - m)
        denom = (jnp.sum(p_old, axis=-1, keepdims=True) + jnp.sum(p_new, axis=-1, keepdims=True)
                 + jnp.exp(sk - m))
        inv = 1.0 / denom
        o = _dot((p_new * inv).astype(BF16), vnew[:, hs])
        pn_old = p_old * inv
        for bi in range(nseq):
            vc = cv_ref[bi][:, hs].astype(BF16)
            o = o + _dot(jnp.where(row_seq == bi, pn_old, 0.0).astype(BF16), vc)
        for r, h in enumerate(heads):
            o_ref[:, HEAD_DIM * h:HEAD_DIM * (h + 1)] = o[DEC_ROWS * r:DEC_ROWS * (r + 1)].astype(o_ref.dtype)


def _attn_decode(q, kn, v, cache_k, cache_v, sinks, layer, TP, DB, t_dec, name):
    nseq = DEC_ROWS // t_dec
    r0 = TP // DEC_ROWS
    kw = N_KV_HEADS * HEAD_DIM
    wc = cache_k.shape[1]
    kern = functools.partial(_attn_decode_kernel, layer=layer, t_dec=t_dec)
    return pl.pallas_call(
        kern,
        grid=(DB // nseq,),
        in_specs=[pl.BlockSpec(memory_space=pltpu.SMEM),
                  pl.BlockSpec((DEC_ROWS, D_MODEL), lambda i: (r0 + i, 0)),
                  pl.BlockSpec((DEC_ROWS, kw), lambda i: (r0 + i, 0)),
                  pl.BlockSpec((DEC_ROWS, kw), lambda i: (r0 + i, 0)),
                  pl.BlockSpec((nseq, wc, kw), lambda i: (i, 0, 0)),
                  pl.BlockSpec((nseq, wc, kw), lambda i: (i, 0, 0))],
        out_specs=pl.BlockSpec((DEC_ROWS, D_MODEL), lambda i: (i, 0)),
        out_shape=jax.ShapeDtypeStruct((DB * t_dec, D_MODEL), BF16),
        compiler_params=_params(1),
        name=name,
    )(sinks, q, kn, v, cache_k, cache_v)


def _rope_cos_sin(pos, half):
    inv = 1.0 / (ROPE_THETA ** (np.arange(half, dtype=np.float64) / half))
    ang = pos.astype(np.float64)[:, None] * inv[None, :]
    return np.cos(ang), np.sin(ang)


def _decay_tables(chunk, reps):
    lg = np.log(1.0 - 2.0 ** (-5.0 - np.arange(RET_HEADS, dtype=np.float64)))
    idx = np.arange(chunk, dtype=np.float64)
    rel = idx[:, None] - idx[None, :]
    dmask = np.where(rel >= 0, np.exp(lg[:, None, None] * np.maximum(rel, 0.0)), 0.0)
    qdec = np.exp(lg[:, None] * (idx + 1.0))[:, :, None]
    kdec = np.exp(lg[:, None] * (chunk - 1.0 - idx))[:, :, None]
    cdec = np.exp(lg * chunk)[:, None, None]
    if reps > 1:
        dmask = np.einsum("ab,hij->haibj", np.eye(reps), dmask).reshape(RET_HEADS, reps * chunk, reps * chunk)
        qdec = np.tile(qdec, (1, reps, 1))
        kdec = np.tile(kdec, (1, reps, 1))
    return dmask, qdec, kdec, cdec


def _tables(B, L, TD, t_dec, chunk_p):
    pos = np.concatenate([np.arange(L), PAST_LEN + np.arange(TD) % t_dec])
    ret_cos, ret_sin = _rope_cos_sin(pos, RET_DK // 2)
    per_row = lambda t: np.concatenate([np.tile(t[:L], (B, 1)), t[L:]], axis=0)
    c, s = _rope_cos_sin(pos, HEAD_DIM // 2)
    att_cos = np.concatenate([c, c, c, c], axis=1)
    att_sin = np.concatenate([-s, s, -s, s], axis=1)
    dmask_p, qdec_p, kdec_p, cdec_p = _decay_tables(chunk_p, 1)
    dmask_d, qdec_d, kdec_d, cdec_d = _decay_tables(math.gcd(t_dec, RET_CHUNK), RET_DEC_ROWS // t_dec)
    head_of_lane = np.arange(256) // HEAD_DIM
    f32 = dict(ret_cos_rows=per_row(ret_cos), ret_sin_rows=per_row(ret_sin), att_cos=att_cos, att_sin=att_sin,
               dmask_p=dmask_p, qdec_p=qdec_p, kdec_p=kdec_p, cdec_p=cdec_p,
               dmask_d=dmask_d, qdec_d=qdec_d, kdec_d=kdec_d, cdec_d=cdec_d)
    tabs = {name: jnp.asarray(t.astype(np.float32)) for name, t in f32.items()}
    tabs["bd"] = jnp.asarray((head_of_lane[:, None] == head_of_lane[None, :]).astype(np.float32), dtype=BF16)
    tabs["chunk_p"] = chunk_p
    return tabs


def kernel(x_prompt, x_sample, state_ret, cache_k, cache_v, ln_ret, w_ret_in, w_ret_out, ln_ffn, w_ffn_in,
           w_ffn_out, ln_kv, w_kv, k_norm, ln_attn, w_q, q_norm, sinks, w_o):
    B, L, D = x_prompt.shape
    DB, t_dec, _ = x_sample.shape
    n_ret = w_ret_in.shape[0]
    n_attn = w_q.shape[0]
    TP, TD = B * L, DB * t_dec
    T = TP + TD
    wc = cache_k.shape[1]
    kw = N_KV_HEADS * HEAD_DIM
    assert D == D_MODEL and L % RET_CHUNK == 0 and L % ROW_BLOCK == 0 and TD % ROW_BLOCK == 0
    assert t_dec == math.gcd(t_dec, RET_CHUNK) and DEC_ROWS % t_dec == 0 and wc == WINDOW and PAST_LEN >= WINDOW

    tabs = _tables(B, L, TD, t_dec, RET_CHUNK_PROMPT if L % RET_CHUNK_PROMPT == 0 else RET_CHUNK)
    ck = cache_k.reshape(DB, wc, kw)
    cv = cache_v.reshape(DB, wc, kw)
    q_norm_w = jnp.tile(q_norm, (1, N_HEADS))[:, None, :]
    k_norm_w = jnp.tile(k_norm[None, :], (1, N_KV_HEADS))
    ln_ret, ln_ffn, ln_attn = ln_ret[:, None, :], ln_ffn[:, None, :], ln_attn[:, None, :]

    ln_kv3 = ln_kv[None, None, :]
    prompt_states = []
    dec_states = None
    x = (x_prompt.reshape(TP, D), x_sample.reshape(TD, D))
    xn = _embed(*x, (ln_ret, 0), "embed")
    for l in range(n_ret):
        proj = _ret_in(xn, w_ret_in, l, tabs, f"ret_in_{l}")
        go_p, s_p, go_d, dec_states = _retention(proj, state_ret, l, dec_states, tabs, B, L, DB, t_dec,
                                                 f"retention_{l}")
        prompt_states.append(s_p)
        x, xn = _matmul_residual((go_p, go_d), w_ret_out, l, x, [(ln_ffn, l)], f"ret_out_{l}", TP)
        h = _swiglu(xn, w_ffn_in, l, f"ffn_in_{l}")
        if l + 1 < n_ret:
            x, xn = _matmul_residual(h, w_ffn_out, l, x, [(ln_ret, l + 1)], f"ffn_out_{l}", TP)
        else:
            x, xn_kv, xn = _matmul_residual(h, w_ffn_out, l, x, [(ln_kv3, 0), (ln_attn, 0)], f"ffn_out_{l}", TP)

    kn, v, klo, khi = _kv_proj(xn_kv, w_kv, k_norm_w, tabs, L, TP, "kv_proj")
    for j in range(n_attn):
        layer = n_ret + j
        q = _q_proj(xn, w_q, q_norm_w, j, tabs, L, TP, f"q_proj_{j}")
        ao_p = _attn_prompt(q, klo, khi, v, sinks, j, B, L, f"attn_prompt_{j}")
        ao_d = _attn_decode(q, kn, v, ck, cv, sinks, j, TP, DB, t_dec, f"attn_decode_{j}")
        x, xn = _matmul_residual((ao_p, ao_d), w_o, j, x, [(ln_ffn, layer)], f"attn_out_{j}", TP)
        h = _swiglu(xn, w_ffn_in, layer, f"ffn_in_{layer}")
        if j + 1 < n_attn:
            x, xn = _matmul_residual(h, w_ffn_out, layer, x, [(ln_attn, j + 1)], f"ffn_out_{layer}", TP)
        else:
            y_p, y_s = _matmul_residual(h, w_ffn_out, layer, x, [], f"ffn_out_{layer}", TP, split_out=True)

    y_prompt = y_p.reshape(B, L, D)
    y_sample = y_s.reshape(DB, t_dec, D)
    state_prompt = jnp.stack(prompt_states)
    w_keep = min(WINDOW, L)
    tail_rows = lambda t: t[:TP].reshape(B, L, kw)[:, L - w_keep:].reshape(B, w_keep, N_KV_HEADS, HEAD_DIM)
    kn_p, v_p = tail_rows(kn), tail_rows(v)
    kn_d = kn[TP:].reshape(DB, t_dec, N_KV_HEADS, HEAD_DIM)
    v_d = v[TP:].reshape(DB, t_dec, N_KV_HEADS, HEAD_DIM)
    cache_k_sample = jnp.concatenate([cache_k, kn_d], axis=1)[:, -wc:]
    cache_v_sample = jnp.concatenate([cache_v, v_d], axis=1)[:, -wc:]
    return (y_prompt, y_sample, state_prompt, dec_states, kn_p, v_p,
            cache_k_sample, cache_v_sample)
```

```python
import functools
import math

import numpy as np
import jax
import jax.numpy as jnp
from jax import lax
from jax.experimental import pallas as pl
from jax.experimental.pallas import tpu as pltpu

D_MODEL = 1024
PAST_LEN = 8192
RET_HEADS = 4
RET_DK = 256
RET_DV = 512
RET_CHUNK = 128
RET_CHUNK_PROMPT = 256
N_HEADS = 16
N_KV_HEADS = 4
HEAD_DIM = 64
GROUP = N_HEADS // N_KV_HEADS
WINDOW = 128
D_FF = 2816
ROPE_THETA = 10000.0
EPS = 1e-6
NEG_INF = -1e30

F32 = jnp.float32
BF16 = jnp.bfloat16

V7X_VMEM_LIMIT_BYTES = 56 * 1024 * 1024
DEC_ROWS = 16
RET_DEC_ROWS = 32
ROW_BLOCK = 512
SUB_ROWS = 256
RET_UNROLL = 4
ATTN_Q_BLOCKS = 8


def _dot(a, b):
    return jnp.dot(a, b, preferred_element_type=F32)


def _dot_nt(a, b):
    return lax.dot_general(a, b, (((1,), (1,)), ((), ())), preferred_element_type=F32)


def _dot_tn(a, b):
    return lax.dot_general(a, b, (((0,), (0,)), ((), ())), preferred_element_type=F32)


def _params(n_axes):
    return pltpu.CompilerParams(dimension_semantics=("arbitrary",) * n_axes,
                                vmem_limit_bytes=V7X_VMEM_LIMIT_BYTES)


def _row_tile(rows, prefs):
    for t in prefs:
        if rows % t == 0:
            return t
    raise ValueError(f"no row tile for {rows}")


def _rms_normed(x, gain_refs):
    ms = jnp.mean(x * x, axis=-1, keepdims=True)
    xh = x * lax.rsqrt(ms + EPS)
    return [(xh * g_ref[...]).astype(BF16) for g_ref in gain_refs]


def _embed_kernel(xp_ref, xs_ref, g_ref, xn_ref, *, n_prompt_blocks):
    i = pl.program_id(0)

    @pl.when(i < n_prompt_blocks)
    def _():
        xn_ref[...] = _rms_normed(xp_ref[...], [g_ref])[0]

    @pl.when(i >= n_prompt_blocks)
    def _():
        xn_ref[...] = _rms_normed(xs_ref[...], [g_ref])[0]


def _embed(xp, xs, gain, name):
    TP, D = xp.shape
    TD = xs.shape[0]
    npb = TP // ROW_BLOCK
    T = TP + TD
    g_arr, g_idx = gain
    blk = lambda fn: pl.BlockSpec((ROW_BLOCK, D), fn)
    return pl.pallas_call(
        functools.partial(_embed_kernel, n_prompt_blocks=npb),
        grid=(T // ROW_BLOCK,),
        in_specs=[blk(lambda i: (jnp.minimum(i, npb - 1), 0)),
                  blk(lambda i: (jnp.maximum(i - npb, 0), 0)),
                  pl.BlockSpec((None, 1, D), lambda i: (g_idx, 0, 0))],
        out_specs=blk(lambda i: (i, 0)),
        out_shape=jax.ShapeDtypeStruct((T, D), BF16),
        compiler_params=_params(1),
        name=name,
    )(xp, xs, g_arr)


def _sub_rows(tm):
    return max(r for r in range(16, 3 * SUB_ROWS + 1, 16) if tm % r == 0)


def _ret_in_kernel(x_ref, w_ref, cos_ref, sin_ref, o_ref, *, n_q_tiles):
    j = pl.program_id(1)
    sub = _sub_rows(x_ref.shape[0])
    half = RET_DK // 2

    @pl.when(j >= 2 * n_q_tiles)
    def _():
        o_ref[...] = _dot(x_ref[...], w_ref[...].astype(BF16)).astype(o_ref.dtype)

    @pl.when(j < 2 * n_q_tiles)
    def _():
        scale = jnp.where(j < n_q_tiles, 1.0, RET_DK ** -0.5)
        w = w_ref[...].astype(BF16)
        for r0 in range(0, x_ref.shape[0], sub):
            rows = slice(r0, r0 + sub)
            acc = _dot(x_ref[rows, :], w)
            cos = cos_ref[rows, :]
            sin = sin_ref[rows, :]
            for c0 in range(0, acc.shape[1], RET_DK):
                x1, x2 = acc[:, c0:c0 + half], acc[:, c0 + half:c0 + RET_DK]
                o_ref[rows, c0:c0 + half] = ((x1 * cos - x2 * sin) * scale).astype(o_ref.dtype)
                o_ref[rows, c0 + half:c0 + RET_DK] = ((x2 * cos + x1 * sin) * scale).astype(o_ref.dtype)


def _ret_in(xn, w, layer, tabs, name):
    T, K = xn.shape
    N = w.shape[-1]
    tn = 1024
    tm = _row_tile(T, (2816, 1536, 1024, 512))
    return pl.pallas_call(
        functools.partial(_ret_in_kernel, n_q_tiles=RET_HEADS * RET_DK // tn),
        grid=(T // tm, N // tn),
        in_specs=[pl.BlockSpec((tm, K), lambda i, j: (i, 0)),
                  pl.BlockSpec((None, K, tn), lambda i, j: (layer, 0, j)),
                  pl.BlockSpec((tm, RET_DK // 2), lambda i, j: (i, 0)),
                  pl.BlockSpec((tm, RET_DK // 2), lambda i, j: (i, 0))],
        out_specs=pl.BlockSpec((tm, tn), lambda i, j: (i, j)),
        out_shape=jax.ShapeDtypeStruct((T, N), BF16),
        compiler_params=_params(2),
        name=name,
    )(xn, w, tabs["ret_cos_rows"], tabs["ret_sin_rows"])


def _swiglu_kernel(x_ref, wa_ref, wb_ref, o_ref):
    wa = wa_ref[...].astype(BF16)
    wb = wb_ref[...].astype(BF16)
    sub = _sub_rows(x_ref.shape[0])
    for r0 in range(0, x_ref.shape[0], sub):
        rows = slice(r0, r0 + sub)
        a = _dot(x_ref[rows, :], wa)
        b = _dot(x_ref[rows, :], wb)
        o_ref[rows, :] = ((a * jax.nn.sigmoid(a)) * b).astype(o_ref.dtype)


def _swiglu(xn, w, layer, name):
    T, K = xn.shape
    tn = 256
    nb = D_FF // tn
    tm = _row_tile(T, (5632, 2816, 1536, 1024, 512))
    return pl.pallas_call(
        _swiglu_kernel,
        grid=(T // tm, nb),
        in_specs=[pl.BlockSpec((tm, K), lambda i, j: (i, 0)),
                  pl.BlockSpec((None, K, tn), lambda i, j: (layer, 0, j)),
                  pl.BlockSpec((None, K, tn), lambda i, j: (layer, 0, nb + j))],
        out_specs=pl.BlockSpec((tm, tn), lambda i, j: (i, j)),
        out_shape=jax.ShapeDtypeStruct((T, D_FF), BF16),
        compiler_params=_params(2),
        name=name,
    )(xn, w, w)


def _mm_res_kernel(*refs, n_a, n_r, n_gains, n_prompt_blocks, split_out):
    a_refs = refs[:n_a]
    w_ref = refs[n_a]
    r_refs = refs[n_a + 1:n_a + 1 + n_r]
    rest = refs[n_a + 1 + n_r:]
    gain_refs = rest[:n_gains]
    out_refs = rest[n_gains:-1]
    wb_ref = rest[-1]
    i = pl.program_id(0)

    @pl.when(i == 0)
    def _():
        wb_ref[...] = w_ref[...].astype(BF16)

    def step(a_ref, r_ref, dst):
        for r0 in range(0, a_ref.shape[0], SUB_ROWS):
            rows = slice(r0, r0 + SUB_ROWS)
            x = r_ref[rows, :] + _dot(a_ref[rows, :], wb_ref[...])
            out_refs[dst][rows, :] = x
            if not split_out:
                for o_ref, xn in zip(out_refs[1:], _rms_normed(x, gain_refs)):
                    o_ref[rows, :] = xn

    if n_a == 1 and n_r == 1 and not split_out:
        step(a_refs[0], r_refs[0], 0)
    else:
        @pl.when(i < n_prompt_blocks)
        def _():
            step(a_refs[0], r_refs[0], 0)

        @pl.when(i >= n_prompt_blocks)
        def _():
            step(a_refs[-1], r_refs[-1], 1 if split_out else 0)


def _matmul_residual(a, w, layer, res, gains, name, TP, split_out=False):
    a = a if isinstance(a, (tuple, list)) else (a,)
    res = res if isinstance(res, (tuple, list)) else (res,)
    T = sum(r.shape[0] for r in res)
    N = res[0].shape[1]
    K = a[0].shape[1]
    tm = ROW_BLOCK
    npb = TP // tm
    head = lambda i: (jnp.minimum(i, npb - 1), 0)
    tail = lambda i: (jnp.maximum(i - npb, 0), 0)
    whole = lambda i: (i, 0)
    rows_of = lambda arrs, width: [pl.BlockSpec((tm, width), fn)
                                   for fn in ((whole,) if len(arrs) == 1 else (head, tail))]
    in_specs = rows_of(a, K) + [pl.BlockSpec((None, K, N), lambda i: (layer, 0, 0))] + rows_of(res, N)
    in_specs += [pl.BlockSpec((None, 1, N), lambda i, idx=idx: (idx, 0, 0)) for _, idx in gains]
    if split_out:
        assert not gains
        out_specs = [pl.BlockSpec((tm, N), head), pl.BlockSpec((tm, N), tail)]
        out_shape = [jax.ShapeDtypeStruct((TP, N), F32), jax.ShapeDtypeStruct((T - TP, N), F32)]
    else:
        out_specs = [pl.BlockSpec((tm, N), whole)] * (1 + len(gains))
        out_shape = [jax.ShapeDtypeStruct((T, N), F32)] + [jax.ShapeDtypeStruct((T, N), BF16)] * len(gains)
    aliases = {len(a) + 1: 0} if len(res) == 1 and not split_out else {}
    return pl.pallas_call(
        functools.partial(_mm_res_kernel, n_a=len(a), n_r=len(res), n_gains=len(gains), n_prompt_blocks=npb,
                          split_out=split_out),
        grid=(T // tm,),
        in_specs=in_specs,
        out_specs=out_specs,
        out_shape=out_shape,
        scratch_shapes=[pltpu.VMEM((K, N), BF16)],
        input_output_aliases=aliases,
        compiler_params=_params(1),
        name=name,
    )(*a, w, *res, *[g for g, _ in gains])


def _groupnorm_gate(o, gate):
    mu = jnp.mean(o, axis=-1, keepdims=True)
    d = o - mu
    var = jnp.mean(d * d, axis=-1, keepdims=True)
    on = d * lax.rsqrt(var + EPS)
    return (gate * jax.nn.sigmoid(gate)) * on


def _retention_kernel(q_ref, k_ref, v_ref, g_ref, dmask_ref, qdec_ref, kdec_ref, cdec_ref,
                      qd_ref, kd_ref, vd_ref, gd_ref, dmaskd_ref, qdecd_ref, kdecd_ref, cdecd_ref, sd_ref,
                      *rest, chunk_rows, n_chunks, t_dec):
    go_ref, sfin_ref, god_ref, snew_ref, s_ref = rest[-5:]
    c = pl.program_id(2)

    qb = qd_ref[...]
    kb = kd_ref[...]
    vb = vd_ref[...]
    cdecd = cdecd_ref[...]
    inner = _dot((_dot_nt(qb, kb) * dmaskd_ref[...]).astype(BF16), vb)
    kdd = kb.astype(F32) * kdecd_ref[...]
    seq_of_row = lax.broadcasted_iota(jnp.int32, (RET_DEC_ROWS, 1), 0) // t_dec
    cross = jnp.zeros((RET_DEC_ROWS, RET_DV), F32)
    for bi in range(RET_DEC_ROWS // t_dec):
        mine = seq_of_row == bi
        s = sd_ref[bi]
        cross = cross + jnp.where(mine, _dot(qb, s.astype(BF16)), 0.0)
        snew_ref[bi] = s * cdecd + _dot_tn(jnp.where(mine, kdd, 0.0).astype(BF16), vb)
    cross = cross * qdecd_ref[...]
    god_ref[...] = _groupnorm_gate(inner + cross, gd_ref[...].astype(F32)).astype(god_ref.dtype)

    @pl.when(c == 0)
    def _():
        s_ref[...] = jnp.zeros_like(s_ref)

    dmask = dmask_ref[...]
    qdec = qdec_ref[...]
    kdec = kdec_ref[...]
    cdec = cdec_ref[...]

    def chunk(ci, carry):
        rows = pl.ds(pl.multiple_of(ci * chunk_rows, chunk_rows), chunk_rows)
        qb = q_ref[rows, :]
        kb = k_ref[rows, :]
        vb = v_ref[rows, :]
        s = s_ref[...]
        scores = _dot_nt(qb, kb) * dmask
        inner = _dot(scores.astype(BF16), vb)
        cross = _dot(qb, s.astype(BF16)) * qdec
        s_ref[...] = s * cdec + _dot_tn((kb.astype(F32) * kdec).astype(BF16), vb)
        go_ref[rows, :] = _groupnorm_gate(inner + cross, g_ref[rows, :].astype(F32)).astype(go_ref.dtype)
        return carry

    lax.fori_loop(0, n_chunks, chunk, 0, unroll=min(n_chunks, RET_UNROLL))

    @pl.when(c == pl.num_programs(2) - 1)
    def _():
        sfin_ref[...] = s_ref[...]


def _retention(proj, state, layer, snew_all, tabs, B, L, DB, t_dec, name):
    TP, TD = B * L, DB * t_dec
    nseq = RET_DEC_ROWS // t_dec
    steps = DB // nseq
    assert DB % nseq == 0 and steps % B == 0, (DB, nseq, B)
    nblk = steps // B
    rb = L // nblk
    ck = tabs["chunk_p"]
    assert L % nblk == 0 and rb % ck == 0, (L, nblk, ck)
    r0 = TP // RET_DEC_ROWS
    kcol = vcol = RET_HEADS
    gcol = 2 * RET_HEADS
    prow = lambda b, h, c: b * nblk + c
    drow = lambda b, h, c: b * nblk + c
    dec = lambda width, col0: pl.BlockSpec((RET_DEC_ROWS, width), lambda b, h, c: (r0 + drow(b, h, c), col0 + h))
    per_head = lambda *shape: pl.BlockSpec((None,) + shape, lambda b, h, c: (h,) + (0,) * len(shape))
    s_spec = pl.BlockSpec((None, nseq, None, RET_DK, RET_DV), lambda b, h, c: (layer, drow(b, h, c), h, 0, 0))
    aliased = [] if snew_all is None else [snew_all]
    kern = functools.partial(_retention_kernel, chunk_rows=ck, n_chunks=rb // ck, t_dec=t_dec)
    return pl.pallas_call(
        kern,
        grid=(B, RET_HEADS, nblk),
        in_specs=[pl.BlockSpec((rb, RET_DK), lambda b, h, c: (prow(b, h, c), h)),
                  pl.BlockSpec((rb, RET_DK), lambda b, h, c: (prow(b, h, c), kcol + h)),
                  pl.BlockSpec((rb, RET_DV), lambda b, h, c: (prow(b, h, c), vcol + h)),
                  pl.BlockSpec((rb, RET_DV), lambda b, h, c: (prow(b, h, c), gcol + h)),
                  per_head(ck, ck), per_head(ck, 1), per_head(ck, 1), per_head(1, 1),
                  dec(RET_DK, 0), dec(RET_DK, kcol), dec(RET_DV, vcol), dec(RET_DV, gcol),
                  per_head(RET_DEC_ROWS, RET_DEC_ROWS), per_head(RET_DEC_ROWS, 1), per_head(RET_DEC_ROWS, 1),
                  per_head(1, 1), s_spec] + [pl.BlockSpec(memory_space=pl.ANY)] * len(aliased),
        out_specs=[pl.BlockSpec((rb, RET_DV), lambda b, h, c: (prow(b, h, c), h)),
                   pl.BlockSpec((None, None, RET_DK, RET_DV), lambda b, h, c: (b, h, 0, 0)),
                   pl.BlockSpec((RET_DEC_ROWS, RET_DV), lambda b, h, c: (drow(b, h, c), h)),
                   s_spec],
        out_shape=[jax.ShapeDtypeStruct((TP, RET_HEADS * RET_DV), BF16),
                   jax.ShapeDtypeStruct((B, RET_HEADS, RET_DK, RET_DV), F32),
                   jax.ShapeDtypeStruct((TD, RET_HEADS * RET_DV), BF16),
                   jax.ShapeDtypeStruct(state.shape, state.dtype)],
        scratch_shapes=[pltpu.VMEM((RET_DK, RET_DV), F32)],
        input_output_aliases={17: 3} if aliased else {},
        compiler_params=_params(3),
        name=name,
    )(proj, proj, proj, proj, tabs["dmask_p"], tabs["qdec_p"], tabs["kdec_p"], tabs["cdec_p"],
      proj, proj, proj, proj, tabs["dmask_d"], tabs["qdec_d"], tabs["kdec_d"], tabs["cdec_d"], state, *aliased)


def _headnorm_rope(x, w, cos2, sin2, bd):
    width = x.shape[1]
    sq = (x * x).astype(BF16)
    parts = [_dot(sq[:, 256 * c:256 * (c + 1)], bd) for c in range(width // 256)]
    ss = parts[0] if len(parts) == 1 else jnp.concatenate(parts, axis=1)
    y = (x * lax.rsqrt(ss * (1.0 / HEAD_DIM) + EPS)) * w
    lane = lax.broadcasted_iota(jnp.int32, y.shape, 1)
    first_half = (lane & (HEAD_DIM - 1)) < HEAD_DIM // 2
    rot = jnp.where(first_half, pltpu.roll(y, width - HEAD_DIM // 2, 1), pltpu.roll(y, HEAD_DIM // 2, 1))
    nrep = width // 128
    cosw = jnp.concatenate([cos2] * nrep, axis=1)
    sinw = jnp.concatenate([sin2] * nrep, axis=1)
    return y * cosw + rot * sinw


def _kv_proj_kernel(x_ref, wkv_ref, w_ref, cos_ref, sin_ref, bd_ref, o_ref, v_ref, lo_ref, hi_ref, wb_ref):
    @pl.when(pl.program_id(0) == 0)
    def _():
        wb_ref[...] = wkv_ref[...].astype(BF16)

    kw = N_KV_HEADS * HEAD_DIM
    sub = SUB_ROWS
    low = lax.broadcasted_iota(jnp.int32, (sub, 128), 1) < HEAD_DIM
    zero = jnp.zeros((sub, 128), F32)
    for r0 in range(0, x_ref.shape[0], sub):
        rows = slice(r0, r0 + sub)
        kv = _dot(x_ref[rows, :], wb_ref[...])
        v_ref[rows, :] = kv[:, kw:]
        kn = _headnorm_rope(kv[:, :kw], w_ref[...], cos_ref[rows, :], sin_ref[rows, :], bd_ref[...])
        o_ref[rows, :] = kn
        for c in range(N_KV_HEADS // 2):
            tile = kn[:, 128 * c:128 * (c + 1)]
            swapped = pltpu.roll(tile, HEAD_DIM, 1)
            even, odd = slice(256 * c, 256 * c + 128), slice(256 * c + 128, 256 * (c + 1))
            lo_ref[rows, even] = jnp.where(low, tile, zero).astype(lo_ref.dtype)
            hi_ref[rows, even] = jnp.where(low, zero, swapped).astype(hi_ref.dtype)
            lo_ref[rows, odd] = jnp.where(low, swapped, zero).astype(lo_ref.dtype)
            hi_ref[rows, odd] = jnp.where(low, zero, tile).astype(hi_ref.dtype)


def _table_block(i, n_prompt_blocks, blocks_per_seq):
    return jnp.where(i < n_prompt_blocks, i % blocks_per_seq, blocks_per_seq + i - n_prompt_blocks)


def _kv_proj(xn, w_kv, k_norm_w, tabs, L, TP, name):
    T, K = xn.shape
    kw = N_KV_HEADS * HEAD_DIM
    npb, bps = TP // ROW_BLOCK, L // ROW_BLOCK
    tab = lambda i: (_table_block(i, npb, bps), 0)
    kv_spec = pl.BlockSpec((ROW_BLOCK, kw), lambda i: (i, 0))
    pad_spec = pl.BlockSpec((ROW_BLOCK, 128 * N_KV_HEADS), lambda i: (i, 0))
    pad_shape = jax.ShapeDtypeStruct((T, 128 * N_KV_HEADS), BF16)
    return pl.pallas_call(
        _kv_proj_kernel,
        grid=(T // ROW_BLOCK,),
        in_specs=[pl.BlockSpec((ROW_BLOCK, K), lambda i: (i, 0)),
                  pl.BlockSpec((K, 2 * kw), lambda i: (0, 0)),
                  pl.BlockSpec((1, kw), lambda i: (0, 0)),
                  pl.BlockSpec((ROW_BLOCK, 128), tab),
                  pl.BlockSpec((ROW_BLOCK, 128), tab),
                  pl.BlockSpec((256, 256), lambda i: (0, 0))],
        out_specs=[kv_spec, kv_spec, pad_spec, pad_spec],
        out_shape=[jax.ShapeDtypeStruct((T, kw), F32), jax.ShapeDtypeStruct((T, kw), F32), pad_shape, pad_shape],
        scratch_shapes=[pltpu.VMEM((K, 2 * kw), BF16)],
        compiler_params=_params(1),
        name=name,
    )(xn, w_kv, k_norm_w, tabs["att_cos"], tabs["att_sin"], tabs["bd"])


def _q_proj_kernel(x_ref, w_ref, qw_ref, cos_ref, sin_ref, bd_ref, o_ref, wb_ref):
    @pl.when(pl.program_id(0) == 0)
    def _():
        wb_ref[...] = w_ref[...].astype(BF16)

    bd = bd_ref[...]
    for r0 in range(0, x_ref.shape[0], 128):
        rows = slice(r0, r0 + 128)
        q = _dot(x_ref[rows, :], wb_ref[...])
        cos2, sin2 = cos_ref[rows, :], sin_ref[rows, :]
        for c0 in range(0, q.shape[1], 256):
            cols = slice(c0, c0 + 256)
            qc = _headnorm_rope(q[:, cols], qw_ref[:, cols], cos2, sin2, bd)
            o_ref[rows, cols] = (qc * (HEAD_DIM ** -0.5)).astype(o_ref.dtype)


def _q_proj(xn, w, q_norm_w, layer, tabs, L, TP, name):
    T, K = xn.shape
    N = w.shape[-1]
    npb, bps = TP // ROW_BLOCK, L // ROW_BLOCK
    tab = lambda i: (_table_block(i, npb, bps), 0)
    return pl.pallas_call(
        _q_proj_kernel,
        grid=(T // ROW_BLOCK,),
        in_specs=[pl.BlockSpec((ROW_BLOCK, K), lambda i: (i, 0)),
                  pl.BlockSpec((None, K, N), lambda i: (layer, 0, 0)),
                  pl.BlockSpec((None, 1, N), lambda i: (layer, 0, 0)),
                  pl.BlockSpec((ROW_BLOCK, 128), tab),
                  pl.BlockSpec((ROW_BLOCK, 128), tab),
                  pl.BlockSpec((256, 256), lambda i: (0, 0))],
        out_specs=pl.BlockSpec((ROW_BLOCK, N), lambda i: (i, 0)),
        out_shape=jax.ShapeDtypeStruct((T, N), BF16),
        scratch_shapes=[pltpu.VMEM((K, N), BF16)],
        compiler_params=_params(1),
        name=name,
    )(xn, w, q_norm_w, tabs["att_cos"], tabs["att_sin"], tabs["bd"])


def _sink_column(sink_ref, layer, g, rows, rows_per_head):
    head_in_group = lax.broadcasted_iota(jnp.int32, (rows, 1), 0) // rows_per_head
    col = jnp.full((rows, 1), sink_ref[layer, g * GROUP], F32)
    for r in range(1, GROUP):
        col = jnp.where(head_in_group == r, sink_ref[layer, g * GROUP + r], col)
    return col


def _attn_prompt_kernel(sink_ref, q_ref, lop_ref, loc_ref, hip_ref, hic_ref, vp_ref, vc_ref, o_ref, *,
                        layer, n_qblk):
    first = pl.program_id(1) == 0
    qb = q_ref[...]
    klo = jnp.concatenate([lop_ref[...], loc_ref[...]], axis=0)
    khi = jnp.concatenate([hip_ref[...], hic_ref[...]], axis=0)
    v_t = jnp.concatenate([vp_ref[...], vc_ref[...]], axis=0).T.astype(BF16)
    key = lax.broadcasted_iota(jnp.int32, (WINDOW, WINDOW), 0)
    qi = lax.broadcasted_iota(jnp.int32, (WINDOW, WINDOW), 1)
    own = key <= qi
    scores = {}
    for t in range(n_qblk):
        qrows = slice(WINDOW * t, WINDOW * (t + 1))
        krows = slice(WINDOW * t, WINDOW * (t + 2))
        for g in range(N_KV_HEADS):
            gl = slice(128 * g, 128 * (g + 1))
            kk = jnp.concatenate([klo[krows, gl], khi[krows, gl]], axis=0)
            xq = jnp.concatenate([qb[qrows, 256 * g:256 * g + 128], qb[qrows, 256 * g + 128:256 * (g + 1)]],
                                 axis=0)
            scores[t, g] = _dot_nt(kk, xq)
    for t in range(n_qblk):
        qrows = slice(WINDOW * t, WINDOW * (t + 1))
        krows = slice(WINDOW * t, WINDOW * (t + 2))
        pieces = []
        for g in range(N_KV_HEADS):
            s4 = scores[t, g]
            v_g = v_t[HEAD_DIM * g:HEAD_DIM * (g + 1), krows]
            for pair in range(GROUP // 2):
                p2s, invs = [], []
                for parity in range(2):
                    sk = sink_ref[layer, g * GROUP + 2 * pair + parity]
                    blk = s4[2 * WINDOW * parity:2 * WINDOW * (parity + 1), WINDOW * pair:WINDOW * (pair + 1)]
                    s_prev = blk[:WINDOW]
                    if t == 0:
                        s_prev = jnp.where(first, NEG_INF, s_prev)
                    s = jnp.where(own, blk[WINDOW:], s_prev)
                    m = jnp.maximum(jnp.max(s, axis=0, keepdims=True), sk)
                    p = jnp.exp(s - m)
                    invs.append(1.0 / (jnp.sum(p, axis=0, keepdims=True) + jnp.exp(sk - m)))
                    p2s.append(jnp.concatenate([jnp.where(own, 0.0, p), jnp.where(own, p, 0.0)], axis=0))
                p2 = jnp.concatenate(p2s, axis=1).astype(BF16)
                o_t = _dot(v_g, p2) * jnp.concatenate(invs, axis=1)
                pieces += [o_t[:, :WINDOW], o_t[:, WINDOW:]]
        o_ref[qrows, :] = jnp.concatenate(pieces, axis=0).T.astype(o_ref.dtype)


def _attn_prompt(q, klo, khi, v, sinks, layer, B, L, name):
    nb = L // WINDOW
    nq = ATTN_Q_BLOCKS if nb % ATTN_Q_BLOCKS == 0 else 1
    ns = nb // nq
    kw = N_KV_HEADS * HEAD_DIM
    kpad = 128 * N_KV_HEADS
    cur = lambda b, i: b * ns + i
    prev = lambda b, i: b * nb + jnp.maximum(i * nq - 1, 0)
    kern = functools.partial(_attn_prompt_kernel, layer=layer, n_qblk=nq)
    return pl.pallas_call(
        kern,
        grid=(B, ns),
        in_specs=[pl.BlockSpec(memory_space=pltpu.SMEM),
                  pl.BlockSpec((nq * WINDOW, D_MODEL), lambda b, i: (cur(b, i), 0)),
                  pl.BlockSpec((WINDOW, kpad), lambda b, i: (prev(b, i), 0)),
                  pl.BlockSpec((nq * WINDOW, kpad), lambda b, i: (cur(b, i), 0)),
                  pl.BlockSpec((WINDOW, kpad), lambda b, i: (prev(b, i), 0)),
                  pl.BlockSpec((nq * WINDOW, kpad), lambda b, i: (cur(b, i), 0)),
                  pl.BlockSpec((WINDOW, kw), lambda b, i: (prev(b, i), 0)),
                  pl.BlockSpec((nq * WINDOW, kw), lambda b, i: (cur(b, i), 0))],
        out_specs=pl.BlockSpec((nq * WINDOW, D_MODEL), lambda b, i: (cur(b, i), 0)),
        out_shape=jax.ShapeDtypeStruct((B * L, D_MODEL), BF16),
        compiler_params=_params(2),
        name=name,
    )(sinks, q, klo, klo, khi, khi, v, v)


def _attn_decode_kernel(sink_ref, q_ref, kn_ref, vn_ref, ck_ref, cv_ref, o_ref, *, layer, t_dec):
    nseq = DEC_ROWS // t_dec
    wc = ck_ref.shape[1]
    qb = q_ref[...]
    knew = kn_ref[...].astype(BF16)
    vnew = vn_ref[...].astype(BF16)
    rows = GROUP * DEC_ROWS
    row = lax.broadcasted_iota(jnp.int32, (rows, 1), 0)
    row_seq = (row % DEC_ROWS) // t_dec
    row_tok = row % t_dec
    jold = lax.broadcasted_iota(jnp.int32, (rows, wc), 1)
    vis_old = (jold > row_tok + (wc - WINDOW)) & (jold <= row_tok + wc)
    cnew = lax.broadcasted_iota(jnp.int32, (rows, DEC_ROWS), 1)
    vis_new = (cnew // t_dec == row_seq) & (cnew % t_dec <= row_tok)
    scores = []
    for g in range(N_KV_HEADS):
        heads = [g * GROUP + r for r in range(GROUP)]
        hs = slice(HEAD_DIM * g, HEAD_DIM * (g + 1))
        qs = jnp.concatenate([qb[:, HEAD_DIM * h:HEAD_DIM * (h + 1)] for h in heads], axis=0)
        s_old = jnp.zeros((rows, wc), F32)
        for bi in range(nseq):
            kc = ck_ref[bi][:, hs].astype(BF16)
            s_old = s_old + jnp.where(row_seq == bi, _dot_nt(qs, kc), 0.0)
        scores.append((s_old, _dot_nt(qs, knew[:, hs])))
    for g in range(N_KV_HEADS):
        heads = [g * GROUP + r for r in range(GROUP)]
        hs = slice(HEAD_DIM * g, HEAD_DIM * (g + 1))
        s_old = jnp.where(vis_old, scores[g][0], NEG_INF)
        s_new = jnp.where(vis_new, scores[g][1], NEG_INF)
        sk = _sink_column(sink_ref, layer, g, rows, DEC_ROWS)
        m = jnp.maximum(jnp.maximum(jnp.max(s_old, axis=-1, keepdims=True),
                                    jnp.max(s_new, axis=-1, keepdims=True)), sk)
        p_old = jnp.exp(s_old - m)
        p_new = jnp.exp(s_new - m)
        denom = (jnp.sum(p_old, axis=-1, keepdims=True) + jnp.sum(p_new, axis=-1, keepdims=True)
                 + jnp.exp(sk - m))
        inv = 1.0 / denom
        o = _dot((p_new * inv).astype(BF16), vnew[:, hs])
        pn_old = p_old * inv
        for bi in range(nseq):
            vc = cv_ref[bi][:, hs].astype(BF16)
            o = o + _dot(jnp.where(row_seq == bi, pn_old, 0.0).astype(BF16), vc)
        for r, h in enumerate(heads):
            o_ref[:, HEAD_DIM * h:HEAD_DIM * (h + 1)] = o[DEC_ROWS * r:DEC_ROWS * (r + 1)].astype(o_ref.dtype)


def _attn_decode(q, kn, v, cache_k, cache_v, sinks, layer, TP, DB, t_dec, name):
    nseq = DEC_ROWS // t_dec
    r0 = TP // DEC_ROWS
    kw = N_KV_HEADS * HEAD_DIM
    wc = cache_k.shape[1]
    kern = functools.partial(_attn_decode_kernel, layer=layer, t_dec=t_dec)
    return pl.pallas_call(
        kern,
        grid=(DB // nseq,),
        in_specs=[pl.BlockSpec(memory_space=pltpu.SMEM),
                  pl.BlockSpec((DEC_ROWS, D_MODEL), lambda i: (r0 + i, 0)),
                  pl.BlockSpec((DEC_ROWS, kw), lambda i: (r0 + i, 0)),
                  pl.BlockSpec((DEC_ROWS, kw), lambda i: (r0 + i, 0)),
                  pl.BlockSpec((nseq, wc, kw), lambda i: (i, 0, 0)),
                  pl.BlockSpec((nseq, wc, kw), lambda i: (i, 0, 0))],
        out_specs=pl.BlockSpec((DEC_ROWS, D_MODEL), lambda i: (i, 0)),
        out_shape=jax.ShapeDtypeStruct((DB * t_dec, D_MODEL), BF16),
        compiler_params=_params(1),
        name=name,
    )(sinks, q, kn, v, cache_k, cache_v)


def _rope_cos_sin(pos, half):
    inv = 1.0 / (ROPE_THETA ** (np.arange(half, dtype=np.float64) / half))
    ang = pos.astype(np.float64)[:, None] * inv[None, :]
    return np.cos(ang), np.sin(ang)


def _decay_tables(chunk, reps):
    lg = np.log(1.0 - 2.0 ** (-5.0 - np.arange(RET_HEADS, dtype=np.float64)))
    idx = np.arange(chunk, dtype=np.float64)
    rel = idx[:, None] - idx[None, :]
    dmask = np.where(rel >= 0, np.exp(lg[:, None, None] * np.maximum(rel, 0.0)), 0.0)
    qdec = np.exp(lg[:, None] * (idx + 1.0))[:, :, None]
    kdec = np.exp(lg[:, None] * (chunk - 1.0 - idx))[:, :, None]
    cdec = np.exp(lg * chunk)[:, None, None]
    if reps > 1:
        dmask = np.einsum("ab,hij->haibj", np.eye(reps), dmask).reshape(RET_HEADS, reps * chunk, reps * chunk)
        qdec = np.tile(qdec, (1, reps, 1))
        kdec = np.tile(kdec, (1, reps, 1))
    return dmask, qdec, kdec, cdec


def _tables(B, L, TD, t_dec, chunk_p):
    pos = np.concatenate([np.arange(L), PAST_LEN + np.arange(TD) % t_dec])
    ret_cos, ret_sin = _rope_cos_sin(pos, RET_DK // 2)
    per_row = lambda t: np.concatenate([np.tile(t[:L], (B, 1)), t[L:]], axis=0)
    c, s = _rope_cos_sin(pos, HEAD_DIM // 2)
    att_cos = np.concatenate([c, c, c, c], axis=1)
    att_sin = np.concatenate([-s, s, -s, s], axis=1)
    dmask_p, qdec_p, kdec_p, cdec_p = _decay_tables(chunk_p, 1)
    dmask_d, qdec_d, kdec_d, cdec_d = _decay_tables(math.gcd(t_dec, RET_CHUNK), RET_DEC_ROWS // t_dec)
    head_of_lane = np.arange(256) // HEAD_DIM
    f32 = dict(ret_cos_rows=per_row(ret_cos), ret_sin_rows=per_row(ret_sin), att_cos=att_cos, att_sin=att_sin,
               dmask_p=dmask_p, qdec_p=qdec_p, kdec_p=kdec_p, cdec_p=cdec_p,
               dmask_d=dmask_d, qdec_d=qdec_d, kdec_d=kdec_d, cdec_d=cdec_d)
    tabs = {name: jnp.asarray(t.astype(np.float32)) for name, t in f32.items()}
    tabs["bd"] = jnp.asarray((head_of_lane[:, None] == head_of_lane[None, :]).astype(np.float32), dtype=BF16)
    tabs["chunk_p"] = chunk_p
    return tabs


def kernel(x_prompt, x_sample, state_ret, cache_k, cache_v, ln_ret, w_ret_in, w_ret_out, ln_ffn, w_ffn_in,
           w_ffn_out, ln_kv, w_kv, k_norm, ln_attn, w_q, q_norm, sinks, w_o):
    B, L, D = x_prompt.shape
    DB, t_dec, _ = x_sample.shape
    n_ret = w_ret_in.shape[0]
    n_attn = w_q.shape[0]
    TP, TD = B * L, DB * t_dec
    T = TP + TD
    wc = cache_k.shape[1]
    kw = N_KV_HEADS * HEAD_DIM
    assert D == D_MODEL and L % RET_CHUNK == 0 and L % ROW_BLOCK == 0 and TD % ROW_BLOCK == 0
    assert t_dec == math.gcd(t_dec, RET_CHUNK) and DEC_ROWS % t_dec == 0 and wc == WINDOW and PAST_LEN >= WINDOW

    tabs = _tables(B, L, TD, t_dec, RET_CHUNK_PROMPT if L % RET_CHUNK_PROMPT == 0 else RET_CHUNK)
    ck = cache_k.reshape(DB, wc, kw)
    cv = cache_v.reshape(DB, wc, kw)
    q_norm_w = jnp.tile(q_norm, (1, N_HEADS))[:, None, :]
    k_norm_w = jnp.tile(k_norm[None, :], (1, N_KV_HEADS))
    ln_ret, ln_ffn, ln_attn = ln_ret[:, None, :], ln_ffn[:, None, :], ln_attn[:, None, :]

    ln_kv3 = ln_kv[None, None, :]
    prompt_states = []
    dec_states = None
    x = (x_prompt.reshape(TP, D), x_sample.reshape(TD, D))
    xn = _embed(*x, (ln_ret, 0), "embed")
    for l in range(n_ret):
        proj = _ret_in(xn, w_ret_in, l, tabs, f"ret_in_{l}")
        go_p, s_p, go_d, dec_states = _retention(proj, state_ret, l, dec_states, tabs, B, L, DB, t_dec,
                                                 f"retention_{l}")
        prompt_states.append(s_p)
        x, xn = _matmul_residual((go_p, go_d), w_ret_out, l, x, [(ln_ffn, l)], f"ret_out_{l}", TP)
        h = _swiglu(xn, w_ffn_in, l, f"ffn_in_{l}")
        if l + 1 < n_ret:
            x, xn = _matmul_residual(h, w_ffn_out, l, x, [(ln_ret, l + 1)], f"ffn_out_{l}", TP)
        else:
            x, xn_kv, xn = _matmul_residual(h, w_ffn_out, l, x, [(ln_kv3, 0), (ln_attn, 0)], f"ffn_out_{l}", TP)

    kn, v, klo, khi = _kv_proj(xn_kv, w_kv, k_norm_w, tabs, L, TP, "kv_proj")
    for j in range(n_attn):
        layer = n_ret + j
        q = _q_proj(xn, w_q, q_norm_w, j, tabs, L, TP, f"q_proj_{j}")
        ao_p = _attn_prompt(q, klo, khi, v, sinks, j, B, L, f"attn_prompt_{j}")
        ao_d = _attn_decode(q, kn, v, ck, cv, sinks, j, TP, DB, t_dec, f"attn_decode_{j}")
        x, xn = _matmul_residual((ao_p, ao_d), w_o, j, x, [(ln_ffn, layer)], f"attn_out_{j}", TP)
        h = _swiglu(xn, w_ffn_in, layer, f"ffn_in_{layer}")
        if j + 1 < n_attn:
            x, xn = _matmul_residual(h, w_ffn_out, layer, x, [(ln_attn, j + 1)], f"ffn_out_{layer}", TP)
        else:
            y_p, y_s = _matmul_residual(h, w_ffn_out, layer, x, [], f"ffn_out_{layer}", TP, split_out=True)

    y_prompt = y_p.reshape(B, L, D)
    y_sample = y_s.reshape(DB, t_dec, D)
    state_prompt = jnp.stack(prompt_states)
    w_keep = min(WINDOW, L)
    tail_rows = lambda t: t[:TP].reshape(B, L, kw)[:, L - w_keep:].reshape(B, w_keep, N_KV_HEADS, HEAD_DIM)
    kn_p, v_p = tail_rows(kn), tail_rows(v)
    kn_d = kn[TP:].reshape(DB, t_dec, N_KV_HEADS, HEAD_DIM)
    v_d = v[TP:].reshape(DB, t_dec, N_KV_HEADS, HEAD_DIM)
    cache_k_sample = jnp.concatenate([cache_k, kn_d], axis=1)[:, -wc:]
    cache_v_sample = jnp.concatenate([cache_v, v_d], axis=1)[:, -wc:]
    return (y_prompt, y_sample, state_prompt, dec_states, kn_p, v_p,
            cache_k_sample, cache_v_sample)
```

```python
import functools
import math

import numpy as np
import jax
import jax.numpy as jnp
from jax import lax
from jax.experimental import pallas as pl
from jax.experimental.pallas import tpu as pltpu

D_MODEL = 1024
PAST_LEN = 8192
RET_HEADS = 4
RET_DK = 256
RET_DV = 512
RET_CHUNK = 128
RET_CHUNK_PROMPT = 256
N_HEADS = 16
N_KV_HEADS = 4
HEAD_DIM = 64
GROUP = N_HEADS // N_KV_HEADS
WINDOW = 128
D_FF = 2816
ROPE_THETA = 10000.0
EPS = 1e-6
NEG_INF = -1e30

F32 = jnp.float32
BF16 = jnp.bfloat16

V7X_VMEM_LIMIT_BYTES = 56 * 1024 * 1024
DEC_ROWS = 16
RET_DEC_ROWS = 32
ROW_BLOCK = 512
SUB_ROWS = 256
RET_UNROLL = 4
ATTN_Q_BLOCKS = 8


def _dot(a, b):
    return jnp.dot(a, b, preferred_element_type=F32)


def _dot_nt(a, b):
    return lax.dot_general(a, b, (((1,), (1,)), ((), ())), preferred_element_type=F32)


def _dot_tn(a, b):
    return lax.dot_general(a, b, (((0,), (0,)), ((), ())), preferred_element_type=F32)


def _params(n_axes):
    return pltpu.CompilerParams(dimension_semantics=("arbitrary",) * n_axes,
                                vmem_limit_bytes=V7X_VMEM_LIMIT_BYTES)


def _row_tile(rows, prefs):
    for t in prefs:
        if rows % t == 0:
            return t
    raise ValueError(f"no row tile for {rows}")


def _rms_normed(x, gain_refs):
    ms = jnp.mean(x * x, axis=-1, keepdims=True)
    xh = x * lax.rsqrt(ms + EPS)
    return [(xh * g_ref[...]).astype(BF16) for g_ref in gain_refs]


def _embed_kernel(xp_ref, xs_ref, g_ref, xn_ref, *, n_prompt_blocks):
    i = pl.program_id(0)

    @pl.when(i < n_prompt_blocks)
    def _():
        xn_ref[...] = _rms_normed(xp_ref[...], [g_ref])[0]

    @pl.when(i >= n_prompt_blocks)
    def _():
        xn_ref[...] = _rms_normed(xs_ref[...], [g_ref])[0]


def _embed(xp, xs, gain, name):
    TP, D = xp.shape
    TD = xs.shape[0]
    npb = TP // ROW_BLOCK
    T = TP + TD
    g_arr, g_idx = gain
    blk = lambda fn: pl.BlockSpec((ROW_BLOCK, D), fn)
    return pl.pallas_call(
        functools.partial(_embed_kernel, n_prompt_blocks=npb),
        grid=(T // ROW_BLOCK,),
        in_specs=[blk(lambda i: (jnp.minimum(i, npb - 1), 0)),
                  blk(lambda i: (jnp.maximum(i - npb, 0), 0)),
                  pl.BlockSpec((None, 1, D), lambda i: (g_idx, 0, 0))],
        out_specs=blk(lambda i: (i, 0)),
        out_shape=jax.ShapeDtypeStruct((T, D), BF16),
        compiler_params=_params(1),
        name=name,
    )(xp, xs, g_arr)


def _sub_rows(tm):
    return max(r for r in range(16, 3 * SUB_ROWS + 1, 16) if tm % r == 0)


def _ret_in_kernel(x_ref, w_ref, cos_ref, sin_ref, o_ref, *, n_q_tiles):
    j = pl.program_id(1)
    sub = _sub_rows(x_ref.shape[0])
    half = RET_DK // 2

    @pl.when(j >= 2 * n_q_tiles)
    def _():
        o_ref[...] = _dot(x_ref[...], w_ref[...].astype(BF16)).astype(o_ref.dtype)

    @pl.when(j < 2 * n_q_tiles)
    def _():
        scale = jnp.where(j < n_q_tiles, 1.0, RET_DK ** -0.5)
        w = w_ref[...].astype(BF16)
        for r0 in range(0, x_ref.shape[0], sub):
            rows = slice(r0, r0 + sub)
            acc = _dot(x_ref[rows, :], w)
            cos = cos_ref[rows, :]
            sin = sin_ref[rows, :]
            for c0 in range(0, acc.shape[1], RET_DK):
                x1, x2 = acc[:, c0:c0 + half], acc[:, c0 + half:c0 + RET_DK]
                o_ref[rows, c0:c0 + half] = ((x1 * cos - x2 * sin) * scale).astype(o_ref.dtype)
                o_ref[rows, c0 + half:c0 + RET_DK] = ((x2 * cos + x1 * sin) * scale).astype(o_ref.dtype)


def _ret_in(xn, w, layer, tabs, name):
    T, K = xn.shape
    N = w.shape[-1]
    tn = 1024
    tm = _row_tile(T, (2816, 1536, 1024, 512))
    return pl.pallas_call(
        functools.partial(_ret_in_kernel, n_q_tiles=RET_HEADS * RET_DK // tn),
        grid=(T // tm, N // tn),
        in_specs=[pl.BlockSpec((tm, K), lambda i, j: (i, 0)),
                  pl.BlockSpec((None, K, tn), lambda i, j: (layer, 0, j)),
                  pl.BlockSpec((tm, RET_DK // 2), lambda i, j: (i, 0)),
                  pl.BlockSpec((tm, RET_DK // 2), lambda i, j: (i, 0))],
        out_specs=pl.BlockSpec((tm, tn), lambda i, j: (i, j)),
        out_shape=jax.ShapeDtypeStruct((T, N), BF16),
        compiler_params=_params(2),
        name=name,
    )(xn, w, tabs["ret_cos_rows"], tabs["ret_sin_rows"])


def _swiglu_kernel(x_ref, wa_ref, wb_ref, o_ref):
    wa = wa_ref[...].astype(BF16)
    wb = wb_ref[...].astype(BF16)
    sub = _sub_rows(x_ref.shape[0])
    for r0 in range(0, x_ref.shape[0], sub):
        rows = slice(r0, r0 + sub)
        a = _dot(x_ref[rows, :], wa)
        b = _dot(x_ref[rows, :], wb)
        o_ref[rows, :] = ((a * jax.nn.sigmoid(a)) * b).astype(o_ref.dtype)


def _swiglu(xn, w, layer, name):
    T, K = xn.shape
    tn = 256
    nb = D_FF // tn
    tm = _row_tile(T, (5632, 2816, 1536, 1024, 512))
    return pl.pallas_call(
        _swiglu_kernel,
        grid=(T // tm, nb),
        in_specs=[pl.BlockSpec((tm, K), lambda i, j: (i, 0)),
                  pl.BlockSpec((None, K, tn), lambda i, j: (layer, 0, j)),
                  pl.BlockSpec((None, K, tn), lambda i, j: (layer, 0, nb + j))],
        out_specs=pl.BlockSpec((tm, tn), lambda i, j: (i, j)),
        out_shape=jax.ShapeDtypeStruct((T, D_FF), BF16),
        compiler_params=_params(2),
        name=name,
    )(xn, w, w)


def _mm_res_kernel(*refs, n_a, n_r, n_gains, n_prompt_blocks, split_out):
    a_refs = refs[:n_a]
    w_ref = refs[n_a]
    r_refs = refs[n_a + 1:n_a + 1 + n_r]
    rest = refs[n_a + 1 + n_r:]
    gain_refs = rest[:n_gains]
    out_refs = rest[n_gains:-1]
    wb_ref = rest[-1]
    i = pl.program_id(0)

    @pl.when(i == 0)
    def _():
        wb_ref[...] = w_ref[...].astype(BF16)

    def step(a_ref, r_ref, dst):
        for r0 in range(0, a_ref.shape[0], SUB_ROWS):
            rows = slice(r0, r0 + SUB_ROWS)
            x = r_ref[rows, :] + _dot(a_ref[rows, :], wb_ref[...])
            out_refs[dst][rows, :] = x
            if not split_out:
                for o_ref, xn in zip(out_refs[1:], _rms_normed(x, gain_refs)):
                    o_ref[rows, :] = xn

    if n_a == 1 and n_r == 1 and not split_out:
        step(a_refs[0], r_refs[0], 0)
    else:
        @pl.when(i < n_prompt_blocks)
        def _():
            step(a_refs[0], r_refs[0], 0)

        @pl.when(i >= n_prompt_blocks)
        def _():
            step(a_refs[-1], r_refs[-1], 1 if split_out else 0)


def _matmul_residual(a, w, layer, res, gains, name, TP, split_out=False):
    a = a if isinstance(a, (tuple, list)) else (a,)
    res = res if isinstance(res, (tuple, list)) else (res,)
    T = sum(r.shape[0] for r in res)
    N = res[0].shape[1]
    K = a[0].shape[1]
    tm = ROW_BLOCK
    npb = TP // tm
    head = lambda i: (jnp.minimum(i, npb - 1), 0)
    tail = lambda i: (jnp.maximum(i - npb, 0), 0)
    whole = lambda i: (i, 0)
    rows_of = lambda arrs, width: [pl.BlockSpec((tm, width), fn)
                                   for fn in ((whole,) if len(arrs) == 1 else (head, tail))]
    in_specs = rows_of(a, K) + [pl.BlockSpec((None, K, N), lambda i: (layer, 0, 0))] + rows_of(res, N)
    in_specs += [pl.BlockSpec((None, 1, N), lambda i, idx=idx: (idx, 0, 0)) for _, idx in gains]
    if split_out:
        assert not gains
        out_specs = [pl.BlockSpec((tm, N), head), pl.BlockSpec((tm, N), tail)]
        out_shape = [jax.ShapeDtypeStruct((TP, N), F32), jax.ShapeDtypeStruct((T - TP, N), F32)]
    else:
        out_specs = [pl.BlockSpec((tm, N), whole)] * (1 + len(gains))
        out_shape = [jax.ShapeDtypeStruct((T, N), F32)] + [jax.ShapeDtypeStruct((T, N), BF16)] * len(gains)
    aliases = {len(a) + 1: 0} if len(res) == 1 and not split_out else {}
    return pl.pallas_call(
        functools.partial(_mm_res_kernel, n_a=len(a), n_r=len(res), n_gains=len(gains), n_prompt_blocks=npb,
                          split_out=split_out),
        grid=(T // tm,),
        in_specs=in_specs,
        out_specs=out_specs,
        out_shape=out_shape,
        scratch_shapes=[pltpu.VMEM((K, N), BF16)],
        input_output_aliases=aliases,
        compiler_params=_params(1),
        name=name,
    )(*a, w, *res, *[g for g, _ in gains])


def _groupnorm_gate(o, gate):
    mu = jnp.mean(o, axis=-1, keepdims=True)
    d = o - mu
    var = jnp.mean(d * d, axis=-1, keepdims=True)
    on = d * lax.rsqrt(var + EPS)
    return (gate * jax.nn.sigmoid(gate)) * on


def _retention_kernel(q_ref, k_ref, v_ref, g_ref, dmask_ref, qdec_ref, kdec_ref, cdec_ref,
                      qd_ref, kd_ref, vd_ref, gd_ref, dmaskd_ref, qdecd_ref, kdecd_ref, cdecd_ref, sd_ref,
                      *rest, chunk_rows, n_chunks, t_dec):
    go_ref, sfin_ref, god_ref, snew_ref, s_ref = rest[-5:]
    c = pl.program_id(2)

    qb = qd_ref[...]
    kb = kd_ref[...]
    vb = vd_ref[...]
    cdecd = cdecd_ref[...]
    inner = _dot((_dot_nt(qb, kb) * dmaskd_ref[...]).astype(BF16), vb)
    kdd = kb.astype(F32) * kdecd_ref[...]
    seq_of_row = lax.broadcasted_iota(jnp.int32, (RET_DEC_ROWS, 1), 0) // t_dec
    cross = jnp.zeros((RET_DEC_ROWS, RET_DV), F32)
    for bi in range(RET_DEC_ROWS // t_dec):
        mine = seq_of_row == bi
        s = sd_ref[bi]
        cross = cross + jnp.where(mine, _dot(qb, s.astype(BF16)), 0.0)
        snew_ref[bi] = s * cdecd + _dot_tn(jnp.where(mine, kdd, 0.0).astype(BF16), vb)
    cross = cross * qdecd_ref[...]
    god_ref[...] = _groupnorm_gate(inner + cross, gd_ref[...].astype(F32)).astype(god_ref.dtype)

    @pl.when(c == 0)
    def _():
        s_ref[...] = jnp.zeros_like(s_ref)

    dmask = dmask_ref[...]
    qdec = qdec_ref[...]
    kdec = kdec_ref[...]
    cdec = cdec_ref[...]

    def chunk(ci, carry):
        rows = pl.ds(pl.multiple_of(ci * chunk_rows, chunk_rows), chunk_rows)
        qb = q_ref[rows, :]
        kb = k_ref[rows, :]
        vb = v_ref[rows, :]
        s = s_ref[...]
        scores = _dot_nt(qb, kb) * dmask
        inner = _dot(scores.astype(BF16), vb)
        cross = _dot(qb, s.astype(BF16)) * qdec
        s_ref[...] = s * cdec + _dot_tn((kb.astype(F32) * kdec).astype(BF16), vb)
        go_ref[rows, :] = _groupnorm_gate(inner + cross, g_ref[rows, :].astype(F32)).astype(go_ref.dtype)
        return carry

    lax.fori_loop(0, n_chunks, chunk, 0, unroll=min(n_chunks, RET_UNROLL))

    @pl.when(c == pl.num_programs(2) - 1)
    def _():
        sfin_ref[...] = s_ref[...]


def _rope_full_head(x, cos, sin):
    x1, x2 = x[:, :RET_DK // 2], x[:, RET_DK // 2:]
    return jnp.concatenate([x1 * cos - x2 * sin, x2 * cos + x1 * sin], axis=1)


def _ret_layer_kernel(x_ref, xd_ref, wq_ref, wk_ref, wv_ref, wg_ref, cos_ref, sin_ref, cosd_ref, sind_ref,
                      dmask_ref, qdec_ref, kdec_ref, cdec_ref, dmaskd_ref, qdecd_ref, kdecd_ref, cdecd_ref,
                      sd_ref, *rest, chunk_rows, n_chunks, t_dec):
    go_ref, sfin_ref, god_ref, snew_ref, s_ref, w_ref = rest[-6:]
    b, c = pl.program_id(1), pl.program_id(2)
    kcol, vcol, gcol = RET_DK, 2 * RET_DK, 2 * RET_DK + RET_DV

    @pl.when((b == 0) & (c == 0))
    def _():
        w_ref[:, :kcol] = wq_ref[...].astype(BF16)
        w_ref[:, kcol:vcol] = wk_ref[...].astype(BF16)
        w_ref[:, vcol:gcol] = wv_ref[...].astype(BF16)
        w_ref[:, gcol:] = wg_ref[...].astype(BF16)

    @pl.when(c == 0)
    def _():
        s_ref[...] = jnp.zeros_like(s_ref)

    def project(x, cos, sin):
        p = _dot(x, w_ref[...])
        q = _rope_full_head(p[:, :kcol], cos, sin).astype(BF16)
        k = (_rope_full_head(p[:, kcol:vcol], cos, sin) * (RET_DK ** -0.5)).astype(BF16)
        return q, k, p[:, vcol:gcol].astype(BF16), p[:, gcol:]

    dmask = dmask_ref[...]
    qdec = qdec_ref[...]
    kdec = kdec_ref[...]
    cdec = cdec_ref[...]
    for ci in range(n_chunks):
        rows = slice(ci * chunk_rows, (ci + 1) * chunk_rows)
        x, cos, sin = x_ref[rows, :], cos_ref[rows, :], sin_ref[rows, :]
        if ci == 0:
            x = jnp.concatenate([x, xd_ref[...]], axis=0)
            cos = jnp.concatenate([cos, cosd_ref[...]], axis=0)
            sin = jnp.concatenate([sin, sind_ref[...]], axis=0)
        qb, kb, vb, gate = project(x, cos, sin)
        if ci == 0:
            qd, kd, vd, gd = (t[chunk_rows:] for t in (qb, kb, vb, gate))
            qb, kb, vb, gate = (t[:chunk_rows] for t in (qb, kb, vb, gate))
        s = s_ref[...]
        scores = _dot_nt(qb, kb) * dmask
        inner = _dot(scores.astype(BF16), vb)
        cross = _dot(qb, s.astype(BF16)) * qdec
        s_ref[...] = s * cdec + _dot_tn((kb.astype(F32) * kdec).astype(BF16), vb)
        go_ref[rows, :] = _groupnorm_gate(inner + cross, gate).astype(go_ref.dtype)

    @pl.when(c == pl.num_programs(2) - 1)
    def _():
        sfin_ref[...] = s_ref[...]

    cdecd = cdecd_ref[...]
    inner = _dot((_dot_nt(qd, kd) * dmaskd_ref[...]).astype(BF16), vd)
    kdd = kd.astype(F32) * kdecd_ref[...]
    seq_of_row = lax.broadcasted_iota(jnp.int32, (RET_DEC_ROWS, 1), 0) // t_dec
    cross = jnp.zeros((RET_DEC_ROWS, RET_DV), F32)
    for bi in range(RET_DEC_ROWS // t_dec):
        mine = seq_of_row == bi
        s = sd_ref[bi]
        cross = cross + jnp.where(mine, _dot(qd, s.astype(BF16)), 0.0)
        snew_ref[bi] = s * cdecd + _dot_tn(jnp.where(mine, kdd, 0.0).astype(BF16), vd)
    cross = cross * qdecd_ref[...]
    god_ref[...] = _groupnorm_gate(inner + cross, gd).astype(god_ref.dtype)


def _ret_layer(xn, w, state, layer, snew_all, tabs, B, L, DB, t_dec, name):
    TP, TD = B * L, DB * t_dec
    K = xn.shape[1]
    nseq = RET_DEC_ROWS // t_dec
    steps = DB // nseq
    assert DB % nseq == 0 and steps % B == 0, (DB, nseq, B)
    nblk = steps // B
    rb = L // nblk
    ck = tabs["chunk_p"]
    assert L % nblk == 0 and rb % ck == 0, (L, nblk, ck)
    r0 = TP // RET_DEC_ROWS
    blk = lambda h, b, c: b * nblk + c
    wcol = lambda width, col0: pl.BlockSpec((None, K, width), lambda h, b, c: (layer, 0, col0 + h))
    per_head = lambda *shape: pl.BlockSpec((None,) + shape, lambda h, b, c: (h,) + (0,) * len(shape))
    s_spec = pl.BlockSpec((None, nseq, None, RET_DK, RET_DV), lambda h, b, c: (layer, blk(h, b, c), h, 0, 0))
    aliased = [] if snew_all is None else [snew_all]
    kern = functools.partial(_ret_layer_kernel, chunk_rows=ck, n_chunks=rb // ck, t_dec=t_dec)
    return pl.pallas_call(
        kern,
        grid=(RET_HEADS, B, nblk),
        in_specs=[pl.BlockSpec((rb, K), lambda h, b, c: (blk(h, b, c), 0)),
                  pl.BlockSpec((RET_DEC_ROWS, K), lambda h, b, c: (r0 + blk(h, b, c), 0)),
                  wcol(RET_DK, 0), wcol(RET_DK, RET_HEADS), wcol(RET_DV, RET_HEADS), wcol(RET_DV, 2 * RET_HEADS),
                  pl.BlockSpec((rb, RET_DK // 2), lambda h, b, c: (blk(h, b, c), 0)),
                  pl.BlockSpec((rb, RET_DK // 2), lambda h, b, c: (blk(h, b, c), 0)),
                  pl.BlockSpec((RET_DEC_ROWS, RET_DK // 2), lambda h, b, c: (r0 + blk(h, b, c), 0)),
                  pl.BlockSpec((RET_DEC_ROWS, RET_DK // 2), lambda h, b, c: (r0 + blk(h, b, c), 0)),
                  per_head(ck, ck), per_head(ck, 1), per_head(ck, 1), per_head(1, 1),
                  per_head(RET_DEC_ROWS, RET_DEC_ROWS), per_head(RET_DEC_ROWS, 1), per_head(RET_DEC_ROWS, 1),
                  per_head(1, 1), s_spec] + [pl.BlockSpec(memory_space=pl.ANY)] * len(aliased),
        out_specs=[pl.BlockSpec((rb, RET_DV), lambda h, b, c: (blk(h, b, c), h)),
                   pl.BlockSpec((None, None, RET_DK, RET_DV), lambda h, b, c: (b, h, 0, 0)),
                   pl.BlockSpec((RET_DEC_ROWS, RET_DV), lambda h, b, c: (blk(h, b, c), h)),
                   s_spec],
        out_shape=[jax.ShapeDtypeStruct((TP, RET_HEADS * RET_DV), BF16),
                   jax.ShapeDtypeStruct((B, RET_HEADS, RET_DK, RET_DV), F32),
                   jax.ShapeDtypeStruct((TD, RET_HEADS * RET_DV), BF16),
                   jax.ShapeDtypeStruct(state.shape, state.dtype)],
        scratch_shapes=[pltpu.VMEM((RET_DK, RET_DV), F32), pltpu.VMEM((K, 2 * RET_DK + 2 * RET_DV), BF16)],
        input_output_aliases={19: 3} if aliased else {},
        compiler_params=_params(3),
        name=name,
    )(xn, xn, w, w, w, w, tabs["ret_cos_rows"], tabs["ret_sin_rows"], tabs["ret_cos_rows"], tabs["ret_sin_rows"],
      tabs["dmask_p"], tabs["qdec_p"], tabs["kdec_p"], tabs["cdec_p"],
      tabs["dmask_d"], tabs["qdec_d"], tabs["kdec_d"], tabs["cdec_d"], state, *aliased)


def _retention(proj, state, layer, snew_all, tabs, B, L, DB, t_dec, name):
    TP, TD = B * L, DB * t_dec
    nseq = RET_DEC_ROWS // t_dec
    steps = DB // nseq
    assert DB % nseq == 0 and steps % B == 0, (DB, nseq, B)
    nblk = steps // B
    rb = L // nblk
    ck = tabs["chunk_p"]
    assert L % nblk == 0 and rb % ck == 0, (L, nblk, ck)
    r0 = TP // RET_DEC_ROWS
    kcol = vcol = RET_HEADS
    gcol = 2 * RET_HEADS
    prow = lambda b, h, c: b * nblk + c
    drow = lambda b, h, c: b * nblk + c
    dec = lambda width, col0: pl.BlockSpec((RET_DEC_ROWS, width), lambda b, h, c: (r0 + drow(b, h, c), col0 + h))
    per_head = lambda *shape: pl.BlockSpec((None,) + shape, lambda b, h, c: (h,) + (0,) * len(shape))
    s_spec = pl.BlockSpec((None, nseq, None, RET_DK, RET_DV), lambda b, h, c: (layer, drow(b, h, c), h, 0, 0))
    aliased = [] if snew_all is None else [snew_all]
    kern = functools.partial(_retention_kernel, chunk_rows=ck, n_chunks=rb // ck, t_dec=t_dec)
    return pl.pallas_call(
        kern,
        grid=(B, RET_HEADS, nblk),
        in_specs=[pl.BlockSpec((rb, RET_DK), lambda b, h, c: (prow(b, h, c), h)),
                  pl.BlockSpec((rb, RET_DK), lambda b, h, c: (prow(b, h, c), kcol + h)),
                  pl.BlockSpec((rb, RET_DV), lambda b, h, c: (prow(b, h, c), vcol + h)),
                  pl.BlockSpec((rb, RET_DV), lambda b, h, c: (prow(b, h, c), gcol + h)),
                  per_head(ck, ck), per_head(ck, 1), per_head(ck, 1), per_head(1, 1),
                  dec(RET_DK, 0), dec(RET_DK, kcol), dec(RET_DV, vcol), dec(RET_DV, gcol),
                  per_head(RET_DEC_ROWS, RET_DEC_ROWS), per_head(RET_DEC_ROWS, 1), per_head(RET_DEC_ROWS, 1),
                  per_head(1, 1), s_spec] + [pl.BlockSpec(memory_space=pl.ANY)] * len(aliased),
        out_specs=[pl.BlockSpec((rb, RET_DV), lambda b, h, c: (prow(b, h, c), h)),
                   pl.BlockSpec((None, None, RET_DK, RET_DV), lambda b, h, c: (b, h, 0, 0)),
                   pl.BlockSpec((RET_DEC_ROWS, RET_DV), lambda b, h, c: (drow(b, h, c), h)),
                   s_spec],
        out_shape=[jax.ShapeDtypeStruct((TP, RET_HEADS * RET_DV), BF16),
                   jax.ShapeDtypeStruct((B, RET_HEADS, RET_DK, RET_DV), F32),
                   jax.ShapeDtypeStruct((TD, RET_HEADS * RET_DV), BF16),
                   jax.ShapeDtypeStruct(state.shape, state.dtype)],
        scratch_shapes=[pltpu.VMEM((RET_DK, RET_DV), F32)],
        input_output_aliases={17: 3} if aliased else {},
        compiler_params=_params(3),
        name=name,
    )(proj, proj, proj, proj, tabs["dmask_p"], tabs["qdec_p"], tabs["kdec_p"], tabs["cdec_p"],
      proj, proj, proj, proj, tabs["dmask_d"], tabs["qdec_d"], tabs["kdec_d"], tabs["cdec_d"], state, *aliased)


def _headnorm_rope(x, w, cos2, sin2, bd):
    width = x.shape[1]
    sq = (x * x).astype(BF16)
    parts = [_dot(sq[:, 256 * c:256 * (c + 1)], bd) for c in range(width // 256)]
    ss = parts[0] if len(parts) == 1 else jnp.concatenate(parts, axis=1)
    y = (x * lax.rsqrt(ss * (1.0 / HEAD_DIM) + EPS)) * w
    lane = lax.broadcasted_iota(jnp.int32, y.shape, 1)
    first_half = (lane & (HEAD_DIM - 1)) < HEAD_DIM // 2
    rot = jnp.where(first_half, pltpu.roll(y, width - HEAD_DIM // 2, 1), pltpu.roll(y, HEAD_DIM // 2, 1))
    nrep = width // 128
    cosw = jnp.concatenate([cos2] * nrep, axis=1)
    sinw = jnp.concatenate([sin2] * nrep, axis=1)
    return y * cosw + rot * sinw


def _kv_proj_kernel(x_ref, wkv_ref, w_ref, cos_ref, sin_ref, bd_ref, o_ref, v_ref, lo_ref, hi_ref, wb_ref):
    @pl.when(pl.program_id(0) == 0)
    def _():
        wb_ref[...] = wkv_ref[...].astype(BF16)

    kw = N_KV_HEADS * HEAD_DIM
    sub = SUB_ROWS
    low = lax.broadcasted_iota(jnp.int32, (sub, 128), 1) < HEAD_DIM
    zero = jnp.zeros((sub, 128), F32)
    for r0 in range(0, x_ref.shape[0], sub):
        rows = slice(r0, r0 + sub)
        kv = _dot(x_ref[rows, :], wb_ref[...])
        v_ref[rows, :] = kv[:, kw:]
        kn = _headnorm_rope(kv[:, :kw], w_ref[...], cos_ref[rows, :], sin_ref[rows, :], bd_ref[...])
        o_ref[rows, :] = kn
        for c in range(N_KV_HEADS // 2):
            tile = kn[:, 128 * c:128 * (c + 1)]
            swapped = pltpu.roll(tile, HEAD_DIM, 1)
            even, odd = slice(256 * c, 256 * c + 128), slice(256 * c + 128, 256 * (c + 1))
            lo_ref[rows, even] = jnp.where(low, tile, zero).astype(lo_ref.dtype)
            hi_ref[rows, even] = jnp.where(low, zero, swapped).astype(hi_ref.dtype)
            lo_ref[rows, odd] = jnp.where(low, swapped, zero).astype(lo_ref.dtype)
            hi_ref[rows, odd] = jnp.where(low, zero, tile).astype(hi_ref.dtype)


def _table_block(i, n_prompt_blocks, blocks_per_seq):
    return jnp.where(i < n_prompt_blocks, i % blocks_per_seq, blocks_per_seq + i - n_prompt_blocks)


def _kv_proj(xn, w_kv, k_norm_w, tabs, L, TP, name):
    T, K = xn.shape
    kw = N_KV_HEADS * HEAD_DIM
    npb, bps = TP // ROW_BLOCK, L // ROW_BLOCK
    tab = lambda i: (_table_block(i, npb, bps), 0)
    kv_spec = pl.BlockSpec((ROW_BLOCK, kw), lambda i: (i, 0))
    pad_spec = pl.BlockSpec((ROW_BLOCK, 128 * N_KV_HEADS), lambda i: (i, 0))
    pad_shape = jax.ShapeDtypeStruct((T, 128 * N_KV_HEADS), BF16)
    return pl.pallas_call(
        _kv_proj_kernel,
        grid=(T // ROW_BLOCK,),
        in_specs=[pl.BlockSpec((ROW_BLOCK, K), lambda i: (i, 0)),
                  pl.BlockSpec((K, 2 * kw), lambda i: (0, 0)),
                  pl.BlockSpec((1, kw), lambda i: (0, 0)),
                  pl.BlockSpec((ROW_BLOCK, 128), tab),
                  pl.BlockSpec((ROW_BLOCK, 128), tab),
                  pl.BlockSpec((256, 256), lambda i: (0, 0))],
        out_specs=[kv_spec, kv_spec, pad_spec, pad_spec],
        out_shape=[jax.ShapeDtypeStruct((T, kw), F32), jax.ShapeDtypeStruct((T, kw), F32), pad_shape, pad_shape],
        scratch_shapes=[pltpu.VMEM((K, 2 * kw), BF16)],
        compiler_params=_params(1),
        name=name,
    )(xn, w_kv, k_norm_w, tabs["att_cos"], tabs["att_sin"], tabs["bd"])


def _q_proj_kernel(x_ref, w_ref, qw_ref, cos_ref, sin_ref, bd_ref, o_ref, wb_ref):
    @pl.when(pl.program_id(0) == 0)
    def _():
        wb_ref[...] = w_ref[...].astype(BF16)

    bd = bd_ref[...]
    for r0 in range(0, x_ref.shape[0], 128):
        rows = slice(r0, r0 + 128)
        q = _dot(x_ref[rows, :], wb_ref[...])
        cos2, sin2 = cos_ref[rows, :], sin_ref[rows, :]
        for c0 in range(0, q.shape[1], 256):
            cols = slice(c0, c0 + 256)
            qc = _headnorm_rope(q[:, cols], qw_ref[:, cols], cos2, sin2, bd)
            o_ref[rows, cols] = (qc * (HEAD_DIM ** -0.5)).astype(o_ref.dtype)


def _q_proj(xn, w, q_norm_w, layer, tabs, L, TP, name):
    T, K = xn.shape
    N = w.shape[-1]
    npb, bps = TP // ROW_BLOCK, L // ROW_BLOCK
    tab = lambda i: (_table_block(i, npb, bps), 0)
    return pl.pallas_call(
        _q_proj_kernel,
        grid=(T // ROW_BLOCK,),
        in_specs=[pl.BlockSpec((ROW_BLOCK, K), lambda i: (i, 0)),
                  pl.BlockSpec((None, K, N), lambda i: (layer, 0, 0)),
                  pl.BlockSpec((None, 1, N), lambda i: (layer, 0, 0)),
                  pl.BlockSpec((ROW_BLOCK, 128), tab),
                  pl.BlockSpec((ROW_BLOCK, 128), tab),
                  pl.BlockSpec((256, 256), lambda i: (0, 0))],
        out_specs=pl.BlockSpec((ROW_BLOCK, N), lambda i: (i, 0)),
        out_shape=jax.ShapeDtypeStruct((T, N), BF16),
        scratch_shapes=[pltpu.VMEM((K, N), BF16)],
        compiler_params=_params(1),
        name=name,
    )(xn, w, q_norm_w, tabs["att_cos"], tabs["att_sin"], tabs["bd"])


def _sink_column(sink_ref, layer, g, rows, rows_per_head):
    head_in_group = lax.broadcasted_iota(jnp.int32, (rows, 1), 0) // rows_per_head
    col = jnp.full((rows, 1), sink_ref[layer, g * GROUP], F32)
    for r in range(1, GROUP):
        col = jnp.where(head_in_group == r, sink_ref[layer, g * GROUP + r], col)
    return col


def _attn_prompt_kernel(sink_ref, q_ref, lop_ref, loc_ref, hip_ref, hic_ref, vp_ref, vc_ref, o_ref, *,
                        layer, n_qblk):
    first = pl.program_id(1) == 0
    qb = q_ref[...]
    klo = jnp.concatenate([lop_ref[...], loc_ref[...]], axis=0)
    khi = jnp.concatenate([hip_ref[...], hic_ref[...]], axis=0)
    v_t = jnp.concatenate([vp_ref[...], vc_ref[...]], axis=0).T.astype(BF16)
    key = lax.broadcasted_iota(jnp.int32, (WINDOW, WINDOW), 0)
    qi = lax.broadcasted_iota(jnp.int32, (WINDOW, WINDOW), 1)
    own = key <= qi
    scores = {}
    for t in range(n_qblk):
        qrows = slice(WINDOW * t, WINDOW * (t + 1))
        krows = slice(WINDOW * t, WINDOW * (t + 2))
        for g in range(N_KV_HEADS):
            gl = slice(128 * g, 128 * (g + 1))
            kk = jnp.concatenate([klo[krows, gl], khi[krows, gl]], axis=0)
            xq = jnp.concatenate([qb[qrows, 256 * g:256 * g + 128], qb[qrows, 256 * g + 128:256 * (g + 1)]],
                                 axis=0)
            scores[t, g] = _dot_nt(kk, xq)
    for t in range(n_qblk):
        qrows = slice(WINDOW * t, WINDOW * (t + 1))
        krows = slice(WINDOW * t, WINDOW * (t + 2))
        pieces = []
        for g in range(N_KV_HEADS):
            s4 = scores[t, g]
            v_g = v_t[HEAD_DIM * g:HEAD_DIM * (g + 1), krows]
            for pair in range(GROUP // 2):
                p2s, invs = [], []
                for parity in range(2):
                    sk = sink_ref[layer, g * GROUP + 2 * pair + parity]
                    blk = s4[2 * WINDOW * parity:2 * WINDOW * (parity + 1), WINDOW * pair:WINDOW * (pair + 1)]
                    s_prev = blk[:WINDOW]
                    if t == 0:
                        s_prev = jnp.where(first, NEG_INF, s_prev)
                    s = jnp.where(own, blk[WINDOW:], s_prev)
                    m = jnp.maximum(jnp.max(s, axis=0, keepdims=True), sk)
                    p = jnp.exp(s - m)
                    invs.append(1.0 / (jnp.sum(p, axis=0, keepdims=True) + jnp.exp(sk - m)))
                    p2s.append(jnp.concatenate([jnp.where(own, 0.0, p), jnp.where(own, p, 0.0)], axis=0))
                p2 = jnp.concatenate(p2s, axis=1).astype(BF16)
                o_t = _dot(v_g, p2) * jnp.concatenate(invs, axis=1)
                pieces += [o_t[:, :WINDOW], o_t[:, WINDOW:]]
        o_ref[qrows, :] = jnp.concatenate(pieces, axis=0).T.astype(o_ref.dtype)


def _attn_prompt(q, klo, khi, v, sinks, layer, B, L, name):
    nb = L // WINDOW
    nq = ATTN_Q_BLOCKS if nb % ATTN_Q_BLOCKS == 0 else 1
    ns = nb // nq
    kw = N_KV_HEADS * HEAD_DIM
    kpad = 128 * N_KV_HEADS
    cur = lambda b, i: b * ns + i
    prev = lambda b, i: b * nb + jnp.maximum(i * nq - 1, 0)
    kern = functools.partial(_attn_prompt_kernel, layer=layer, n_qblk=nq)
    return pl.pallas_call(
        kern,
        grid=(B, ns),
        in_specs=[pl.BlockSpec(memory_space=pltpu.SMEM),
                  pl.BlockSpec((nq * WINDOW, D_MODEL), lambda b, i: (cur(b, i), 0)),
                  pl.BlockSpec((WINDOW, kpad), lambda b, i: (prev(b, i), 0)),
                  pl.BlockSpec((nq * WINDOW, kpad), lambda b, i: (cur(b, i), 0)),
                  pl.BlockSpec((WINDOW, kpad), lambda b, i: (prev(b, i), 0)),
                  pl.BlockSpec((nq * WINDOW, kpad), lambda b, i: (cur(b, i), 0)),
                  pl.BlockSpec((WINDOW, kw), lambda b, i: (prev(b, i), 0)),
                  pl.BlockSpec((nq * WINDOW, kw), lambda b, i: (cur(b, i), 0))],
        out_specs=pl.BlockSpec((nq * WINDOW, D_MODEL), lambda b, i: (cur(b, i), 0)),
        out_shape=jax.ShapeDtypeStruct((B * L, D_MODEL), BF16),
        compiler_params=_params(2),
        name=name,
    )(sinks, q, klo, klo, khi, khi, v, v)


def _attn_decode_kernel(sink_ref, q_ref, kn_ref, vn_ref, ck_ref, cv_ref, o_ref, *, layer, t_dec):
    nseq = DEC_ROWS // t_dec
    wc = ck_ref.shape[1]
    qb = q_ref[...]
    knew = kn_ref[...].astype(BF16)
    vnew = vn_ref[...].astype(BF16)
    rows = GROUP * DEC_ROWS
    row = lax.broadcasted_iota(jnp.int32, (rows, 1), 0)
    row_seq = (row % DEC_ROWS) // t_dec
    row_tok = row % t_dec
    jold = lax.broadcasted_iota(jnp.int32, (rows, wc), 1)
    vis_old = (jold > row_tok + (wc - WINDOW)) & (jold <= row_tok + wc)
    cnew = lax.broadcasted_iota(jnp.int32, (rows, DEC_ROWS), 1)
    vis_new = (cnew // t_dec == row_seq) & (cnew % t_dec <= row_tok)
    scores = []
    for g in range(N_KV_HEADS):
        heads = [g * GROUP + r for r in range(GROUP)]
        hs = slice(HEAD_DIM * g, HEAD_DIM * (g + 1))
        qs = jnp.concatenate([qb[:, HEAD_DIM * h:HEAD_DIM * (h + 1)] for h in heads], axis=0)
        s_old = jnp.zeros((rows, wc), F32)
        for bi in range(nseq):
            kc = ck_ref[bi][:, hs].astype(BF16)
            s_old = s_old + jnp.where(row_seq == bi, _dot_nt(qs, kc), 0.0)
        scores.append((s_old, _dot_nt(qs, knew[:, hs])))
    for g in range(N_KV_HEADS):
        heads = [g * GROUP + r for r in range(GROUP)]
        hs = slice(HEAD_DIM * g, HEAD_DIM * (g + 1))
        s_old = jnp.where(vis_old, scores[g][0], NEG_INF)
        s_new = jnp.where(vis_new, scores[g][1], NEG_INF)
        sk = _sink_column(sink_ref, layer, g, rows, DEC_ROWS)
        m = jnp.maximum(jnp.maximum(jnp.max(s_old, axis=-1, keepdims=True),
                                    jnp.max(s_new, axis=-1, keepdims=True)), sk)
        p_old = jnp.exp(s_old - m)
        p_new = jnp.exp(s_new - m)
        denom = (jnp.sum(p_old, axis=-1, keepdims=True) + jnp.sum(p_new, axis=-1, keepdims=True)
                 + jnp.exp(sk - m))
        inv = 1.0 / denom
        o = _dot((p_new * inv).astype(BF16), vnew[:, hs])
        pn_old = p_old * inv
        for bi in range(nseq):
            vc = cv_ref[bi][:, hs].astype(BF16)
            o = o + _dot(jnp.where(row_seq == bi, pn_old, 0.0).astype(BF16), vc)
        for r, h in enumerate(heads):
            o_ref[:, HEAD_DIM * h:HEAD_DIM * (h + 1)] = o[DEC_ROWS * r:DEC_ROWS * (r + 1)].astype(o_ref.dtype)


def _attn_decode(q, kn, v, cache_k, cache_v, sinks, layer, TP, DB, t_dec, name):
    nseq = DEC_ROWS // t_dec
    r0 = TP // DEC_ROWS
    kw = N_KV_HEADS * HEAD_DIM
    wc = cache_k.shape[1]
    kern = functools.partial(_attn_decode_kernel, layer=layer, t_dec=t_dec)
    return pl.pallas_call(
        kern,
        grid=(DB // nseq,),
        in_specs=[pl.BlockSpec(memory_space=pltpu.SMEM),
                  pl.BlockSpec((DEC_ROWS, D_MODEL), lambda i: (r0 + i, 0)),
                  pl.BlockSpec((DEC_ROWS, kw), lambda i: (r0 + i, 0)),
                  pl.BlockSpec((DEC_ROWS, kw), lambda i: (r0 + i, 0)),
                  pl.BlockSpec((nseq, wc, kw), lambda i: (i, 0, 0)),
                  pl.BlockSpec((nseq, wc, kw), lambda i: (i, 0, 0))],
        out_specs=pl.BlockSpec((DEC_ROWS, D_MODEL), lambda i: (i, 0)),
        out_shape=jax.ShapeDtypeStruct((DB * t_dec, D_MODEL), BF16),
        compiler_params=_params(1),
        name=name,
    )(sinks, q, kn, v, cache_k, cache_v)


def _rope_cos_sin(pos, half):
    inv = 1.0 / (ROPE_THETA ** (np.arange(half, dtype=np.float64) / half))
    ang = pos.astype(np.float64)[:, None] * inv[None, :]
    return np.cos(ang), np.sin(ang)


def _decay_tables(chunk, reps):
    lg = np.log(1.0 - 2.0 ** (-5.0 - np.arange(RET_HEADS, dtype=np.float64)))
    idx = np.arange(chunk, dtype=np.float64)
    rel = idx[:, None] - idx[None, :]
    dmask = np.where(rel >= 0, np.exp(lg[:, None, None] * np.maximum(rel, 0.0)), 0.0)
    qdec = np.exp(lg[:, None] * (idx + 1.0))[:, :, None]
    kdec = np.exp(lg[:, None] * (chunk - 1.0 - idx))[:, :, None]
    cdec = np.exp(lg * chunk)[:, None, None]
    if reps > 1:
        dmask = np.einsum("ab,hij->haibj", np.eye(reps), dmask).reshape(RET_HEADS, reps * chunk, reps * chunk)
        qdec = np.tile(qdec, (1, reps, 1))
        kdec = np.tile(kdec, (1, reps, 1))
    return dmask, qdec, kdec, cdec


def _tables(B, L, TD, t_dec, chunk_p):
    pos = np.concatenate([np.arange(L), PAST_LEN + np.arange(TD) % t_dec])
    ret_cos, ret_sin = _rope_cos_sin(pos, RET_DK // 2)
    per_row = lambda t: np.concatenate([np.tile(t[:L], (B, 1)), t[L:]], axis=0)
    c, s = _rope_cos_sin(pos, HEAD_DIM // 2)
    att_cos = np.concatenate([c, c, c, c], axis=1)
    att_sin = np.concatenate([-s, s, -s, s], axis=1)
    dmask_p, qdec_p, kdec_p, cdec_p = _decay_tables(chunk_p, 1)
    dmask_d, qdec_d, kdec_d, cdec_d = _decay_tables(math.gcd(t_dec, RET_CHUNK), RET_DEC_ROWS // t_dec)
    head_of_lane = np.arange(256) // HEAD_DIM
    f32 = dict(ret_cos_rows=per_row(ret_cos), ret_sin_rows=per_row(ret_sin), att_cos=att_cos, att_sin=att_sin,
               dmask_p=dmask_p, qdec_p=qdec_p, kdec_p=kdec_p, cdec_p=cdec_p,
               dmask_d=dmask_d, qdec_d=qdec_d, kdec_d=kdec_d, cdec_d=cdec_d)
    tabs = {name: jnp.asarray(t.astype(np.float32)) for name, t in f32.items()}
    tabs["bd"] = jnp.asarray((head_of_lane[:, None] == head_of_lane[None, :]).astype(np.float32), dtype=BF16)
    tabs["chunk_p"] = chunk_p
    return tabs


def kernel(x_prompt, x_sample, state_ret, cache_k, cache_v, ln_ret, w_ret_in, w_ret_out, ln_ffn, w_ffn_in,
           w_ffn_out, ln_kv, w_kv, k_norm, ln_attn, w_q, q_norm, sinks, w_o):
    B, L, D = x_prompt.shape
    DB, t_dec, _ = x_sample.shape
    n_ret = w_ret_in.shape[0]
    n_attn = w_q.shape[0]
    TP, TD = B * L, DB * t_dec
    T = TP + TD
    wc = cache_k.shape[1]
    kw = N_KV_HEADS * HEAD_DIM
    assert D == D_MODEL and L % RET_CHUNK == 0 and L % ROW_BLOCK == 0 and TD % ROW_BLOCK == 0
    assert t_dec == math.gcd(t_dec, RET_CHUNK) and DEC_ROWS % t_dec == 0 and wc == WINDOW and PAST_LEN >= WINDOW

    tabs = _tables(B, L, TD, t_dec, RET_CHUNK_PROMPT if L % RET_CHUNK_PROMPT == 0 else RET_CHUNK)
    ck = cache_k.reshape(DB, wc, kw)
    cv = cache_v.reshape(DB, wc, kw)
    q_norm_w = jnp.tile(q_norm, (1, N_HEADS))[:, None, :]
    k_norm_w = jnp.tile(k_norm[None, :], (1, N_KV_HEADS))
    ln_ret, ln_ffn, ln_attn = ln_ret[:, None, :], ln_ffn[:, None, :], ln_attn[:, None, :]

    ln_kv3 = ln_kv[None, None, :]
    prompt_states = []
    dec_states = None
    x = (x_prompt.reshape(TP, D), x_sample.reshape(TD, D))
    xn = _embed(*x, (ln_ret, 0), "embed")
    for l in range(n_ret):
        go_p, s_p, go_d, dec_states = _ret_layer(xn, w_ret_in, state_ret, l, dec_states, tabs, B, L, DB, t_dec,
                                                 f"retention_{l}")
        prompt_states.append(s_p)
        x, xn = _matmul_residual((go_p, go_d), w_ret_out, l, x, [(ln_ffn, l)], f"ret_out_{l}", TP)
        h = _swiglu(xn, w_ffn_in, l, f"ffn_in_{l}")
        if l + 1 < n_ret:
            x, xn = _matmul_residual(h, w_ffn_out, l, x, [(ln_ret, l + 1)], f"ffn_out_{l}", TP)
        else:
            x, xn_kv, xn = _matmul_residual(h, w_ffn_out, l, x, [(ln_kv3, 0), (ln_attn, 0)], f"ffn_out_{l}", TP)

    kn, v, klo, khi = _kv_proj(xn_kv, w_kv, k_norm_w, tabs, L, TP, "kv_proj")
    for j in range(n_attn):
        layer = n_ret + j
        q = _q_proj(xn, w_q, q_norm_w, j, tabs, L, TP, f"q_proj_{j}")
        ao_p = _attn_prompt(q, klo, khi, v, sinks, j, B, L, f"attn_prompt_{j}")
        ao_d = _attn_decode(q, kn, v, ck, cv, sinks, j, TP, DB, t_dec, f"attn_decode_{j}")
        x, xn = _matmul_residual((ao_p, ao_d), w_o, j, x, [(ln_ffn, layer)], f"attn_out_{j}", TP)
        h = _swiglu(xn, w_ffn_in, layer, f"ffn_in_{layer}")
        if j + 1 < n_attn:
            x, xn = _matmul_residual(h, w_ffn_out, layer, x, [(ln_attn, j + 1)], f"ffn_out_{layer}", TP)
        else:
            y_p, y_s = _matmul_residual(h, w_ffn_out, layer, x, [], f"ffn_out_{layer}", TP, split_out=True)

    y_prompt = y_p.reshape(B, L, D)
    y_sample = y_s.reshape(DB, t_dec, D)
    state_prompt = jnp.stack(prompt_states)
    w_keep = min(WINDOW, L)
    tail_rows = lambda t: t[:TP].reshape(B, L, kw)[:, L - w_keep:].reshape(B, w_keep, N_KV_HEADS, HEAD_DIM)
    kn_p, v_p = tail_rows(kn), tail_rows(v)
    kn_d = kn[TP:].reshape(DB, t_dec, N_KV_HEADS, HEAD_DIM)
    v_d = v[TP:].reshape(DB, t_dec, N_KV_HEADS, HEAD_DIM)
    cache_k_sample = jnp.concatenate([cache_k, kn_d], axis=1)[:, -wc:]
    cache_v_sample = jnp.concatenate([cache_v, v_d], axis=1)[:, -wc:]
    return (y_prompt, y_sample, state_prompt, dec_states, kn_p, v_p,
            cache_k_sample, cache_v_sample)
```

```python
import functools
import math

import numpy as np
import jax
import jax.numpy as jnp
from jax import lax
from jax.experimental import pallas as pl
from jax.experimental.pallas import tpu as pltpu

D_MODEL = 1024
PAST_LEN = 8192
RET_HEADS = 4
RET_DK = 256
RET_DV = 512
RET_CHUNK = 128
RET_CHUNK_PROMPT = 256
N_HEADS = 16
N_KV_HEADS = 4
HEAD_DIM = 64
GROUP = N_HEADS // N_KV_HEADS
WINDOW = 128
D_FF = 2816
ROPE_THETA = 10000.0
EPS = 1e-6
NEG_INF = -1e30

F32 = jnp.float32
BF16 = jnp.bfloat16

V7X_VMEM_LIMIT_BYTES = 56 * 1024 * 1024
DEC_ROWS = 16
RET_DEC_ROWS = 32
ROW_BLOCK = 512
SUB_ROWS = 256
ATTN_Q_BLOCKS = 4


def _dot(a, b):
    return jnp.dot(a, b, preferred_element_type=F32)


def _dot_nt(a, b):
    return lax.dot_general(a, b, (((1,), (1,)), ((), ())), preferred_element_type=F32)


def _dot_tn(a, b):
    return lax.dot_general(a, b, (((0,), (0,)), ((), ())), preferred_element_type=F32)


def _params(n_axes):
    return pltpu.CompilerParams(dimension_semantics=("arbitrary",) * n_axes,
                                vmem_limit_bytes=V7X_VMEM_LIMIT_BYTES)


def _row_tile(rows, prefs):
    for t in prefs:
        if rows % t == 0:
            return t
    raise ValueError(f"no row tile for {rows}")


def _rms_normed(x, gain_refs):
    ms = jnp.mean(x * x, axis=-1, keepdims=True)
    xh = x * lax.rsqrt(ms + EPS)
    return [(xh * g_ref[...]).astype(BF16) for g_ref in gain_refs]


def _embed_kernel(xp_ref, xs_ref, g_ref, xn_ref, *, n_prompt_blocks):
    i = pl.program_id(0)

    @pl.when(i < n_prompt_blocks)
    def _():
        xn_ref[...] = _rms_normed(xp_ref[...], [g_ref])[0]

    @pl.when(i >= n_prompt_blocks)
    def _():
        xn_ref[...] = _rms_normed(xs_ref[...], [g_ref])[0]


def _embed(xp, xs, gain, name):
    TP, D = xp.shape
    TD = xs.shape[0]
    npb = TP // ROW_BLOCK
    T = TP + TD
    g_arr, g_idx = gain
    blk = lambda fn: pl.BlockSpec((ROW_BLOCK, D), fn)
    return pl.pallas_call(
        functools.partial(_embed_kernel, n_prompt_blocks=npb),
        grid=(T // ROW_BLOCK,),
        in_specs=[blk(lambda i: (jnp.minimum(i, npb - 1), 0)),
                  blk(lambda i: (jnp.maximum(i - npb, 0), 0)),
                  pl.BlockSpec((None, 1, D), lambda i: (g_idx, 0, 0))],
        out_specs=blk(lambda i: (i, 0)),
        out_shape=jax.ShapeDtypeStruct((T, D), BF16),
        compiler_params=_params(1),
        name=name,
    )(xp, xs, g_arr)


def _sub_rows(tm):
    return max(r for r in range(16, 3 * SUB_ROWS + 1, 16) if tm % r == 0)


def _swiglu_kernel(x_ref, wa_ref, wb_ref, o_ref):
    wa = wa_ref[...].astype(BF16)
    wb = wb_ref[...].astype(BF16)
    sub = _sub_rows(x_ref.shape[0])
    for r0 in range(0, x_ref.shape[0], sub):
        rows = slice(r0, r0 + sub)
        a = _dot(x_ref[rows, :], wa)
        b = _dot(x_ref[rows, :], wb)
        o_ref[rows, :] = ((a * jax.nn.sigmoid(a)) * b).astype(o_ref.dtype)


def _swiglu(xn, w, layer, name):
    T, K = xn.shape
    tn = 256
    nb = D_FF // tn
    tm = _row_tile(T, (5632, 2816, 1536, 1024, 512))
    return pl.pallas_call(
        _swiglu_kernel,
        grid=(T // tm, nb),
        in_specs=[pl.BlockSpec((tm, K), lambda i, j: (i, 0)),
                  pl.BlockSpec((None, K, tn), lambda i, j: (layer, 0, j)),
                  pl.BlockSpec((None, K, tn), lambda i, j: (layer, 0, nb + j))],
        out_specs=pl.BlockSpec((tm, tn), lambda i, j: (i, j)),
        out_shape=jax.ShapeDtypeStruct((T, D_FF), BF16),
        compiler_params=_params(2),
        name=name,
    )(xn, w, w)


def _mm_res_kernel(*refs, n_a, n_r, n_gains, n_prompt_blocks, split_out, n_alias=0):
    a_refs = refs[:n_a]
    w_ref = refs[n_a]
    r_refs = refs[n_a + 1:n_a + 1 + n_r]
    rest = refs[n_a + 1 + n_r:]
    gain_refs = rest[:n_gains]
    out_refs = rest[n_gains + n_alias:-1]
    wb_ref = rest[-1]
    i = pl.program_id(0)

    @pl.when(i == 0)
    def _():
        wb_ref[...] = w_ref[...].astype(BF16)

    def step(a_ref, r_ref, dst):
        for r0 in range(0, a_ref.shape[0], SUB_ROWS):
            rows = slice(r0, r0 + SUB_ROWS)
            x = r_ref[rows, :] + _dot(a_ref[rows, :], wb_ref[...])
            out_refs[dst][rows, :] = x
            if not split_out:
                for o_ref, xn in zip(out_refs[1:], _rms_normed(x, gain_refs)):
                    o_ref[rows, :] = xn

    if n_a == 1 and n_r == 1 and not split_out:
        step(a_refs[0], r_refs[0], 0)
    else:
        @pl.when(i < n_prompt_blocks)
        def _():
            step(a_refs[0], r_refs[0], 0)

        @pl.when(i >= n_prompt_blocks)
        def _():
            step(a_refs[-1], r_refs[-1], 1 if split_out else 0)


def _matmul_residual(a, w, layer, res, gains, name, TP, split_out=False):
    a = a if isinstance(a, (tuple, list)) else (a,)
    res = res if isinstance(res, (tuple, list)) else (res,)
    T = sum(r.shape[0] for r in res)
    N = res[0].shape[1]
    K = a[0].shape[1]
    tm = ROW_BLOCK
    npb = TP // tm
    head = lambda i: (jnp.minimum(i, npb - 1), 0)
    tail = lambda i: (jnp.maximum(i - npb, 0), 0)
    whole = lambda i: (i, 0)
    rows_of = lambda arrs, width: [pl.BlockSpec((tm, width), fn)
                                   for fn in ((whole,) if len(arrs) == 1 else (head, tail))]
    in_specs = rows_of(a, K) + [pl.BlockSpec((None, K, N), lambda i: (layer, 0, 0))] + rows_of(res, N)
    in_specs += [pl.BlockSpec((None, 1, N), lambda i, idx=idx: (idx, 0, 0)) for _, idx in gains]
    if split_out:
        assert not gains
        out_specs = [pl.BlockSpec((tm, N), head), pl.BlockSpec((tm, N), tail)]
        out_shape = [jax.ShapeDtypeStruct((TP, N), F32), jax.ShapeDtypeStruct((T - TP, N), F32)]
    else:
        out_specs = [pl.BlockSpec((tm, N), whole)] * (1 + len(gains))
        out_shape = [jax.ShapeDtypeStruct((T, N), F32)] + [jax.ShapeDtypeStruct((T, N), BF16)] * len(gains)
    aliases = {len(a) + 1: 0} if len(res) == 1 and not split_out else {}
    return pl.pallas_call(
        functools.partial(_mm_res_kernel, n_a=len(a), n_r=len(res), n_gains=len(gains), n_prompt_blocks=npb,
                          split_out=split_out),
        grid=(T // tm,),
        in_specs=in_specs,
        out_specs=out_specs,
        out_shape=out_shape,
        scratch_shapes=[pltpu.VMEM((K, N), BF16)],
        input_output_aliases=aliases,
        compiler_params=_params(1),
        name=name,
    )(*a, w, *res, *[g for g, _ in gains])


def _residual_tail_rows(a_tail, w, layer, res, gain, x_buf, xn_buf, TP, name):
    T, N = res.shape
    K = a_tail.shape[1]
    tm = ROW_BLOCK
    npb = TP // tm
    g_arr, g_idx = gain
    tail = pl.BlockSpec((tm, N), lambda i: (npb + i, 0))
    return pl.pallas_call(
        functools.partial(_mm_res_kernel, n_a=1, n_r=1, n_gains=1, n_prompt_blocks=npb, split_out=False,
                          n_alias=2),
        grid=((T - TP) // tm,),
        in_specs=[pl.BlockSpec((tm, K), lambda i: (i, 0)),
                  pl.BlockSpec((None, K, N), lambda i: (layer, 0, 0)),
                  tail,
                  pl.BlockSpec((None, 1, N), lambda i: (g_idx, 0, 0)),
                  pl.BlockSpec(memory_space=pl.ANY), pl.BlockSpec(memory_space=pl.ANY)],
        out_specs=[tail, tail],
        out_shape=[jax.ShapeDtypeStruct((T, N), F32), jax.ShapeDtypeStruct((T, N), BF16)],
        scratch_shapes=[pltpu.VMEM((K, N), BF16)],
        input_output_aliases={4: 0, 5: 1},
        compiler_params=_params(1),
        name=name,
    )(a_tail, w, res, g_arr, x_buf, xn_buf)


def _groupnorm_gate(o, gate):
    mu = jnp.mean(o, axis=-1, keepdims=True)
    d = o - mu
    var = jnp.mean(d * d, axis=-1, keepdims=True)
    on = d * lax.rsqrt(var + EPS)
    return (gate * jax.nn.sigmoid(gate)) * on


def _rope_full_head(x, cos, sin):
    x1, x2 = x[:, :RET_DK // 2], x[:, RET_DK // 2:]
    return jnp.concatenate([x1 * cos - x2 * sin, x2 * cos + x1 * sin], axis=1)


def _ret_layer_kernel(x_ref, xd_ref, wq_ref, wk_ref, wv_ref, wg_ref, cos_ref, sin_ref, cosd_ref, sind_ref,
                      dmask_ref, qdec_ref, kdec_ref, cdec_ref, dmaskd_ref, qdecd_ref, kdecd_ref, cdecd_ref,
                      sd_ref, *rest, chunk_rows, n_chunks, t_dec):
    go_ref, sfin_ref, god_ref, snew_ref, s_ref, w_ref = rest[-6:]
    b, c = pl.program_id(1), pl.program_id(2)
    kcol, vcol, gcol = RET_DK, 2 * RET_DK, 2 * RET_DK + RET_DV

    @pl.when((b == 0) & (c == 0))
    def _():
        w_ref[:, :kcol] = wq_ref[...].astype(BF16)
        w_ref[:, kcol:vcol] = wk_ref[...].astype(BF16)
        w_ref[:, vcol:gcol] = wv_ref[...].astype(BF16)
        w_ref[:, gcol:] = wg_ref[...].astype(BF16)

    @pl.when(c == 0)
    def _():
        s_ref[...] = jnp.zeros_like(s_ref)

    def project(x, cos, sin):
        p = _dot(x, w_ref[...])
        q = _rope_full_head(p[:, :kcol], cos, sin).astype(BF16)
        k = (_rope_full_head(p[:, kcol:vcol], cos, sin) * (RET_DK ** -0.5)).astype(BF16)
        return q, k, p[:, vcol:gcol].astype(BF16), p[:, gcol:]

    dmask = dmask_ref[...]
    qdec = qdec_ref[...]
    kdec = kdec_ref[...]
    cdec = cdec_ref[...]
    def project_chunk(ci):
        rows = slice(ci * chunk_rows, (ci + 1) * chunk_rows)
        x, cos, sin = x_ref[rows, :], cos_ref[rows, :], sin_ref[rows, :]
        if ci == 0:
            x = jnp.concatenate([x, xd_ref[...]], axis=0)
            cos = jnp.concatenate([cos, cosd_ref[...]], axis=0)
            sin = jnp.concatenate([sin, sind_ref[...]], axis=0)
        return project(x, cos, sin)

    proj = project_chunk(0)
    qd, kd, vd, gd = (t[chunk_rows:] for t in proj)
    proj = tuple(t[:chunk_rows] for t in proj)
    for ci in range(n_chunks):
        nxt = project_chunk(ci + 1) if ci + 1 < n_chunks else None
        qb, kb, vb, gate = proj
        s = s_ref[...]
        scores = _dot_nt(qb, kb) * dmask
        inner = _dot(scores.astype(BF16), vb)
        cross = _dot(qb, s.astype(BF16)) * qdec
        s_ref[...] = s * cdec + _dot_tn((kb.astype(F32) * kdec).astype(BF16), vb)
        rows = slice(ci * chunk_rows, (ci + 1) * chunk_rows)
        go_ref[rows, :] = _groupnorm_gate(inner + cross, gate).astype(go_ref.dtype)
        proj = nxt

    @pl.when(c == pl.num_programs(2) - 1)
    def _():
        sfin_ref[...] = s_ref[...]

    cdecd = cdecd_ref[...]
    inner = _dot((_dot_nt(qd, kd) * dmaskd_ref[...]).astype(BF16), vd)
    kdd = kd.astype(F32) * kdecd_ref[...]
    seq_of_row = lax.broadcasted_iota(jnp.int32, (RET_DEC_ROWS, 1), 0) // t_dec
    cross = jnp.zeros((RET_DEC_ROWS, RET_DV), F32)
    for bi in range(RET_DEC_ROWS // t_dec):
        mine = seq_of_row == bi
        s = sd_ref[bi]
        cross = cross + jnp.where(mine, _dot(qd, s.astype(BF16)), 0.0)
        snew_ref[bi] = s * cdecd + _dot_tn(jnp.where(mine, kdd, 0.0).astype(BF16), vd)
    cross = cross * qdecd_ref[...]
    god_ref[...] = _groupnorm_gate(inner + cross, gd).astype(god_ref.dtype)


def _ret_layer(xn, w, state, layer, snew_all, tabs, B, L, DB, t_dec, name):
    TP, TD = B * L, DB * t_dec
    K = xn.shape[1]
    nseq = RET_DEC_ROWS // t_dec
    steps = DB // nseq
    assert DB % nseq == 0 and steps % B == 0, (DB, nseq, B)
    nblk = steps // B
    rb = L // nblk
    ck = tabs["chunk_p"]
    assert L % nblk == 0 and rb % ck == 0, (L, nblk, ck)
    r0 = TP // RET_DEC_ROWS
    blk = lambda h, b, c: b * nblk + c
    wcol = lambda width, col0: pl.BlockSpec((None, K, width), lambda h, b, c: (layer, 0, col0 + h))
    per_head = lambda *shape: pl.BlockSpec((None,) + shape, lambda h, b, c: (h,) + (0,) * len(shape))
    s_spec = pl.BlockSpec((None, nseq, None, RET_DK, RET_DV), lambda h, b, c: (layer, blk(h, b, c), h, 0, 0))
    aliased = [] if snew_all is None else [snew_all]
    kern = functools.partial(_ret_layer_kernel, chunk_rows=ck, n_chunks=rb // ck, t_dec=t_dec)
    return pl.pallas_call(
        kern,
        grid=(RET_HEADS, B, nblk),
        in_specs=[pl.BlockSpec((rb, K), lambda h, b, c: (blk(h, b, c), 0)),
                  pl.BlockSpec((RET_DEC_ROWS, K), lambda h, b, c: (r0 + blk(h, b, c), 0)),
                  wcol(RET_DK, 0), wcol(RET_DK, RET_HEADS), wcol(RET_DV, RET_HEADS), wcol(RET_DV, 2 * RET_HEADS),
                  pl.BlockSpec((rb, RET_DK // 2), lambda h, b, c: (blk(h, b, c), 0)),
                  pl.BlockSpec((rb, RET_DK // 2), lambda h, b, c: (blk(h, b, c), 0)),
                  pl.BlockSpec((RET_DEC_ROWS, RET_DK // 2), lambda h, b, c: (r0 + blk(h, b, c), 0)),
                  pl.BlockSpec((RET_DEC_ROWS, RET_DK // 2), lambda h, b, c: (r0 + blk(h, b, c), 0)),
                  per_head(ck, ck), per_head(ck, 1), per_head(ck, 1), per_head(1, 1),
                  per_head(RET_DEC_ROWS, RET_DEC_ROWS), per_head(RET_DEC_ROWS, 1), per_head(RET_DEC_ROWS, 1),
                  per_head(1, 1), s_spec] + [pl.BlockSpec(memory_space=pl.ANY)] * len(aliased),
        out_specs=[pl.BlockSpec((rb, RET_DV), lambda h, b, c: (blk(h, b, c), h)),
                   pl.BlockSpec((None, None, RET_DK, RET_DV), lambda h, b, c: (b, h, 0, 0)),
                   pl.BlockSpec((RET_DEC_ROWS, RET_DV), lambda h, b, c: (blk(h, b, c), h)),
                   s_spec],
        out_shape=[jax.ShapeDtypeStruct((TP, RET_HEADS * RET_DV), BF16),
                   jax.ShapeDtypeStruct((B, RET_HEADS, RET_DK, RET_DV), F32),
                   jax.ShapeDtypeStruct((TD, RET_HEADS * RET_DV), BF16),
                   jax.ShapeDtypeStruct(state.shape, state.dtype)],
        scratch_shapes=[pltpu.VMEM((RET_DK, RET_DV), F32), pltpu.VMEM((K, 2 * RET_DK + 2 * RET_DV), BF16)],
        input_output_aliases={19: 3} if aliased else {},
        compiler_params=_params(3),
        name=name,
    )(xn, xn, w, w, w, w, tabs["ret_cos_rows"], tabs["ret_sin_rows"], tabs["ret_cos_rows"], tabs["ret_sin_rows"],
      tabs["dmask_p"], tabs["qdec_p"], tabs["kdec_p"], tabs["cdec_p"],
      tabs["dmask_d"], tabs["qdec_d"], tabs["kdec_d"], tabs["cdec_d"], state, *aliased)


def _headnorm_rope(x, w, cos2, sin2, bd):
    width = x.shape[1]
    sq = (x * x).astype(BF16)
    parts = [_dot(sq[:, 256 * c:256 * (c + 1)], bd) for c in range(width // 256)]
    ss = parts[0] if len(parts) == 1 else jnp.concatenate(parts, axis=1)
    y = (x * lax.rsqrt(ss * (1.0 / HEAD_DIM) + EPS)) * w
    lane = lax.broadcasted_iota(jnp.int32, y.shape, 1)
    first_half = (lane & (HEAD_DIM - 1)) < HEAD_DIM // 2
    rot = jnp.where(first_half, pltpu.roll(y, width - HEAD_DIM // 2, 1), pltpu.roll(y, HEAD_DIM // 2, 1))
    nrep = width // 128
    cosw = jnp.concatenate([cos2] * nrep, axis=1)
    sinw = jnp.concatenate([sin2] * nrep, axis=1)
    return y * cosw + rot * sinw


def _kv_proj_kernel(x_ref, wkv_ref, w_ref, cos_ref, sin_ref, bd_ref, o_ref, v_ref, lo_ref, hi_ref, wb_ref):
    @pl.when(pl.program_id(0) == 0)
    def _():
        wb_ref[...] = wkv_ref[...].astype(BF16)

    kw = N_KV_HEADS * HEAD_DIM
    sub = SUB_ROWS
    low = lax.broadcasted_iota(jnp.int32, (sub, 128), 1) < HEAD_DIM
    zero = jnp.zeros((sub, 128), F32)
    for r0 in range(0, x_ref.shape[0], sub):
        rows = slice(r0, r0 + sub)
        kv = _dot(x_ref[rows, :], wb_ref[...])
        v_ref[rows, :] = kv[:, kw:]
        kn = _headnorm_rope(kv[:, :kw], w_ref[...], cos_ref[rows, :], sin_ref[rows, :], bd_ref[...])
        o_ref[rows, :] = kn
        for c in range(N_KV_HEADS // 2):
            tile = kn[:, 128 * c:128 * (c + 1)]
            swapped = pltpu.roll(tile, HEAD_DIM, 1)
            even, odd = slice(256 * c, 256 * c + 128), slice(256 * c + 128, 256 * (c + 1))
            lo_ref[rows, even] = jnp.where(low, tile, zero).astype(lo_ref.dtype)
            hi_ref[rows, even] = jnp.where(low, zero, swapped).astype(hi_ref.dtype)
            lo_ref[rows, odd] = jnp.where(low, swapped, zero).astype(lo_ref.dtype)
            hi_ref[rows, odd] = jnp.where(low, zero, tile).astype(hi_ref.dtype)


def _table_block(i, n_prompt_blocks, blocks_per_seq):
    return jnp.where(i < n_prompt_blocks, i % blocks_per_seq, blocks_per_seq + i - n_prompt_blocks)


def _kv_proj(xn, w_kv, k_norm_w, tabs, L, TP, name):
    T, K = xn.shape
    kw = N_KV_HEADS * HEAD_DIM
    npb, bps = TP // ROW_BLOCK, L // ROW_BLOCK
    tab = lambda i: (_table_block(i, npb, bps), 0)
    kv_spec = pl.BlockSpec((ROW_BLOCK, kw), lambda i: (i, 0))
    pad_spec = pl.BlockSpec((ROW_BLOCK, 128 * N_KV_HEADS), lambda i: (i, 0))
    pad_shape = jax.ShapeDtypeStruct((T, 128 * N_KV_HEADS), BF16)
    return pl.pallas_call(
        _kv_proj_kernel,
        grid=(T // ROW_BLOCK,),
        in_specs=[pl.BlockSpec((ROW_BLOCK, K), lambda i: (i, 0)),
                  pl.BlockSpec((K, 2 * kw), lambda i: (0, 0)),
                  pl.BlockSpec((1, kw), lambda i: (0, 0)),
                  pl.BlockSpec((ROW_BLOCK, 128), tab),
                  pl.BlockSpec((ROW_BLOCK, 128), tab),
                  pl.BlockSpec((256, 256), lambda i: (0, 0))],
        out_specs=[kv_spec, kv_spec, pad_spec, pad_spec],
        out_shape=[jax.ShapeDtypeStruct((T, kw), F32), jax.ShapeDtypeStruct((T, kw), F32), pad_shape, pad_shape],
        scratch_shapes=[pltpu.VMEM((K, 2 * kw), BF16)],
        compiler_params=_params(1),
        name=name,
    )(xn, w_kv, k_norm_w, tabs["att_cos"], tabs["att_sin"], tabs["bd"])


def _q_proj_kernel(x_ref, w_ref, qw_ref, cos_ref, sin_ref, bd_ref, o_ref, wb_ref):
    @pl.when(pl.program_id(0) == 0)
    def _():
        wb_ref[...] = w_ref[...].astype(BF16)

    bd = bd_ref[...]
    for r0 in range(0, x_ref.shape[0], 128):
        rows = slice(r0, r0 + 128)
        q = _dot(x_ref[rows, :], wb_ref[...])
        cos2, sin2 = cos_ref[rows, :], sin_ref[rows, :]
        for c0 in range(0, q.shape[1], 256):
            cols = slice(c0, c0 + 256)
            qc = _headnorm_rope(q[:, cols], qw_ref[:, cols], cos2, sin2, bd)
            o_ref[rows, cols] = (qc * (HEAD_DIM ** -0.5)).astype(o_ref.dtype)


def _q_proj(xn, w, q_norm_w, layer, tabs, L, TP, name):
    T, K = xn.shape
    N = w.shape[-1]
    npb, bps = TP // ROW_BLOCK, L // ROW_BLOCK
    tab = lambda i: (_table_block(i, npb, bps), 0)
    return pl.pallas_call(
        _q_proj_kernel,
        grid=(T // ROW_BLOCK,),
        in_specs=[pl.BlockSpec((ROW_BLOCK, K), lambda i: (i, 0)),
                  pl.BlockSpec((None, K, N), lambda i: (layer, 0, 0)),
                  pl.BlockSpec((None, 1, N), lambda i: (layer, 0, 0)),
                  pl.BlockSpec((ROW_BLOCK, 128), tab),
                  pl.BlockSpec((ROW_BLOCK, 128), tab),
                  pl.BlockSpec((256, 256), lambda i: (0, 0))],
        out_specs=pl.BlockSpec((ROW_BLOCK, N), lambda i: (i, 0)),
        out_shape=jax.ShapeDtypeStruct((T, N), BF16),
        scratch_shapes=[pltpu.VMEM((K, N), BF16)],
        compiler_params=_params(1),
        name=name,
    )(xn, w, q_norm_w, tabs["att_cos"], tabs["att_sin"], tabs["bd"])


def _sink_column(sink_ref, layer, g, rows, rows_per_head):
    head_in_group = lax.broadcasted_iota(jnp.int32, (rows, 1), 0) // rows_per_head
    col = jnp.full((rows, 1), sink_ref[layer, g * GROUP], F32)
    for r in range(1, GROUP):
        col = jnp.where(head_in_group == r, sink_ref[layer, g * GROUP + r], col)
    return col


def _attn_prompt_kernel(sink_ref, q_ref, lop_ref, loc_ref, hip_ref, hic_ref, vp_ref, vc_ref, wo_ref, r_ref, g_ref,
                        x_ref, xn_ref, wb_ref, *, layer, n_qblk):
    first = pl.program_id(1) == 0

    @pl.when((pl.program_id(0) == 0) & first)
    def _():
        wb_ref[...] = wo_ref[...].astype(BF16)

    qb = q_ref[...]
    klo = jnp.concatenate([lop_ref[...], loc_ref[...]], axis=0)
    khi = jnp.concatenate([hip_ref[...], hic_ref[...]], axis=0)
    v_t = jnp.concatenate([vp_ref[...], vc_ref[...]], axis=0).T.astype(BF16)
    key = lax.broadcasted_iota(jnp.int32, (WINDOW, WINDOW), 0)
    qi = lax.broadcasted_iota(jnp.int32, (WINDOW, WINDOW), 1)
    own = key <= qi
    scores = {}
    for t in range(n_qblk):
        qrows = slice(WINDOW * t, WINDOW * (t + 1))
        krows = slice(WINDOW * t, WINDOW * (t + 2))
        for g in range(N_KV_HEADS):
            gl = slice(128 * g, 128 * (g + 1))
            kk = jnp.concatenate([klo[krows, gl], khi[krows, gl]], axis=0)
            xq = jnp.concatenate([qb[qrows, 256 * g:256 * g + 128], qb[qrows, 256 * g + 128:256 * (g + 1)]],
                                 axis=0)
            scores[t, g] = _dot_nt(kk, xq)
    for t in range(n_qblk):
        qrows = slice(WINDOW * t, WINDOW * (t + 1))
        krows = slice(WINDOW * t, WINDOW * (t + 2))
        pieces = []
        for g in range(N_KV_HEADS):
            s4 = scores[t, g]
            v_g = v_t[HEAD_DIM * g:HEAD_DIM * (g + 1), krows]
            for pair in range(GROUP // 2):
                p2s, invs = [], []
                for parity in range(2):
                    sk = sink_ref[layer, g * GROUP + 2 * pair + parity]
                    blk = s4[2 * WINDOW * parity:2 * WINDOW * (parity + 1), WINDOW * pair:WINDOW * (pair + 1)]
                    s_prev = blk[:WINDOW]
                    if t == 0:
                        s_prev = jnp.where(first, NEG_INF, s_prev)
                    s = jnp.where(own, blk[WINDOW:], s_prev)
                    m = jnp.maximum(jnp.max(s, axis=0, keepdims=True), sk)
                    p = jnp.exp(s - m)
                    invs.append(1.0 / (jnp.sum(p, axis=0, keepdims=True) + jnp.exp(sk - m)))
                    p2s.append(jnp.concatenate([jnp.where(own, 0.0, p), jnp.where(own, p, 0.0)], axis=0))
                p2 = jnp.concatenate(p2s, axis=1).astype(BF16)
                o_t = _dot(v_g, p2) * jnp.concatenate(invs, axis=1)
                pieces += [o_t[:, :WINDOW], o_t[:, WINDOW:]]
        ao = jnp.concatenate(pieces, axis=0).T.astype(BF16)
        x = r_ref[qrows, :] + _dot(ao, wb_ref[...])
        x_ref[qrows, :] = x
        xn_ref[qrows, :] = _rms_normed(x, [g_ref])[0]


def _attn_prompt(q, klo, khi, v, sinks, w_o, res, gain, layer, B, L, name):
    T = res.shape[0]
    g_arr, g_idx = gain
    nb = L // WINDOW
    nq = ATTN_Q_BLOCKS if nb % ATTN_Q_BLOCKS == 0 else 1
    ns = nb // nq
    kw = N_KV_HEADS * HEAD_DIM
    kpad = 128 * N_KV_HEADS
    cur = lambda b, i: b * ns + i
    prev = lambda b, i: b * nb + jnp.maximum(i * nq - 1, 0)
    kern = functools.partial(_attn_prompt_kernel, layer=layer, n_qblk=nq)
    return pl.pallas_call(
        kern,
        grid=(B, ns),
        in_specs=[pl.BlockSpec(memory_space=pltpu.SMEM),
                  pl.BlockSpec((nq * WINDOW, D_MODEL), lambda b, i: (cur(b, i), 0)),
                  pl.BlockSpec((WINDOW, kpad), lambda b, i: (prev(b, i), 0)),
                  pl.BlockSpec((nq * WINDOW, kpad), lambda b, i: (cur(b, i), 0)),
                  pl.BlockSpec((WINDOW, kpad), lambda b, i: (prev(b, i), 0)),
                  pl.BlockSpec((nq * WINDOW, kpad), lambda b, i: (cur(b, i), 0)),
                  pl.BlockSpec((WINDOW, kw), lambda b, i: (prev(b, i), 0)),
                  pl.BlockSpec((nq * WINDOW, kw), lambda b, i: (cur(b, i), 0)),
                  pl.BlockSpec((None, D_MODEL, D_MODEL), lambda b, i: (layer, 0, 0)),
                  pl.BlockSpec((nq * WINDOW, D_MODEL), lambda b, i: (cur(b, i), 0)),
                  pl.BlockSpec((None, 1, D_MODEL), lambda b, i: (g_idx, 0, 0))],
        out_specs=[pl.BlockSpec((nq * WINDOW, D_MODEL), lambda b, i: (cur(b, i), 0))] * 2,
        out_shape=[jax.ShapeDtypeStruct((T, D_MODEL), F32), jax.ShapeDtypeStruct((T, D_MODEL), BF16)],
        scratch_shapes=[pltpu.VMEM((D_MODEL, D_MODEL), BF16)],
        compiler_params=_params(2),
        name=name,
    )(sinks, q, klo, klo, khi, khi, v, v, w_o, res, g_arr)


def _attn_decode_kernel(sink_ref, q_ref, kn_ref, vn_ref, ck_ref, cv_ref, o_ref, *, layer, t_dec):
    nseq = DEC_ROWS // t_dec
    wc = ck_ref.shape[1]
    qb = q_ref[...]
    knew = kn_ref[...].astype(BF16)
    vnew = vn_ref[...].astype(BF16)
    rows = GROUP * DEC_ROWS
    row = lax.broadcasted_iota(jnp.int32, (rows, 1), 0)
    row_seq = (row % DEC_ROWS) // t_dec
    row_tok = row % t_dec
    jold = lax.broadcasted_iota(jnp.int32, (rows, wc), 1)
    vis_old = (jold > row_tok + (wc - WINDOW)) & (jold <= row_tok + wc)
    cnew = lax.broadcasted_iota(jnp.int32, (rows, DEC_ROWS), 1)
    vis_new = (cnew // t_dec == row_seq) & (cnew % t_dec <= row_tok)
    scores = []
    for g in range(N_KV_HEADS):
        heads = [g * GROUP + r for r in range(GROUP)]
        hs = slice(HEAD_DIM * g, HEAD_DIM * (g + 1))
        qs = jnp.concatenate([qb[:, HEAD_DIM * h:HEAD_DIM * (h + 1)] for h in heads], axis=0)
        s_old = jnp.zeros((rows, wc), F32)
        for bi in range(nseq):
            kc = ck_ref[bi][:, hs].astype(BF16)
            s_old = s_old + jnp.where(row_seq == bi, _dot_nt(qs, kc), 0.0)
        scores.append((s_old, _dot_nt(qs, knew[:, hs])))
    for g in range(N_KV_HEADS):
        heads = [g * GROUP + r for r in range(GROUP)]
        hs = slice(HEAD_DIM * g, HEAD_DIM * (g + 1))
        s_old = jnp.where(vis_old, scores[g][0], NEG_INF)
        s_new = jnp.where(vis_new, scores[g][1], NEG_INF)
        sk = _sink_column(sink_ref, layer, g, rows, DEC_ROWS)
        m = jnp.maximum(jnp.maximum(jnp.max(s_old, axis=-1, keepdims=True),
                                    jnp.max(s_new, axis=-1, keepdims=True)), sk)
        p_old = jnp.exp(s_old - m)
        p_new = jnp.exp(s_new - m)
        denom = (jnp.sum(p_old, axis=-1, keepdims=True) + jnp.sum(p_new, axis=-1, keepdims=True)
                 + jnp.exp(sk - m))
        inv = 1.0 / denom
        o = _dot((p_new * inv).astype(BF16), vnew[:, hs])
        pn_old = p_old * inv
        for bi in range(nseq):
            vc = cv_ref[bi][:, hs].astype(BF16)
            o = o + _dot(jnp.where(row_seq == bi, pn_old, 0.0).astype(BF16), vc)
        for r, h in enumerate(heads):
            o_ref[:, HEAD_DIM * h:HEAD_DIM * (h + 1)] = o[DEC_ROWS * r:DEC_ROWS * (r + 1)].astype(o_ref.dtype)


def _attn_decode(q, kn, v, cache_k, cache_v, sinks, layer, TP, DB, t_dec, name):
    nseq = DEC_ROWS // t_dec
    r0 = TP // DEC_ROWS
    kw = N_KV_HEADS * HEAD_DIM
    wc = cache_k.shape[1]
    kern = functools.partial(_attn_decode_kernel, layer=layer, t_dec=t_dec)
    return pl.pallas_call(
        kern,
        grid=(DB // nseq,),
        in_specs=[pl.BlockSpec(memory_space=pltpu.SMEM),
                  pl.BlockSpec((DEC_ROWS, D_MODEL), lambda i: (r0 + i, 0)),
                  pl.BlockSpec((DEC_ROWS, kw), lambda i: (r0 + i, 0)),
                  pl.BlockSpec((DEC_ROWS, kw), lambda i: (r0 + i, 0)),
                  pl.BlockSpec((nseq, wc, kw), lambda i: (i, 0, 0)),
                  pl.BlockSpec((nseq, wc, kw), lambda i: (i, 0, 0))],
        out_specs=pl.BlockSpec((DEC_ROWS, D_MODEL), lambda i: (i, 0)),
        out_shape=jax.ShapeDtypeStruct((DB * t_dec, D_MODEL), BF16),
        compiler_params=_params(1),
        name=name,
    )(sinks, q, kn, v, cache_k, cache_v)


def _rope_cos_sin(pos, half):
    inv = 1.0 / (ROPE_THETA ** (np.arange(half, dtype=np.float64) / half))
    ang = pos.astype(np.float64)[:, None] * inv[None, :]
    return np.cos(ang), np.sin(ang)


def _decay_tables(chunk, reps):
    lg = np.log(1.0 - 2.0 ** (-5.0 - np.arange(RET_HEADS, dtype=np.float64)))
    idx = np.arange(chunk, dtype=np.float64)
    rel = idx[:, None] - idx[None, :]
    dmask = np.where(rel >= 0, np.exp(lg[:, None, None] * np.maximum(rel, 0.0)), 0.0)
    qdec = np.exp(lg[:, None] * (idx + 1.0))[:, :, None]
    kdec = np.exp(lg[:, None] * (chunk - 1.0 - idx))[:, :, None]
    cdec = np.exp(lg * chunk)[:, None, None]
    if reps > 1:
        dmask = np.einsum("ab,hij->haibj", np.eye(reps), dmask).reshape(RET_HEADS, reps * chunk, reps * chunk)
        qdec = np.tile(qdec, (1, reps, 1))
        kdec = np.tile(kdec, (1, reps, 1))
    return dmask, qdec, kdec, cdec


def _tables(B, L, TD, t_dec, chunk_p):
    pos = np.concatenate([np.arange(L), PAST_LEN + np.arange(TD) % t_dec])
    ret_cos, ret_sin = _rope_cos_sin(pos, RET_DK // 2)
    per_row = lambda t: np.concatenate([np.tile(t[:L], (B, 1)), t[L:]], axis=0)
    c, s = _rope_cos_sin(pos, HEAD_DIM // 2)
    att_cos = np.concatenate([c, c, c, c], axis=1)
    att_sin = np.concatenate([-s, s, -s, s], axis=1)
    dmask_p, qdec_p, kdec_p, cdec_p = _decay_tables(chunk_p, 1)
    dmask_d, qdec_d, kdec_d, cdec_d = _decay_tables(math.gcd(t_dec, RET_CHUNK), RET_DEC_ROWS // t_dec)
    head_of_lane = np.arange(256) // HEAD_DIM
    f32 = dict(ret_cos_rows=per_row(ret_cos), ret_sin_rows=per_row(ret_sin), att_cos=att_cos, att_sin=att_sin,
               dmask_p=dmask_p, qdec_p=qdec_p, kdec_p=kdec_p, cdec_p=cdec_p,
               dmask_d=dmask_d, qdec_d=qdec_d, kdec_d=kdec_d, cdec_d=cdec_d)
    tabs = {name: jnp.asarray(t.astype(np.float32)) for name, t in f32.items()}
    tabs["bd"] = jnp.asarray((head_of_lane[:, None] == head_of_lane[None, :]).astype(np.float32), dtype=BF16)
    tabs["chunk_p"] = chunk_p
    return tabs


def kernel(x_prompt, x_sample, state_ret, cache_k, cache_v, ln_ret, w_ret_in, w_ret_out, ln_ffn, w_ffn_in,
           w_ffn_out, ln_kv, w_kv, k_norm, ln_attn, w_q, q_norm, sinks, w_o):
    B, L, D = x_prompt.shape
    DB, t_dec, _ = x_sample.shape
    n_ret = w_ret_in.shape[0]
    n_attn = w_q.shape[0]
    TP, TD = B * L, DB * t_dec
    T = TP + TD
    wc = cache_k.shape[1]
    kw = N_KV_HEADS * HEAD_DIM
    assert D == D_MODEL and L % RET_CHUNK == 0 and L % ROW_BLOCK == 0 and TD % ROW_BLOCK == 0
    assert t_dec == math.gcd(t_dec, RET_CHUNK) and DEC_ROWS % t_dec == 0 and wc == WINDOW and PAST_LEN >= WINDOW

    tabs = _tables(B, L, TD, t_dec, RET_CHUNK_PROMPT if L % RET_CHUNK_PROMPT == 0 else RET_CHUNK)
    ck = cache_k.reshape(DB, wc, kw)
    cv = cache_v.reshape(DB, wc, kw)
    q_norm_w = jnp.tile(q_norm, (1, N_HEADS))[:, None, :]
    k_norm_w = jnp.tile(k_norm[None, :], (1, N_KV_HEADS))
    ln_ret, ln_ffn, ln_attn = ln_ret[:, None, :], ln_ffn[:, None, :], ln_attn[:, None, :]

    ln_kv3 = ln_kv[None, None, :]
    prompt_states = []
    dec_states = None
    x = (x_prompt.reshape(TP, D), x_sample.reshape(TD, D))
    xn = _embed(*x, (ln_ret, 0), "embed")
    for l in range(n_ret):
        go_p, s_p, go_d, dec_states = _ret_layer(xn, w_ret_in, state_ret, l, dec_states, tabs, B, L, DB, t_dec,
                                                 f"retention_{l}")
        prompt_states.append(s_p)
        x, xn = _matmul_residual((go_p, go_d), w_ret_out, l, x, [(ln_ffn, l)], f"ret_out_{l}", TP)
        h = _swiglu(xn, w_ffn_in, l, f"ffn_in_{l}")
        if l + 1 < n_ret:
            x, xn = _matmul_residual(h, w_ffn_out, l, x, [(ln_ret, l + 1)], f"ffn_out_{l}", TP)
        else:
            x, xn_kv, xn = _matmul_residual(h, w_ffn_out, l, x, [(ln_kv3, 0), (ln_attn, 0)], f"ffn_out_{l}", TP)

    kn, v, klo, khi = _kv_proj(xn_kv, w_kv, k_norm_w, tabs, L, TP, "kv_proj")
    for j in range(n_attn):
        layer = n_ret + j
        q = _q_proj(xn, w_q, q_norm_w, j, tabs, L, TP, f"q_proj_{j}")
        x_p, xn_p = _attn_prompt(q, klo, khi, v, sinks, w_o, x, (ln_ffn, layer), j, B, L, f"attn_prompt_{j}")
        ao_d = _attn_decode(q, kn, v, ck, cv, sinks, j, TP, DB, t_dec, f"attn_decode_{j}")
        x, xn = _residual_tail_rows(ao_d, w_o, j, x, (ln_ffn, layer), x_p, xn_p, TP, f"attn_out_{j}")
        h = _swiglu(xn, w_ffn_in, layer, f"ffn_in_{layer}")
        if j + 1 < n_attn:
            x, xn = _matmul_residual(h, w_ffn_out, layer, x, [(ln_attn, j + 1)], f"ffn_out_{layer}", TP)
        else:
            y_p, y_s = _matmul_residual(h, w_ffn_out, layer, x, [], f"ffn_out_{layer}", TP, split_out=True)

    y_prompt = y_p.reshape(B, L, D)
    y_sample = y_s.reshape(DB, t_dec, D)
    state_prompt = jnp.stack(prompt_states)
    w_keep = min(WINDOW, L)
    tail_rows = lambda t: t[:TP].reshape(B, L, kw)[:, L - w_keep:].reshape(B, w_keep, N_KV_HEADS, HEAD_DIM)
    kn_p, v_p = tail_rows(kn), tail_rows(v)
    kn_d = kn[TP:].reshape(DB, t_dec, N_KV_HEADS, HEAD_DIM)
    v_d = v[TP:].reshape(DB, t_dec, N_KV_HEADS, HEAD_DIM)
    cache_k_sample = jnp.concatenate([cache_k, kn_d], axis=1)[:, -wc:]
    cache_v_sample = jnp.concatenate([cache_v, v_d], axis=1)[:, -wc:]
    return (y_prompt, y_sample, state_prompt, dec_states, kn_p, v_p,
            cache_k_sample, cache_v_sample)
```

```python
import functools
import math

import numpy as np
import jax
import jax.numpy as jnp
from jax import lax
from jax.experimental import pallas as pl
from jax.experimental.pallas import tpu as pltpu

D_MODEL = 1024
PAST_LEN = 8192
RET_HEADS = 4
RET_DK = 256
RET_DV = 512
RET_CHUNK = 128
RET_CHUNK_PROMPT = 256
N_HEADS = 16
N_KV_HEADS = 4
HEAD_DIM = 64
GROUP = N_HEADS // N_KV_HEADS
WINDOW = 128
D_FF = 2816
ROPE_THETA = 10000.0
EPS = 1e-6
NEG_INF = -1e30

F32 = jnp.float32
BF16 = jnp.bfloat16

V7X_VMEM_LIMIT_BYTES = 56 * 1024 * 1024
DEC_ROWS = 16
RET_DEC_ROWS = 32
ROW_BLOCK = 512
SUB_ROWS = 256
ATTN_Q_BLOCKS = 8


def _dot(a, b):
    return jnp.dot(a, b, preferred_element_type=F32)


def _dot_nt(a, b):
    return lax.dot_general(a, b, (((1,), (1,)), ((), ())), preferred_element_type=F32)


def _dot_tn(a, b):
    return lax.dot_general(a, b, (((0,), (0,)), ((), ())), preferred_element_type=F32)


def _params(n_axes):
    return pltpu.CompilerParams(dimension_semantics=("arbitrary",) * n_axes,
                                vmem_limit_bytes=V7X_VMEM_LIMIT_BYTES)


def _row_tile(rows, prefs):
    for t in prefs:
        if rows % t == 0:
            return t
    raise ValueError(f"no row tile for {rows}")


def _rms_normed(x, gain_refs):
    ms = jnp.mean(x * x, axis=-1, keepdims=True)
    xh = x * lax.rsqrt(ms + EPS)
    return [(xh * g_ref[...]).astype(BF16) for g_ref in gain_refs]


def _embed_kernel(xp_ref, xs_ref, g_ref, xn_ref, *, n_prompt_blocks):
    i = pl.program_id(0)

    @pl.when(i < n_prompt_blocks)
    def _():
        xn_ref[...] = _rms_normed(xp_ref[...], [g_ref])[0]

    @pl.when(i >= n_prompt_blocks)
    def _():
        xn_ref[...] = _rms_normed(xs_ref[...], [g_ref])[0]


def _embed(xp, xs, gain, name):
    TP, D = xp.shape
    TD = xs.shape[0]
    npb = TP // ROW_BLOCK
    T = TP + TD
    g_arr, g_idx = gain
    blk = lambda fn: pl.BlockSpec((ROW_BLOCK, D), fn)
    return pl.pallas_call(
        functools.partial(_embed_kernel, n_prompt_blocks=npb),
        grid=(T // ROW_BLOCK,),
        in_specs=[blk(lambda i: (jnp.minimum(i, npb - 1), 0)),
                  blk(lambda i: (jnp.maximum(i - npb, 0), 0)),
                  pl.BlockSpec((None, 1, D), lambda i: (g_idx, 0, 0))],
        out_specs=blk(lambda i: (i, 0)),
        out_shape=jax.ShapeDtypeStruct((T, D), BF16),
        compiler_params=_params(1),
        name=name,
    )(xp, xs, g_arr)


def _sub_rows(tm):
    return max(r for r in range(16, 3 * SUB_ROWS + 1, 16) if tm % r == 0)


def _swiglu_kernel(x_ref, wa_ref, wb_ref, o_ref):
    wa = wa_ref[...].astype(BF16)
    wb = wb_ref[...].astype(BF16)
    sub = _sub_rows(x_ref.shape[0])
    for r0 in range(0, x_ref.shape[0], sub):
        rows = slice(r0, r0 + sub)
        a = _dot(x_ref[rows, :], wa)
        b = _dot(x_ref[rows, :], wb)
        o_ref[rows, :] = ((a * jax.nn.sigmoid(a)) * b).astype(o_ref.dtype)


def _swiglu(xn, w, layer, name):
    T, K = xn.shape
    tn = 256
    nb = D_FF // tn
    tm = _row_tile(T, (5632, 2816, 1536, 1024, 512))
    return pl.pallas_call(
        _swiglu_kernel,
        grid=(T // tm, nb),
        in_specs=[pl.BlockSpec((tm, K), lambda i, j: (i, 0)),
                  pl.BlockSpec((None, K, tn), lambda i, j: (layer, 0, j)),
                  pl.BlockSpec((None, K, tn), lambda i, j: (layer, 0, nb + j))],
        out_specs=pl.BlockSpec((tm, tn), lambda i, j: (i, j)),
        out_shape=jax.ShapeDtypeStruct((T, D_FF), BF16),
        compiler_params=_params(2),
        name=name,
    )(xn, w, w)


def _mm_res_kernel(*refs, n_a, n_r, n_gains, n_prompt_blocks, split_out, n_alias=0):
    a_refs = refs[:n_a]
    w_ref = refs[n_a]
    r_refs = refs[n_a + 1:n_a + 1 + n_r]
    rest = refs[n_a + 1 + n_r:]
    gain_refs = rest[:n_gains]
    out_refs = rest[n_gains + n_alias:-1]
    wb_ref = rest[-1]
    i = pl.program_id(0)

    @pl.when(i == 0)
    def _():
        wb_ref[...] = w_ref[...].astype(BF16)

    def step(a_ref, r_ref, dst):
        for r0 in range(0, a_ref.shape[0], SUB_ROWS):
            rows = slice(r0, r0 + SUB_ROWS)
            x = r_ref[rows, :] + _dot(a_ref[rows, :], wb_ref[...])
            out_refs[dst][rows, :] = x
            if not split_out:
                for o_ref, xn in zip(out_refs[1:], _rms_normed(x, gain_refs)):
                    o_ref[rows, :] = xn

    if n_a == 1 and n_r == 1 and not split_out:
        step(a_refs[0], r_refs[0], 0)
    else:
        @pl.when(i < n_prompt_blocks)
        def _():
            step(a_refs[0], r_refs[0], 0)

        @pl.when(i >= n_prompt_blocks)
        def _():
            step(a_refs[-1], r_refs[-1], 1 if split_out else 0)


def _matmul_residual(a, w, layer, res, gains, name, TP, split_out=False):
    a = a if isinstance(a, (tuple, list)) else (a,)
    res = res if isinstance(res, (tuple, list)) else (res,)
    T = sum(r.shape[0] for r in res)
    N = res[0].shape[1]
    K = a[0].shape[1]
    tm = ROW_BLOCK
    npb = TP // tm
    head = lambda i: (jnp.minimum(i, npb - 1), 0)
    tail = lambda i: (jnp.maximum(i - npb, 0), 0)
    whole = lambda i: (i, 0)
    rows_of = lambda arrs, width: [pl.BlockSpec((tm, width), fn)
                                   for fn in ((whole,) if len(arrs) == 1 else (head, tail))]
    in_specs = rows_of(a, K) + [pl.BlockSpec((None, K, N), lambda i: (layer, 0, 0))] + rows_of(res, N)
    in_specs += [pl.BlockSpec((None, 1, N), lambda i, idx=idx: (idx, 0, 0)) for _, idx in gains]
    if split_out:
        assert not gains
        out_specs = [pl.BlockSpec((tm, N), head), pl.BlockSpec((tm, N), tail)]
        out_shape = [jax.ShapeDtypeStruct((TP, N), F32), jax.ShapeDtypeStruct((T - TP, N), F32)]
    else:
        out_specs = [pl.BlockSpec((tm, N), whole)] * (1 + len(gains))
        out_shape = [jax.ShapeDtypeStruct((T, N), F32)] + [jax.ShapeDtypeStruct((T, N), BF16)] * len(gains)
    aliases = {len(a) + 1: 0} if len(res) == 1 and not split_out else {}
    return pl.pallas_call(
        functools.partial(_mm_res_kernel, n_a=len(a), n_r=len(res), n_gains=len(gains), n_prompt_blocks=npb,
                          split_out=split_out),
        grid=(T // tm,),
        in_specs=in_specs,
        out_specs=out_specs,
        out_shape=out_shape,
        scratch_shapes=[pltpu.VMEM((K, N), BF16)],
        input_output_aliases=aliases,
        compiler_params=_params(1),
        name=name,
    )(*a, w, *res, *[g for g, _ in gains])


def _residual_tail_rows(a_tail, w, layer, res, gain, x_buf, xn_buf, TP, name):
    T, N = res.shape
    K = a_tail.shape[1]
    tm = ROW_BLOCK
    npb = TP // tm
    g_arr, g_idx = gain
    tail = pl.BlockSpec((tm, N), lambda i: (npb + i, 0))
    return pl.pallas_call(
        functools.partial(_mm_res_kernel, n_a=1, n_r=1, n_gains=1, n_prompt_blocks=npb, split_out=False,
                          n_alias=2),
        grid=((T - TP) // tm,),
        in_specs=[pl.BlockSpec((tm, K), lambda i: (i, 0)),
                  pl.BlockSpec((None, K, N), lambda i: (layer, 0, 0)),
                  tail,
                  pl.BlockSpec((None, 1, N), lambda i: (g_idx, 0, 0)),
                  pl.BlockSpec(memory_space=pl.ANY), pl.BlockSpec(memory_space=pl.ANY)],
        out_specs=[tail, tail],
        out_shape=[jax.ShapeDtypeStruct((T, N), F32), jax.ShapeDtypeStruct((T, N), BF16)],
        scratch_shapes=[pltpu.VMEM((K, N), BF16)],
        input_output_aliases={4: 0, 5: 1},
        compiler_params=_params(1),
        name=name,
    )(a_tail, w, res, g_arr, x_buf, xn_buf)


def _groupnorm_gate(o, gate):
    mu = jnp.mean(o, axis=-1, keepdims=True)
    d = o - mu
    var = jnp.mean(d * d, axis=-1, keepdims=True)
    on = d * lax.rsqrt(var + EPS)
    return (gate * jax.nn.sigmoid(gate)) * on


def _rope_full_head(x, cos, sin):
    x1, x2 = x[:, :RET_DK // 2], x[:, RET_DK // 2:]
    return jnp.concatenate([x1 * cos - x2 * sin, x2 * cos + x1 * sin], axis=1)


def _ret_layer_kernel(x_ref, xd_ref, wq_ref, wk_ref, wv_ref, wg_ref, cos_ref, sin_ref, cosd_ref, sind_ref,
                      dmask_ref, qdec_ref, kdec_ref, cdec_ref, dmaskd_ref, qdecd_ref, kdecd_ref, cdecd_ref,
                      sd_ref, *rest, chunk_rows, n_chunks, t_dec):
    go_ref, sfin_ref, god_ref, snew_ref, s_ref, w_ref = rest[-6:]
    b, c = pl.program_id(1), pl.program_id(2)
    kcol, vcol, gcol = RET_DK, 2 * RET_DK, 2 * RET_DK + RET_DV

    @pl.when((b == 0) & (c == 0))
    def _():
        w_ref[:, :kcol] = wq_ref[...].astype(BF16)
        w_ref[:, kcol:vcol] = wk_ref[...].astype(BF16)
        w_ref[:, vcol:gcol] = wv_ref[...].astype(BF16)
        w_ref[:, gcol:] = wg_ref[...].astype(BF16)

    @pl.when(c == 0)
    def _():
        s_ref[...] = jnp.zeros_like(s_ref)

    def project(x, cos, sin):
        p = _dot(x, w_ref[...])
        q = _rope_full_head(p[:, :kcol], cos, sin).astype(BF16)
        k = (_rope_full_head(p[:, kcol:vcol], cos, sin) * (RET_DK ** -0.5)).astype(BF16)
        return q, k, p[:, vcol:gcol].astype(BF16), p[:, gcol:]

    dmask = dmask_ref[...]
    qdec = qdec_ref[...]
    kdec = kdec_ref[...]
    cdec = cdec_ref[...]
    def project_chunk(ci):
        rows = slice(ci * chunk_rows, (ci + 1) * chunk_rows)
        x, cos, sin = x_ref[rows, :], cos_ref[rows, :], sin_ref[rows, :]
        if ci == 0:
            x = jnp.concatenate([x, xd_ref[...]], axis=0)
            cos = jnp.concatenate([cos, cosd_ref[...]], axis=0)
            sin = jnp.concatenate([sin, sind_ref[...]], axis=0)
        return project(x, cos, sin)

    proj = project_chunk(0)
    qd, kd, vd, gd = (t[chunk_rows:] for t in proj)
    proj = tuple(t[:chunk_rows] for t in proj)
    for ci in range(n_chunks):
        nxt = project_chunk(ci + 1) if ci + 1 < n_chunks else None
        qb, kb, vb, gate = proj
        s = s_ref[...]
        scores = _dot_nt(qb, kb) * dmask
        inner = _dot(scores.astype(BF16), vb)
        cross = _dot(qb, s.astype(BF16)) * qdec
        s_ref[...] = s * cdec + _dot_tn((kb.astype(F32) * kdec).astype(BF16), vb)
        rows = slice(ci * chunk_rows, (ci + 1) * chunk_rows)
        go_ref[rows, :] = _groupnorm_gate(inner + cross, gate).astype(go_ref.dtype)
        proj = nxt

    @pl.when(c == pl.num_programs(2) - 1)
    def _():
        sfin_ref[...] = s_ref[...]

    cdecd = cdecd_ref[...]
    inner = _dot((_dot_nt(qd, kd) * dmaskd_ref[...]).astype(BF16), vd)
    kdd = kd.astype(F32) * kdecd_ref[...]
    seq_of_row = lax.broadcasted_iota(jnp.int32, (RET_DEC_ROWS, 1), 0) // t_dec
    cross = jnp.zeros((RET_DEC_ROWS, RET_DV), F32)
    for bi in range(RET_DEC_ROWS // t_dec):
        mine = seq_of_row == bi
        s = sd_ref[bi]
        cross = cross + jnp.where(mine, _dot(qd, s.astype(BF16)), 0.0)
        snew_ref[bi] = s * cdecd + _dot_tn(jnp.where(mine, kdd, 0.0).astype(BF16), vd)
    cross = cross * qdecd_ref[...]
    god_ref[...] = _groupnorm_gate(inner + cross, gd).astype(god_ref.dtype)


def _ret_layer(xn, w, state, layer, snew_all, tabs, B, L, DB, t_dec, name):
    TP, TD = B * L, DB * t_dec
    K = xn.shape[1]
    nseq = RET_DEC_ROWS // t_dec
    steps = DB // nseq
    assert DB % nseq == 0 and steps % B == 0, (DB, nseq, B)
    nblk = steps // B
    rb = L // nblk
    ck = tabs["chunk_p"]
    assert L % nblk == 0 and rb % ck == 0, (L, nblk, ck)
    r0 = TP // RET_DEC_ROWS
    blk = lambda h, b, c: b * nblk + c
    wcol = lambda width, col0: pl.BlockSpec((None, K, width), lambda h, b, c: (layer, 0, col0 + h))
    per_head = lambda *shape: pl.BlockSpec((None,) + shape, lambda h, b, c: (h,) + (0,) * len(shape))
    s_spec = pl.BlockSpec((None, nseq, None, RET_DK, RET_DV), lambda h, b, c: (layer, blk(h, b, c), h, 0, 0))
    aliased = [] if snew_all is None else [snew_all]
    kern = functools.partial(_ret_layer_kernel, chunk_rows=ck, n_chunks=rb // ck, t_dec=t_dec)
    return pl.pallas_call(
        kern,
        grid=(RET_HEADS, B, nblk),
        in_specs=[pl.BlockSpec((rb, K), lambda h, b, c: (blk(h, b, c), 0)),
                  pl.BlockSpec((RET_DEC_ROWS, K), lambda h, b, c: (r0 + blk(h, b, c), 0)),
                  wcol(RET_DK, 0), wcol(RET_DK, RET_HEADS), wcol(RET_DV, RET_HEADS), wcol(RET_DV, 2 * RET_HEADS),
                  pl.BlockSpec((rb, RET_DK // 2), lambda h, b, c: (blk(h, b, c), 0)),
                  pl.BlockSpec((rb, RET_DK // 2), lambda h, b, c: (blk(h, b, c), 0)),
                  pl.BlockSpec((RET_DEC_ROWS, RET_DK // 2), lambda h, b, c: (r0 + blk(h, b, c), 0)),
                  pl.BlockSpec((RET_DEC_ROWS, RET_DK // 2), lambda h, b, c: (r0 + blk(h, b, c), 0)),
                  per_head(ck, ck), per_head(ck, 1), per_head(ck, 1), per_head(1, 1),
                  per_head(RET_DEC_ROWS, RET_DEC_ROWS), per_head(RET_DEC_ROWS, 1), per_head(RET_DEC_ROWS, 1),
                  per_head(1, 1), s_spec] + [pl.BlockSpec(memory_space=pl.ANY)] * len(aliased),
        out_specs=[pl.BlockSpec((rb, RET_DV), lambda h, b, c: (blk(h, b, c), h)),
                   pl.BlockSpec((None, None, RET_DK, RET_DV), lambda h, b, c: (b, h, 0, 0)),
                   pl.BlockSpec((RET_DEC_ROWS, RET_DV), lambda h, b, c: (blk(h, b, c), h)),
                   s_spec],
        out_shape=[jax.ShapeDtypeStruct((TP, RET_HEADS * RET_DV), BF16),
                   jax.ShapeDtypeStruct((B, RET_HEADS, RET_DK, RET_DV), F32),
                   jax.ShapeDtypeStruct((TD, RET_HEADS * RET_DV), BF16),
                   jax.ShapeDtypeStruct(state.shape, state.dtype)],
        scratch_shapes=[pltpu.VMEM((RET_DK, RET_DV), F32), pltpu.VMEM((K, 2 * RET_DK + 2 * RET_DV), BF16)],
        input_output_aliases={19: 3} if aliased else {},
        compiler_params=_params(3),
        name=name,
    )(xn, xn, w, w, w, w, tabs["ret_cos_rows"], tabs["ret_sin_rows"], tabs["ret_cos_rows"], tabs["ret_sin_rows"],
      tabs["dmask_p"], tabs["qdec_p"], tabs["kdec_p"], tabs["cdec_p"],
      tabs["dmask_d"], tabs["qdec_d"], tabs["kdec_d"], tabs["cdec_d"], state, *aliased)


def _headnorm_rope(x, w, cos2, sin2, bd):
    width = x.shape[1]
    sq = (x * x).astype(BF16)
    parts = [_dot(sq[:, 256 * c:256 * (c + 1)], bd) for c in range(width // 256)]
    ss = parts[0] if len(parts) == 1 else jnp.concatenate(parts, axis=1)
    y = (x * lax.rsqrt(ss * (1.0 / HEAD_DIM) + EPS)) * w
    lane = lax.broadcasted_iota(jnp.int32, y.shape, 1)
    first_half = (lane & (HEAD_DIM - 1)) < HEAD_DIM // 2
    rot = jnp.where(first_half, pltpu.roll(y, width - HEAD_DIM // 2, 1), pltpu.roll(y, HEAD_DIM // 2, 1))
    nrep = width // 128
    cosw = jnp.concatenate([cos2] * nrep, axis=1)
    sinw = jnp.concatenate([sin2] * nrep, axis=1)
    return y * cosw + rot * sinw


def _kv_proj_kernel(x_ref, wkv_ref, w_ref, cos_ref, sin_ref, bd_ref, o_ref, v_ref, lo_ref, hi_ref, wb_ref):
    @pl.when(pl.program_id(0) == 0)
    def _():
        wb_ref[...] = wkv_ref[...].astype(BF16)

    kw = N_KV_HEADS * HEAD_DIM
    sub = SUB_ROWS
    low = lax.broadcasted_iota(jnp.int32, (sub, 128), 1) < HEAD_DIM
    zero = jnp.zeros((sub, 128), F32)
    for r0 in range(0, x_ref.shape[0], sub):
        rows = slice(r0, r0 + sub)
        kv = _dot(x_ref[rows, :], wb_ref[...])
        v_ref[rows, :] = kv[:, kw:]
        kn = _headnorm_rope(kv[:, :kw], w_ref[...], cos_ref[rows, :], sin_ref[rows, :], bd_ref[...])
        o_ref[rows, :] = kn
        for c in range(N_KV_HEADS // 2):
            tile = kn[:, 128 * c:128 * (c + 1)]
            swapped = pltpu.roll(tile, HEAD_DIM, 1)
            even, odd = slice(256 * c, 256 * c + 128), slice(256 * c + 128, 256 * (c + 1))
            lo_ref[rows, even] = jnp.where(low, tile, zero).astype(lo_ref.dtype)
            hi_ref[rows, even] = jnp.where(low, zero, swapped).astype(hi_ref.dtype)
            lo_ref[rows, odd] = jnp.where(low, swapped, zero).astype(lo_ref.dtype)
            hi_ref[rows, odd] = jnp.where(low, zero, tile).astype(hi_ref.dtype)


def _table_block(i, n_prompt_blocks, blocks_per_seq):
    return jnp.where(i < n_prompt_blocks, i % blocks_per_seq, blocks_per_seq + i - n_prompt_blocks)


def _kv_proj(xn, w_kv, k_norm_w, tabs, L, TP, name):
    T, K = xn.shape
    kw = N_KV_HEADS * HEAD_DIM
    npb, bps = TP // ROW_BLOCK, L // ROW_BLOCK
    tab = lambda i: (_table_block(i, npb, bps), 0)
    kv_spec = pl.BlockSpec((ROW_BLOCK, kw), lambda i: (i, 0))
    pad_spec = pl.BlockSpec((ROW_BLOCK, 128 * N_KV_HEADS), lambda i: (i, 0))
    pad_shape = jax.ShapeDtypeStruct((T, 128 * N_KV_HEADS), BF16)
    return pl.pallas_call(
        _kv_proj_kernel,
        grid=(T // ROW_BLOCK,),
        in_specs=[pl.BlockSpec((ROW_BLOCK, K), lambda i: (i, 0)),
                  pl.BlockSpec((K, 2 * kw), lambda i: (0, 0)),
                  pl.BlockSpec((1, kw), lambda i: (0, 0)),
                  pl.BlockSpec((ROW_BLOCK, 128), tab),
                  pl.BlockSpec((ROW_BLOCK, 128), tab),
                  pl.BlockSpec((256, 256), lambda i: (0, 0))],
        out_specs=[kv_spec, kv_spec, pad_spec, pad_spec],
        out_shape=[jax.ShapeDtypeStruct((T, kw), F32), jax.ShapeDtypeStruct((T, kw), F32), pad_shape, pad_shape],
        scratch_shapes=[pltpu.VMEM((K, 2 * kw), BF16)],
        compiler_params=_params(1),
        name=name,
    )(xn, w_kv, k_norm_w, tabs["att_cos"], tabs["att_sin"], tabs["bd"])


def _q_proj_kernel(x_ref, w_ref, qw_ref, cos_ref, sin_ref, bd_ref, o_ref, wb_ref):
    @pl.when(pl.program_id(0) == 0)
    def _():
        wb_ref[...] = w_ref[...].astype(BF16)

    bd = bd_ref[...]
    for r0 in range(0, x_ref.shape[0], 128):
        rows = slice(r0, r0 + 128)
        q = _dot(x_ref[rows, :], wb_ref[...])
        cos2, sin2 = cos_ref[rows, :], sin_ref[rows, :]
        for c0 in range(0, q.shape[1], 256):
            cols = slice(c0, c0 + 256)
            qc = _headnorm_rope(q[:, cols], qw_ref[:, cols], cos2, sin2, bd)
            o_ref[rows, cols] = (qc * (HEAD_DIM ** -0.5)).astype(o_ref.dtype)


def _q_proj(xn, w, q_norm_w, layer, tabs, L, TP, name):
    T, K = xn.shape
    N = w.shape[-1]
    npb, bps = TP // ROW_BLOCK, L // ROW_BLOCK
    tab = lambda i: (_table_block(i, npb, bps), 0)
    return pl.pallas_call(
        _q_proj_kernel,
        grid=(T // ROW_BLOCK,),
        in_specs=[pl.BlockSpec((ROW_BLOCK, K), lambda i: (i, 0)),
                  pl.BlockSpec((None, K, N), lambda i: (layer, 0, 0)),
                  pl.BlockSpec((None, 1, N), lambda i: (layer, 0, 0)),
                  pl.BlockSpec((ROW_BLOCK, 128), tab),
                  pl.BlockSpec((ROW_BLOCK, 128), tab),
                  pl.BlockSpec((256, 256), lambda i: (0, 0))],
        out_specs=pl.BlockSpec((ROW_BLOCK, N), lambda i: (i, 0)),
        out_shape=jax.ShapeDtypeStruct((T, N), BF16),
        scratch_shapes=[pltpu.VMEM((K, N), BF16)],
        compiler_params=_params(1),
        name=name,
    )(xn, w, q_norm_w, tabs["att_cos"], tabs["att_sin"], tabs["bd"])


def _sink_column(sink_ref, layer, g, rows, rows_per_head):
    head_in_group = lax.broadcasted_iota(jnp.int32, (rows, 1), 0) // rows_per_head
    col = jnp.full((rows, 1), sink_ref[layer, g * GROUP], F32)
    for r in range(1, GROUP):
        col = jnp.where(head_in_group == r, sink_ref[layer, g * GROUP + r], col)
    return col


def _attn_prompt_kernel(sink_ref, q_ref, lop_ref, loc_ref, hip_ref, hic_ref, vp_ref, vc_ref, wo_ref, r_ref, g_ref,
                        x_ref, xn_ref, wb_ref, *, layer, n_qblk):
    first = pl.program_id(1) == 0

    @pl.when((pl.program_id(0) == 0) & first)
    def _():
        wb_ref[...] = wo_ref[...].astype(BF16)

    qb = q_ref[...]
    klo = jnp.concatenate([lop_ref[...], loc_ref[...]], axis=0)
    khi = jnp.concatenate([hip_ref[...], hic_ref[...]], axis=0)
    v_t = jnp.concatenate([vp_ref[...], vc_ref[...]], axis=0).T.astype(BF16)
    key = lax.broadcasted_iota(jnp.int32, (WINDOW, WINDOW), 0)
    qi = lax.broadcasted_iota(jnp.int32, (WINDOW, WINDOW), 1)
    own = key <= qi
    scores = {}
    for t in range(n_qblk):
        qrows = slice(WINDOW * t, WINDOW * (t + 1))
        krows = slice(WINDOW * t, WINDOW * (t + 2))
        for g in range(N_KV_HEADS):
            gl = slice(128 * g, 128 * (g + 1))
            kk = jnp.concatenate([klo[krows, gl], khi[krows, gl]], axis=0)
            xq = jnp.concatenate([qb[qrows, 256 * g:256 * g + 128], qb[qrows, 256 * g + 128:256 * (g + 1)]],
                                 axis=0)
            scores[t, g] = _dot_nt(kk, xq)
    for t in range(n_qblk):
        qrows = slice(WINDOW * t, WINDOW * (t + 1))
        krows = slice(WINDOW * t, WINDOW * (t + 2))
        pieces = []
        for g in range(N_KV_HEADS):
            s4 = scores[t, g]
            v_g = v_t[HEAD_DIM * g:HEAD_DIM * (g + 1), krows]
            for pair in range(GROUP // 2):
                p2s, invs = [], []
                for parity in range(2):
                    sk = sink_ref[layer, g * GROUP + 2 * pair + parity]
                    blk = s4[2 * WINDOW * parity:2 * WINDOW * (parity + 1), WINDOW * pair:WINDOW * (pair + 1)]
                    s_prev = blk[:WINDOW]
                    if t == 0:
                        s_prev = jnp.where(first, NEG_INF, s_prev)
                    s = jnp.where(own, blk[WINDOW:], s_prev)
                    m = jnp.maximum(jnp.max(s, axis=0, keepdims=True), sk)
                    p = jnp.exp(s - m)
                    invs.append(1.0 / (jnp.sum(p, axis=0, keepdims=True) + jnp.exp(sk - m)))
                    p2s.append(jnp.concatenate([jnp.where(own, 0.0, p), jnp.where(own, p, 0.0)], axis=0))
                p2 = jnp.concatenate(p2s, axis=1).astype(BF16)
                o_t = _dot(v_g, p2) * jnp.concatenate(invs, axis=1)
                pieces += [o_t[:, :WINDOW], o_t[:, WINDOW:]]
        ao = jnp.concatenate(pieces, axis=0).T.astype(BF16)
        x = r_ref[qrows, :] + _dot(ao, wb_ref[...])
        x_ref[qrows, :] = x
        xn_ref[qrows, :] = _rms_normed(x, [g_ref])[0]


def _attn_prompt(q, klo, khi, v, sinks, w_o, res, gain, layer, B, L, name):
    T = res.shape[0]
    g_arr, g_idx = gain
    nb = L // WINDOW
    nq = ATTN_Q_BLOCKS if nb % ATTN_Q_BLOCKS == 0 else 1
    ns = nb // nq
    kw = N_KV_HEADS * HEAD_DIM
    kpad = 128 * N_KV_HEADS
    cur = lambda b, i: b * ns + i
    prev = lambda b, i: b * nb + jnp.maximum(i * nq - 1, 0)
    kern = functools.partial(_attn_prompt_kernel, layer=layer, n_qblk=nq)
    return pl.pallas_call(
        kern,
        grid=(B, ns),
        in_specs=[pl.BlockSpec(memory_space=pltpu.SMEM),
                  pl.BlockSpec((nq * WINDOW, D_MODEL), lambda b, i: (cur(b, i), 0)),
                  pl.BlockSpec((WINDOW, kpad), lambda b, i: (prev(b, i), 0)),
                  pl.BlockSpec((nq * WINDOW, kpad), lambda b, i: (cur(b, i), 0)),
                  pl.BlockSpec((WINDOW, kpad), lambda b, i: (prev(b, i), 0)),
                  pl.BlockSpec((nq * WINDOW, kpad), lambda b, i: (cur(b, i), 0)),
                  pl.BlockSpec((WINDOW, kw), lambda b, i: (prev(b, i), 0)),
                  pl.BlockSpec((nq * WINDOW, kw), lambda b, i: (cur(b, i), 0)),
                  pl.BlockSpec((None, D_MODEL, D_MODEL), lambda b, i: (layer, 0, 0), pipeline_mode=pl.Buffered(1)),
                  pl.BlockSpec((nq * WINDOW, D_MODEL), lambda b, i: (cur(b, i), 0)),
                  pl.BlockSpec((None, 1, D_MODEL), lambda b, i: (g_idx, 0, 0))],
        out_specs=[pl.BlockSpec((nq * WINDOW, D_MODEL), lambda b, i: (cur(b, i), 0))] * 2,
        out_shape=[jax.ShapeDtypeStruct((T, D_MODEL), F32), jax.ShapeDtypeStruct((T, D_MODEL), BF16)],
        scratch_shapes=[pltpu.VMEM((D_MODEL, D_MODEL), BF16)],
        compiler_params=_params(2),
        name=name,
    )(sinks, q, klo, klo, khi, khi, v, v, w_o, res, g_arr)


def _attn_decode_kernel(sink_ref, q_ref, kn_ref, vn_ref, ck_ref, cv_ref, o_ref, *, layer, t_dec):
    nseq = DEC_ROWS // t_dec
    wc = ck_ref.shape[1]
    qb = q_ref[...]
    knew = kn_ref[...].astype(BF16)
    vnew = vn_ref[...].astype(BF16)
    rows = GROUP * DEC_ROWS
    row = lax.broadcasted_iota(jnp.int32, (rows, 1), 0)
    row_seq = (row % DEC_ROWS) // t_dec
    row_tok = row % t_dec
    jold = lax.broadcasted_iota(jnp.int32, (rows, wc), 1)
    vis_old = (jold > row_tok + (wc - WINDOW)) & (jold <= row_tok + wc)
    cnew = lax.broadcasted_iota(jnp.int32, (rows, DEC_ROWS), 1)
    vis_new = (cnew // t_dec == row_seq) & (cnew % t_dec <= row_tok)
    scores = []
    for g in range(N_KV_HEADS):
        heads = [g * GROUP + r for r in range(GROUP)]
        hs = slice(HEAD_DIM * g, HEAD_DIM * (g + 1))
        qs = jnp.concatenate([qb[:, HEAD_DIM * h:HEAD_DIM * (h + 1)] for h in heads], axis=0)
        s_old = jnp.zeros((rows, wc), F32)
        for bi in range(nseq):
            kc = ck_ref[bi][:, hs].astype(BF16)
            s_old = s_old + jnp.where(row_seq == bi, _dot_nt(qs, kc), 0.0)
        scores.append((s_old, _dot_nt(qs, knew[:, hs])))
    for g in range(N_KV_HEADS):
        heads = [g * GROUP + r for r in range(GROUP)]
        hs = slice(HEAD_DIM * g, HEAD_DIM * (g + 1))
        s_old = jnp.where(vis_old, scores[g][0], NEG_INF)
        s_new = jnp.where(vis_new, scores[g][1], NEG_INF)
        sk = _sink_column(sink_ref, layer, g, rows, DEC_ROWS)
        m = jnp.maximum(jnp.maximum(jnp.max(s_old, axis=-1, keepdims=True),
                                    jnp.max(s_new, axis=-1, keepdims=True)), sk)
        p_old = jnp.exp(s_old - m)
        p_new = jnp.exp(s_new - m)
        denom = (jnp.sum(p_old, axis=-1, keepdims=True) + jnp.sum(p_new, axis=-1, keepdims=True)
                 + jnp.exp(sk - m))
        inv = 1.0 / denom
        o = _dot((p_new * inv).astype(BF16), vnew[:, hs])
        pn_old = p_old * inv
        for bi in range(nseq):
            vc = cv_ref[bi][:, hs].astype(BF16)
            o = o + _dot(jnp.where(row_seq == bi, pn_old, 0.0).astype(BF16), vc)
        for r, h in enumerate(heads):
            o_ref[:, HEAD_DIM * h:HEAD_DIM * (h + 1)] = o[DEC_ROWS * r:DEC_ROWS * (r + 1)].astype(o_ref.dtype)


def _attn_decode(q, kn, v, cache_k, cache_v, sinks, layer, TP, DB, t_dec, name):
    nseq = DEC_ROWS // t_dec
    r0 = TP // DEC_ROWS
    kw = N_KV_HEADS * HEAD_DIM
    wc = cache_k.shape[1]
    kern = functools.partial(_attn_decode_kernel, layer=layer, t_dec=t_dec)
    return pl.pallas_call(
        kern,
        grid=(DB // nseq,),
        in_specs=[pl.BlockSpec(memory_space=pltpu.SMEM),
                  pl.BlockSpec((DEC_ROWS, D_MODEL), lambda i: (r0 + i, 0)),
                  pl.BlockSpec((DEC_ROWS, kw), lambda i: (r0 + i, 0)),
                  pl.BlockSpec((DEC_ROWS, kw), lambda i: (r0 + i, 0)),
                  pl.BlockSpec((nseq, wc, kw), lambda i: (i, 0, 0)),
                  pl.BlockSpec((nseq, wc, kw), lambda i: (i, 0, 0))],
        out_specs=pl.BlockSpec((DEC_ROWS, D_MODEL), lambda i: (i, 0)),
        out_shape=jax.ShapeDtypeStruct((DB * t_dec, D_MODEL), BF16),
        compiler_params=_params(1),
        name=name,
    )(sinks, q, kn, v, cache_k, cache_v)


def _rope_cos_sin(pos, half):
    inv = 1.0 / (ROPE_THETA ** (np.arange(half, dtype=np.float64) / half))
    ang = pos.astype(np.float64)[:, None] * inv[None, :]
    return np.cos(ang), np.sin(ang)


def _decay_tables(chunk, reps):
    lg = np.log(1.0 - 2.0 ** (-5.0 - np.arange(RET_HEADS, dtype=np.float64)))
    idx = np.arange(chunk, dtype=np.float64)
    rel = idx[:, None] - idx[None, :]
    dmask = np.where(rel >= 0, np.exp(lg[:, None, None] * np.maximum(rel, 0.0)), 0.0)
    qdec = np.exp(lg[:, None] * (idx + 1.0))[:, :, None]
    kdec = np.exp(lg[:, None] * (chunk - 1.0 - idx))[:, :, None]
    cdec = np.exp(lg * chunk)[:, None, None]
    if reps > 1:
        dmask = np.einsum("ab,hij->haibj", np.eye(reps), dmask).reshape(RET_HEADS, reps * chunk, reps * chunk)
        qdec = np.tile(qdec, (1, reps, 1))
        kdec = np.tile(kdec, (1, reps, 1))
    return dmask, qdec, kdec, cdec


def _tables(B, L, TD, t_dec, chunk_p):
    pos = np.concatenate([np.arange(L), PAST_LEN + np.arange(TD) % t_dec])
    ret_cos, ret_sin = _rope_cos_sin(pos, RET_DK // 2)
    per_row = lambda t: np.concatenate([np.tile(t[:L], (B, 1)), t[L:]], axis=0)
    c, s = _rope_cos_sin(pos, HEAD_DIM // 2)
    att_cos = np.concatenate([c, c, c, c], axis=1)
    att_sin = np.concatenate([-s, s, -s, s], axis=1)
    dmask_p, qdec_p, kdec_p, cdec_p = _decay_tables(chunk_p, 1)
    dmask_d, qdec_d, kdec_d, cdec_d = _decay_tables(math.gcd(t_dec, RET_CHUNK), RET_DEC_ROWS // t_dec)
    head_of_lane = np.arange(256) // HEAD_DIM
    f32 = dict(ret_cos_rows=per_row(ret_cos), ret_sin_rows=per_row(ret_sin), att_cos=att_cos, att_sin=att_sin,
               dmask_p=dmask_p, qdec_p=qdec_p, kdec_p=kdec_p, cdec_p=cdec_p,
               dmask_d=dmask_d, qdec_d=qdec_d, kdec_d=kdec_d, cdec_d=cdec_d)
    tabs = {name: jnp.asarray(t.astype(np.float32)) for name, t in f32.items()}
    tabs["bd"] = jnp.asarray((head_of_lane[:, None] == head_of_lane[None, :]).astype(np.float32), dtype=BF16)
    tabs["chunk_p"] = chunk_p
    return tabs


def kernel(x_prompt, x_sample, state_ret, cache_k, cache_v, ln_ret, w_ret_in, w_ret_out, ln_ffn, w_ffn_in,
           w_ffn_out, ln_kv, w_kv, k_norm, ln_attn, w_q, q_norm, sinks, w_o):
    B, L, D = x_prompt.shape
    DB, t_dec, _ = x_sample.shape
    n_ret = w_ret_in.shape[0]
    n_attn = w_q.shape[0]
    TP, TD = B * L, DB * t_dec
    T = TP + TD
    wc = cache_k.shape[1]
    kw = N_KV_HEADS * HEAD_DIM
    assert D == D_MODEL and L % RET_CHUNK == 0 and L % ROW_BLOCK == 0 and TD % ROW_BLOCK == 0
    assert t_dec == math.gcd(t_dec, RET_CHUNK) and DEC_ROWS % t_dec == 0 and wc == WINDOW and PAST_LEN >= WINDOW

    tabs = _tables(B, L, TD, t_dec, RET_CHUNK_PROMPT if L % RET_CHUNK_PROMPT == 0 else RET_CHUNK)
    ck = cache_k.reshape(DB, wc, kw)
    cv = cache_v.reshape(DB, wc, kw)
    q_norm_w = jnp.tile(q_norm, (1, N_HEADS))[:, None, :]
    k_norm_w = jnp.tile(k_norm[None, :], (1, N_KV_HEADS))
    ln_ret, ln_ffn, ln_attn = ln_ret[:, None, :], ln_ffn[:, None, :], ln_attn[:, None, :]

    ln_kv3 = ln_kv[None, None, :]
    prompt_states = []
    dec_states = None
    x = (x_prompt.reshape(TP, D), x_sample.reshape(TD, D))
    xn = _embed(*x, (ln_ret, 0), "embed")
    for l in range(n_ret):
        go_p, s_p, go_d, dec_states = _ret_layer(xn, w_ret_in, state_ret, l, dec_states, tabs, B, L, DB, t_dec,
                                                 f"retention_{l}")
        prompt_states.append(s_p)
        x, xn = _matmul_residual((go_p, go_d), w_ret_out, l, x, [(ln_ffn, l)], f"ret_out_{l}", TP)
        h = _swiglu(xn, w_ffn_in, l, f"ffn_in_{l}")
        if l + 1 < n_ret:
            x, xn = _matmul_residual(h, w_ffn_out, l, x, [(ln_ret, l + 1)], f"ffn_out_{l}", TP)
        else:
            x, xn_kv, xn = _matmul_residual(h, w_ffn_out, l, x, [(ln_kv3, 0), (ln_attn, 0)], f"ffn_out_{l}", TP)

    kn, v, klo, khi = _kv_proj(xn_kv, w_kv, k_norm_w, tabs, L, TP, "kv_proj")
    for j in range(n_attn):
        layer = n_ret + j
        q = _q_proj(xn, w_q, q_norm_w, j, tabs, L, TP, f"q_proj_{j}")
        x_p, xn_p = _attn_prompt(q, klo, khi, v, sinks, w_o, x, (ln_ffn, layer), j, B, L, f"attn_prompt_{j}")
        ao_d = _attn_decode(q, kn, v, ck, cv, sinks, j, TP, DB, t_dec, f"attn_decode_{j}")
        x, xn = _residual_tail_rows(ao_d, w_o, j, x, (ln_ffn, layer), x_p, xn_p, TP, f"attn_out_{j}")
        h = _swiglu(xn, w_ffn_in, layer, f"ffn_in_{layer}")
        if j + 1 < n_attn:
            x, xn = _matmul_residual(h, w_ffn_out, layer, x, [(ln_attn, j + 1)], f"ffn_out_{layer}", TP)
        else:
            y_p, y_s = _matmul_residual(h, w_ffn_out, layer, x, [], f"ffn_out_{layer}", TP, split_out=True)

    y_prompt = y_p.reshape(B, L, D)
    y_sample = y_s.reshape(DB, t_dec, D)
    state_prompt = jnp.stack(prompt_states)
    w_keep = min(WINDOW, L)
    tail_rows = lambda t: t[:TP].reshape(B, L, kw)[:, L - w_keep:].reshape(B, w_keep, N_KV_HEADS, HEAD_DIM)
    kn_p, v_p = tail_rows(kn), tail_rows(v)
    kn_d = kn[TP:].reshape(DB, t_dec, N_KV_HEADS, HEAD_DIM)
    v_d = v[TP:].reshape(DB, t_dec, N_KV_HEADS, HEAD_DIM)
    assert t_dec <= wc
    cache_k_sample = jnp.concatenate([cache_k[:, t_dec:], kn_d], axis=1)
    cache_v_sample = jnp.concatenate([cache_v[:, t_dec:], v_d], axis=1)
    return (y_prompt, y_sample, state_prompt, dec_states, kn_p, v_p,
            cache_k_sample, cache_v_sample)
```

```python
import functools
import math

import numpy as np
import jax
import jax.numpy as jnp
from jax import lax
from jax.experimental import pallas as pl
from jax.experimental.pallas import tpu as pltpu

D_MODEL = 1024
PAST_LEN = 8192
RET_HEADS = 4
RET_DK = 256
RET_DV = 512
RET_CHUNK = 128
RET_CHUNK_PROMPT = 256
N_HEADS = 16
N_KV_HEADS = 4
HEAD_DIM = 64
GROUP = N_HEADS // N_KV_HEADS
WINDOW = 128
D_FF = 2816
ROPE_THETA = 10000.0
EPS = 1e-6
NEG_INF = -1e30

F32 = jnp.float32
BF16 = jnp.bfloat16

V7X_VMEM_LIMIT_BYTES = 56 * 1024 * 1024
DEC_ROWS = 16
RET_DEC_ROWS = 32
ROW_BLOCK = 512
SUB_ROWS = 256
ATTN_Q_BLOCKS = 8


def _dot(a, b):
    return jnp.dot(a, b, preferred_element_type=F32)


def _dot_nt(a, b):
    return lax.dot_general(a, b, (((1,), (1,)), ((), ())), preferred_element_type=F32)


def _dot_tn(a, b):
    return lax.dot_general(a, b, (((0,), (0,)), ((), ())), preferred_element_type=F32)


def _params(n_axes):
    return pltpu.CompilerParams(dimension_semantics=("arbitrary",) * n_axes,
                                vmem_limit_bytes=V7X_VMEM_LIMIT_BYTES)


def _row_tile(rows, prefs):
    for t in prefs:
        if rows % t == 0:
            return t
    raise ValueError(f"no row tile for {rows}")


def _rms_normed(x, gain_refs):
    ms = jnp.mean(x * x, axis=-1, keepdims=True)
    xh = x * lax.rsqrt(ms + EPS)
    return [(xh * g_ref[...]).astype(BF16) for g_ref in gain_refs]


def _embed_kernel(xp_ref, xs_ref, g_ref, xn_ref, *, n_prompt_blocks):
    i = pl.program_id(0)

    @pl.when(i < n_prompt_blocks)
    def _():
        xn_ref[...] = _rms_normed(xp_ref[...], [g_ref])[0]

    @pl.when(i >= n_prompt_blocks)
    def _():
        xn_ref[...] = _rms_normed(xs_ref[...], [g_ref])[0]


def _embed(xp, xs, gain, name):
    TP, D = xp.shape
    TD = xs.shape[0]
    npb = TP // ROW_BLOCK
    T = TP + TD
    g_arr, g_idx = gain
    blk = lambda fn: pl.BlockSpec((ROW_BLOCK, D), fn)
    return pl.pallas_call(
        functools.partial(_embed_kernel, n_prompt_blocks=npb),
        grid=(T // ROW_BLOCK,),
        in_specs=[blk(lambda i: (jnp.minimum(i, npb - 1), 0)),
                  blk(lambda i: (jnp.maximum(i - npb, 0), 0)),
                  pl.BlockSpec((None, 1, D), lambda i: (g_idx, 0, 0))],
        out_specs=blk(lambda i: (i, 0)),
        out_shape=jax.ShapeDtypeStruct((T, D), BF16),
        compiler_params=_params(1),
        name=name,
    )(xp, xs, g_arr)


def _sub_rows(tm):
    return max(r for r in range(16, 3 * SUB_ROWS + 1, 16) if tm % r == 0)


def _swiglu_kernel(x_ref, wa_ref, wb_ref, o_ref):
    wa = wa_ref[...].astype(BF16)
    wb = wb_ref[...].astype(BF16)
    sub = _sub_rows(x_ref.shape[0])
    for r0 in range(0, x_ref.shape[0], sub):
        rows = slice(r0, r0 + sub)
        a = _dot(x_ref[rows, :], wa)
        b = _dot(x_ref[rows, :], wb)
        o_ref[rows, :] = ((a * jax.nn.sigmoid(a)) * b).astype(o_ref.dtype)


def _swiglu(xn, w, layer, name):
    T, K = xn.shape
    tn = 256
    nb = D_FF // tn
    tm = _row_tile(T, (5632, 2816, 1536, 1024, 512))
    return pl.pallas_call(
        _swiglu_kernel,
        grid=(T // tm, nb),
        in_specs=[pl.BlockSpec((tm, K), lambda i, j: (i, 0)),
                  pl.BlockSpec((None, K, tn), lambda i, j: (layer, 0, j)),
                  pl.BlockSpec((None, K, tn), lambda i, j: (layer, 0, nb + j))],
        out_specs=pl.BlockSpec((tm, tn), lambda i, j: (i, j)),
        out_shape=jax.ShapeDtypeStruct((T, D_FF), BF16),
        compiler_params=_params(2),
        name=name,
    )(xn, w, w)


def _mm_res_kernel(*refs, n_a, n_r, n_gains, n_prompt_blocks, split_out, n_alias=0):
    a_refs = refs[:n_a]
    w_ref = refs[n_a]
    r_refs = refs[n_a + 1:n_a + 1 + n_r]
    rest = refs[n_a + 1 + n_r:]
    gain_refs = rest[:n_gains]
    out_refs = rest[n_gains + n_alias:-1]
    wb_ref = rest[-1]
    i = pl.program_id(0)

    @pl.when(i == 0)
    def _():
        wb_ref[...] = w_ref[...].astype(BF16)

    def step(a_ref, r_ref, dst):
        for r0 in range(0, a_ref.shape[0], SUB_ROWS):
            rows = slice(r0, r0 + SUB_ROWS)
            x = r_ref[rows, :] + _dot(a_ref[rows, :], wb_ref[...])
            out_refs[dst][rows, :] = x
            if not split_out:
                for o_ref, xn in zip(out_refs[1:], _rms_normed(x, gain_refs)):
                    o_ref[rows, :] = xn

    if n_a == 1 and n_r == 1 and not split_out:
        step(a_refs[0], r_refs[0], 0)
    else:
        @pl.when(i < n_prompt_blocks)
        def _():
            step(a_refs[0], r_refs[0], 0)

        @pl.when(i >= n_prompt_blocks)
        def _():
            step(a_refs[-1], r_refs[-1], 1 if split_out else 0)


def _matmul_residual(a, w, layer, res, gains, name, TP, split_out=False):
    a = a if isinstance(a, (tuple, list)) else (a,)
    res = res if isinstance(res, (tuple, list)) else (res,)
    T = sum(r.shape[0] for r in res)
    N = res[0].shape[1]
    K = a[0].shape[1]
    tm = ROW_BLOCK
    npb = TP // tm
    head = lambda i: (jnp.minimum(i, npb - 1), 0)
    tail = lambda i: (jnp.maximum(i - npb, 0), 0)
    whole = lambda i: (i, 0)
    rows_of = lambda arrs, width: [pl.BlockSpec((tm, width), fn)
                                   for fn in ((whole,) if len(arrs) == 1 else (head, tail))]
    in_specs = rows_of(a, K) + [pl.BlockSpec((None, K, N), lambda i: (layer, 0, 0))] + rows_of(res, N)
    in_specs += [pl.BlockSpec((None, 1, N), lambda i, idx=idx: (idx, 0, 0)) for _, idx in gains]
    if split_out:
        assert not gains
        out_specs = [pl.BlockSpec((tm, N), head), pl.BlockSpec((tm, N), tail)]
        out_shape = [jax.ShapeDtypeStruct((TP, N), F32), jax.ShapeDtypeStruct((T - TP, N), F32)]
    else:
        out_specs = [pl.BlockSpec((tm, N), whole)] * (1 + len(gains))
        out_shape = [jax.ShapeDtypeStruct((T, N), F32)] + [jax.ShapeDtypeStruct((T, N), BF16)] * len(gains)
    aliases = {len(a) + 1: 0} if len(res) == 1 and not split_out else {}
    return pl.pallas_call(
        functools.partial(_mm_res_kernel, n_a=len(a), n_r=len(res), n_gains=len(gains), n_prompt_blocks=npb,
                          split_out=split_out),
        grid=(T // tm,),
        in_specs=in_specs,
        out_specs=out_specs,
        out_shape=out_shape,
        scratch_shapes=[pltpu.VMEM((K, N), BF16)],
        input_output_aliases=aliases,
        compiler_params=_params(1),
        name=name,
    )(*a, w, *res, *[g for g, _ in gains])


def _residual_tail_rows(a_tail, w, layer, res, gain, x_buf, xn_buf, TP, name):
    T, N = res.shape
    K = a_tail.shape[1]
    tm = ROW_BLOCK
    npb = TP // tm
    g_arr, g_idx = gain
    tail = pl.BlockSpec((tm, N), lambda i: (npb + i, 0))
    return pl.pallas_call(
        functools.partial(_mm_res_kernel, n_a=1, n_r=1, n_gains=1, n_prompt_blocks=npb, split_out=False,
                          n_alias=2),
        grid=((T - TP) // tm,),
        in_specs=[pl.BlockSpec((tm, K), lambda i: (i, 0)),
                  pl.BlockSpec((None, K, N), lambda i: (layer, 0, 0)),
                  tail,
                  pl.BlockSpec((None, 1, N), lambda i: (g_idx, 0, 0)),
                  pl.BlockSpec(memory_space=pl.ANY), pl.BlockSpec(memory_space=pl.ANY)],
        out_specs=[tail, tail],
        out_shape=[jax.ShapeDtypeStruct((T, N), F32), jax.ShapeDtypeStruct((T, N), BF16)],
        scratch_shapes=[pltpu.VMEM((K, N), BF16)],
        input_output_aliases={4: 0, 5: 1},
        compiler_params=_params(1),
        name=name,
    )(a_tail, w, res, g_arr, x_buf, xn_buf)


def _groupnorm_gate(o, gate):
    mu = jnp.mean(o, axis=-1, keepdims=True)
    d = o - mu
    var = jnp.mean(d * d, axis=-1, keepdims=True)
    on = d * lax.rsqrt(var + EPS)
    return (gate * jax.nn.sigmoid(gate)) * on


def _rope_full_head(x, cos, sin):
    x1, x2 = x[:, :RET_DK // 2], x[:, RET_DK // 2:]
    return jnp.concatenate([x1 * cos - x2 * sin, x2 * cos + x1 * sin], axis=1)


def _ret_layer_kernel(x_ref, xd_ref, wq_ref, wk_ref, wv_ref, wg_ref, cos_ref, sin_ref, cosd_ref, sind_ref,
                      dmask_ref, qdec_ref, kdec_ref, cdec_ref, dmaskd_ref, qdecd_ref, kdecd_ref, cdecd_ref,
                      sd_ref, *rest, chunk_rows, n_chunks, t_dec):
    go_ref, sfin_ref, god_ref, snew_ref, s_ref, w_ref = rest[-6:]
    b, c = pl.program_id(1), pl.program_id(2)
    kcol, vcol, gcol = RET_DK, 2 * RET_DK, 2 * RET_DK + RET_DV

    @pl.when((b == 0) & (c == 0))
    def _():
        w_ref[:, :kcol] = wq_ref[...].astype(BF16)
        w_ref[:, kcol:vcol] = wk_ref[...].astype(BF16)
        w_ref[:, vcol:gcol] = wv_ref[...].astype(BF16)
        w_ref[:, gcol:] = wg_ref[...].astype(BF16)

    @pl.when(c == 0)
    def _():
        s_ref[...] = jnp.zeros_like(s_ref)

    def project(x, cos, sin):
        p = _dot(x, w_ref[...])
        q = _rope_full_head(p[:, :kcol], cos, sin).astype(BF16)
        k = (_rope_full_head(p[:, kcol:vcol], cos, sin) * (RET_DK ** -0.5)).astype(BF16)
        return q, k, p[:, vcol:gcol].astype(BF16), p[:, gcol:]

    dmask = dmask_ref[...]
    qdec = qdec_ref[...]
    kdec = kdec_ref[...]
    cdec = cdec_ref[...]
    def project_chunk(ci):
        rows = slice(ci * chunk_rows, (ci + 1) * chunk_rows)
        x, cos, sin = x_ref[rows, :], cos_ref[rows, :], sin_ref[rows, :]
        if ci == 0:
            x = jnp.concatenate([x, xd_ref[...]], axis=0)
            cos = jnp.concatenate([cos, cosd_ref[...]], axis=0)
            sin = jnp.concatenate([sin, sind_ref[...]], axis=0)
        return project(x, cos, sin)

    proj = project_chunk(0)
    qd, kd, vd, gd = (t[chunk_rows:] for t in proj)
    proj = tuple(t[:chunk_rows] for t in proj)
    for ci in range(n_chunks):
        nxt = project_chunk(ci + 1) if ci + 1 < n_chunks else None
        qb, kb, vb, gate = proj
        s = s_ref[...]
        scores = _dot_nt(qb, kb) * dmask
        inner = _dot(scores.astype(BF16), vb)
        cross = _dot(qb, s.astype(BF16)) * qdec
        s_ref[...] = s * cdec + _dot_tn((kb.astype(F32) * kdec).astype(BF16), vb)
        rows = slice(ci * chunk_rows, (ci + 1) * chunk_rows)
        go_ref[rows, :] = _groupnorm_gate(inner + cross, gate).astype(go_ref.dtype)
        proj = nxt

    @pl.when(c == pl.num_programs(2) - 1)
    def _():
        sfin_ref[...] = s_ref[...]

    cdecd = cdecd_ref[...]
    inner = _dot((_dot_nt(qd, kd) * dmaskd_ref[...]).astype(BF16), vd)
    kdd = kd.astype(F32) * kdecd_ref[...]
    seq_of_row = lax.broadcasted_iota(jnp.int32, (RET_DEC_ROWS, 1), 0) // t_dec
    cross = jnp.zeros((RET_DEC_ROWS, RET_DV), F32)
    for bi in range(RET_DEC_ROWS // t_dec):
        mine = seq_of_row == bi
        s = sd_ref[bi]
        cross = cross + jnp.where(mine, _dot(qd, s.astype(BF16)), 0.0)
        snew_ref[bi] = s * cdecd + _dot_tn(jnp.where(mine, kdd, 0.0).astype(BF16), vd)
    cross = cross * qdecd_ref[...]
    god_ref[...] = _groupnorm_gate(inner + cross, gd).astype(god_ref.dtype)


def _ret_layer(xn, w, state, layer, snew_all, tabs, B, L, DB, t_dec, name):
    TP, TD = B * L, DB * t_dec
    K = xn.shape[1]
    nseq = RET_DEC_ROWS // t_dec
    steps = DB // nseq
    assert DB % nseq == 0 and steps % B == 0, (DB, nseq, B)
    nblk = steps // B
    rb = L // nblk
    ck = tabs["chunk_p"]
    assert L % nblk == 0 and rb % ck == 0, (L, nblk, ck)
    r0 = TP // RET_DEC_ROWS
    blk = lambda h, b, c: b * nblk + c
    wcol = lambda width, col0: pl.BlockSpec((None, K, width), lambda h, b, c: (layer, 0, col0 + h))
    per_head = lambda *shape: pl.BlockSpec((None,) + shape, lambda h, b, c: (h,) + (0,) * len(shape))
    s_spec = pl.BlockSpec((None, nseq, None, RET_DK, RET_DV), lambda h, b, c: (layer, blk(h, b, c), h, 0, 0))
    aliased = [] if snew_all is None else [snew_all]
    kern = functools.partial(_ret_layer_kernel, chunk_rows=ck, n_chunks=rb // ck, t_dec=t_dec)
    return pl.pallas_call(
        kern,
        grid=(RET_HEADS, B, nblk),
        in_specs=[pl.BlockSpec((rb, K), lambda h, b, c: (blk(h, b, c), 0)),
                  pl.BlockSpec((RET_DEC_ROWS, K), lambda h, b, c: (r0 + blk(h, b, c), 0)),
                  wcol(RET_DK, 0), wcol(RET_DK, RET_HEADS), wcol(RET_DV, RET_HEADS), wcol(RET_DV, 2 * RET_HEADS),
                  pl.BlockSpec((rb, RET_DK // 2), lambda h, b, c: (blk(h, b, c), 0)),
                  pl.BlockSpec((rb, RET_DK // 2), lambda h, b, c: (blk(h, b, c), 0)),
                  pl.BlockSpec((RET_DEC_ROWS, RET_DK // 2), lambda h, b, c: (r0 + blk(h, b, c), 0)),
                  pl.BlockSpec((RET_DEC_ROWS, RET_DK // 2), lambda h, b, c: (r0 + blk(h, b, c), 0)),
                  per_head(ck, ck), per_head(ck, 1), per_head(ck, 1), per_head(1, 1),
                  per_head(RET_DEC_ROWS, RET_DEC_ROWS), per_head(RET_DEC_ROWS, 1), per_head(RET_DEC_ROWS, 1),
                  per_head(1, 1), s_spec] + [pl.BlockSpec(memory_space=pl.ANY)] * len(aliased),
        out_specs=[pl.BlockSpec((rb, RET_DV), lambda h, b, c: (blk(h, b, c), h)),
                   pl.BlockSpec((None, None, RET_DK, RET_DV), lambda h, b, c: (b, h, 0, 0)),
                   pl.BlockSpec((RET_DEC_ROWS, RET_DV), lambda h, b, c: (blk(h, b, c), h)),
                   s_spec],
        out_shape=[jax.ShapeDtypeStruct((TP, RET_HEADS * RET_DV), BF16),
                   jax.ShapeDtypeStruct((B, RET_HEADS, RET_DK, RET_DV), F32),
                   jax.ShapeDtypeStruct((TD, RET_HEADS * RET_DV), BF16),
                   jax.ShapeDtypeStruct(state.shape, state.dtype)],
        scratch_shapes=[pltpu.VMEM((RET_DK, RET_DV), F32), pltpu.VMEM((K, 2 * RET_DK + 2 * RET_DV), BF16)],
        input_output_aliases={19: 3} if aliased else {},
        compiler_params=_params(3),
        name=name,
    )(xn, xn, w, w, w, w, tabs["ret_cos_rows"], tabs["ret_sin_rows"], tabs["ret_cos_rows"], tabs["ret_sin_rows"],
      tabs["dmask_p"], tabs["qdec_p"], tabs["kdec_p"], tabs["cdec_p"],
      tabs["dmask_d"], tabs["qdec_d"], tabs["kdec_d"], tabs["cdec_d"], state, *aliased)


def _headnorm_rope(x, w, cos2, sin2, bd):
    width = x.shape[1]
    sq = (x * x).astype(BF16)
    parts = [_dot(sq[:, 256 * c:256 * (c + 1)], bd) for c in range(width // 256)]
    ss = parts[0] if len(parts) == 1 else jnp.concatenate(parts, axis=1)
    y = (x * lax.rsqrt(ss * (1.0 / HEAD_DIM) + EPS)) * w
    lane = lax.broadcasted_iota(jnp.int32, y.shape, 1)
    first_half = (lane & (HEAD_DIM - 1)) < HEAD_DIM // 2
    rot = jnp.where(first_half, pltpu.roll(y, width - HEAD_DIM // 2, 1), pltpu.roll(y, HEAD_DIM // 2, 1))
    nrep = width // 128
    cosw = jnp.concatenate([cos2] * nrep, axis=1)
    sinw = jnp.concatenate([sin2] * nrep, axis=1)
    return y * cosw + rot * sinw


def _kv_proj_kernel(x_ref, wkv_ref, w_ref, cos_ref, sin_ref, bd_ref, o_ref, v_ref, lo_ref, hi_ref, wb_ref):
    @pl.when(pl.program_id(0) == 0)
    def _():
        wb_ref[...] = wkv_ref[...].astype(BF16)

    kw = N_KV_HEADS * HEAD_DIM
    sub = SUB_ROWS
    low = lax.broadcasted_iota(jnp.int32, (sub, 128), 1) < HEAD_DIM
    zero = jnp.zeros((sub, 128), F32)
    for r0 in range(0, x_ref.shape[0], sub):
        rows = slice(r0, r0 + sub)
        kv = _dot(x_ref[rows, :], wb_ref[...])
        v_ref[rows, :] = kv[:, kw:]
        kn = _headnorm_rope(kv[:, :kw], w_ref[...], cos_ref[rows, :], sin_ref[rows, :], bd_ref[...])
        o_ref[rows, :] = kn
        for c in range(N_KV_HEADS // 2):
            tile = kn[:, 128 * c:128 * (c + 1)]
            swapped = pltpu.roll(tile, HEAD_DIM, 1)
            even, odd = slice(256 * c, 256 * c + 128), slice(256 * c + 128, 256 * (c + 1))
            lo_ref[rows, even] = jnp.where(low, tile, zero).astype(lo_ref.dtype)
            hi_ref[rows, even] = jnp.where(low, zero, swapped).astype(hi_ref.dtype)
            lo_ref[rows, odd] = jnp.where(low, swapped, zero).astype(lo_ref.dtype)
            hi_ref[rows, odd] = jnp.where(low, zero, tile).astype(hi_ref.dtype)


def _table_block(i, n_prompt_blocks, blocks_per_seq):
    return jnp.where(i < n_prompt_blocks, i % blocks_per_seq, blocks_per_seq + i - n_prompt_blocks)


def _kv_proj(xn, w_kv, k_norm_w, tabs, L, TP, name):
    T, K = xn.shape
    kw = N_KV_HEADS * HEAD_DIM
    npb, bps = TP // ROW_BLOCK, L // ROW_BLOCK
    tab = lambda i: (_table_block(i, npb, bps), 0)
    kv_spec = pl.BlockSpec((ROW_BLOCK, kw), lambda i: (i, 0))
    pad_spec = pl.BlockSpec((ROW_BLOCK, 128 * N_KV_HEADS), lambda i: (i, 0))
    pad_shape = jax.ShapeDtypeStruct((T, 128 * N_KV_HEADS), BF16)
    return pl.pallas_call(
        _kv_proj_kernel,
        grid=(T // ROW_BLOCK,),
        in_specs=[pl.BlockSpec((ROW_BLOCK, K), lambda i: (i, 0)),
                  pl.BlockSpec((K, 2 * kw), lambda i: (0, 0)),
                  pl.BlockSpec((1, kw), lambda i: (0, 0)),
                  pl.BlockSpec((ROW_BLOCK, 128), tab),
                  pl.BlockSpec((ROW_BLOCK, 128), tab),
                  pl.BlockSpec((256, 256), lambda i: (0, 0))],
        out_specs=[kv_spec, kv_spec, pad_spec, pad_spec],
        out_shape=[jax.ShapeDtypeStruct((T, kw), F32), jax.ShapeDtypeStruct((T, kw), F32), pad_shape, pad_shape],
        scratch_shapes=[pltpu.VMEM((K, 2 * kw), BF16)],
        compiler_params=_params(1),
        name=name,
    )(xn, w_kv, k_norm_w, tabs["att_cos"], tabs["att_sin"], tabs["bd"])


def _q_proj_kernel(x_ref, w_ref, qw_ref, cos_ref, sin_ref, bd_ref, o_ref, wb_ref):
    @pl.when(pl.program_id(0) == 0)
    def _():
        wb_ref[...] = w_ref[...].astype(BF16)

    bd = bd_ref[...]
    for r0 in range(0, x_ref.shape[0], 128):
        rows = slice(r0, r0 + 128)
        q = _dot(x_ref[rows, :], wb_ref[...])
        cos2, sin2 = cos_ref[rows, :], sin_ref[rows, :]
        for c0 in range(0, q.shape[1], 256):
            cols = slice(c0, c0 + 256)
            qc = _headnorm_rope(q[:, cols], qw_ref[:, cols], cos2, sin2, bd)
            o_ref[rows, cols] = (qc * (HEAD_DIM ** -0.5)).astype(o_ref.dtype)


def _q_proj(xn, w, q_norm_w, layer, tabs, L, TP, name):
    T, K = xn.shape
    N = w.shape[-1]
    npb, bps = TP // ROW_BLOCK, L // ROW_BLOCK
    tab = lambda i: (_table_block(i, npb, bps), 0)
    return pl.pallas_call(
        _q_proj_kernel,
        grid=(T // ROW_BLOCK,),
        in_specs=[pl.BlockSpec((ROW_BLOCK, K), lambda i: (i, 0)),
                  pl.BlockSpec((None, K, N), lambda i: (layer, 0, 0)),
                  pl.BlockSpec((None, 1, N), lambda i: (layer, 0, 0)),
                  pl.BlockSpec((ROW_BLOCK, 128), tab),
                  pl.BlockSpec((ROW_BLOCK, 128), tab),
                  pl.BlockSpec((256, 256), lambda i: (0, 0))],
        out_specs=pl.BlockSpec((ROW_BLOCK, N), lambda i: (i, 0)),
        out_shape=jax.ShapeDtypeStruct((T, N), BF16),
        scratch_shapes=[pltpu.VMEM((K, N), BF16)],
        compiler_params=_params(1),
        name=name,
    )(xn, w, q_norm_w, tabs["att_cos"], tabs["att_sin"], tabs["bd"])


def _sink_column(sink_ref, layer, g, rows, rows_per_head):
    head_in_group = lax.broadcasted_iota(jnp.int32, (rows, 1), 0) // rows_per_head
    col = jnp.full((rows, 1), sink_ref[layer, g * GROUP], F32)
    for r in range(1, GROUP):
        col = jnp.where(head_in_group == r, sink_ref[layer, g * GROUP + r], col)
    return col


def _attn_prompt_kernel(sink_ref, q_ref, lop_ref, loc_ref, hip_ref, hic_ref, vp_ref, vc_ref, wo_ref, r_ref, g_ref,
                        x_ref, xn_ref, wb_ref, *, layer, n_qblk):
    first = pl.program_id(1) == 0

    @pl.when((pl.program_id(0) == 0) & first)
    def _():
        wb_ref[...] = wo_ref[...].astype(BF16)

    qb = q_ref[...]
    klo = jnp.concatenate([lop_ref[...], loc_ref[...]], axis=0)
    khi = jnp.concatenate([hip_ref[...], hic_ref[...]], axis=0)
    v_t = jnp.concatenate([vp_ref[...], vc_ref[...]], axis=0).T.astype(BF16)
    key = lax.broadcasted_iota(jnp.int32, (WINDOW, WINDOW), 0)
    qi = lax.broadcasted_iota(jnp.int32, (WINDOW, WINDOW), 1)
    own = key <= qi
    scores = {}
    for t in range(n_qblk):
        qrows = slice(WINDOW * t, WINDOW * (t + 1))
        krows = slice(WINDOW * t, WINDOW * (t + 2))
        for g in range(N_KV_HEADS):
            gl = slice(128 * g, 128 * (g + 1))
            kk = jnp.concatenate([klo[krows, gl], khi[krows, gl]], axis=0)
            xq = jnp.concatenate([qb[qrows, 256 * g:256 * g + 128], qb[qrows, 256 * g + 128:256 * (g + 1)]],
                                 axis=0)
            scores[t, g] = _dot_nt(kk, xq)
    for t in range(n_qblk):
        qrows = slice(WINDOW * t, WINDOW * (t + 1))
        krows = slice(WINDOW * t, WINDOW * (t + 2))
        pieces = []
        for g in range(N_KV_HEADS):
            s4 = scores[t, g]
            v_g = v_t[HEAD_DIM * g:HEAD_DIM * (g + 1), krows]
            for pair in range(GROUP // 2):
                p2s, invs = [], []
                for parity in range(2):
                    sk = sink_ref[layer, g * GROUP + 2 * pair + parity]
                    blk = s4[2 * WINDOW * parity:2 * WINDOW * (parity + 1), WINDOW * pair:WINDOW * (pair + 1)]
                    s_prev = blk[:WINDOW]
                    if t == 0:
                        s_prev = jnp.where(first, NEG_INF, s_prev)
                    s = jnp.where(own, blk[WINDOW:], s_prev)
                    m = jnp.maximum(jnp.max(s, axis=0, keepdims=True), sk)
                    p = jnp.exp(s - m)
                    invs.append(1.0 / (jnp.sum(p, axis=0, keepdims=True) + jnp.exp(sk - m)))
                    p2s.append(jnp.concatenate([jnp.where(own, 0.0, p), jnp.where(own, p, 0.0)], axis=0))
                p2 = jnp.concatenate(p2s, axis=1).astype(BF16)
                o_t = _dot(v_g, p2) * jnp.concatenate(invs, axis=1)
                pieces += [o_t[:, :WINDOW], o_t[:, WINDOW:]]
        ao = jnp.concatenate(pieces, axis=0).T.astype(BF16)
        x = r_ref[qrows, :] + _dot(ao, wb_ref[...])
        x_ref[qrows, :] = x
        xn_ref[qrows, :] = _rms_normed(x, [g_ref])[0]


def _attn_prompt(q, klo, khi, v, sinks, w_o, res, gain, layer, B, L, name):
    T = res.shape[0]
    g_arr, g_idx = gain
    nb = L // WINDOW
    nq = ATTN_Q_BLOCKS if nb % ATTN_Q_BLOCKS == 0 else 1
    ns = nb // nq
    kw = N_KV_HEADS * HEAD_DIM
    kpad = 128 * N_KV_HEADS
    cur = lambda b, i: b * ns + i
    prev = lambda b, i: b * nb + jnp.maximum(i * nq - 1, 0)
    kern = functools.partial(_attn_prompt_kernel, layer=layer, n_qblk=nq)
    return pl.pallas_call(
        kern,
        grid=(B, ns),
        in_specs=[pl.BlockSpec(memory_space=pltpu.SMEM),
                  pl.BlockSpec((nq * WINDOW, D_MODEL), lambda b, i: (cur(b, i), 0)),
                  pl.BlockSpec((WINDOW, kpad), lambda b, i: (prev(b, i), 0)),
                  pl.BlockSpec((nq * WINDOW, kpad), lambda b, i: (cur(b, i), 0)),
                  pl.BlockSpec((WINDOW, kpad), lambda b, i: (prev(b, i), 0)),
                  pl.BlockSpec((nq * WINDOW, kpad), lambda b, i: (cur(b, i), 0)),
                  pl.BlockSpec((WINDOW, kw), lambda b, i: (prev(b, i), 0)),
                  pl.BlockSpec((nq * WINDOW, kw), lambda b, i: (cur(b, i), 0)),
                  pl.BlockSpec((None, D_MODEL, D_MODEL), lambda b, i: (layer, 0, 0), pipeline_mode=pl.Buffered(1)),
                  pl.BlockSpec((nq * WINDOW, D_MODEL), lambda b, i: (cur(b, i), 0)),
                  pl.BlockSpec((None, 1, D_MODEL), lambda b, i: (g_idx, 0, 0))],
        out_specs=[pl.BlockSpec((nq * WINDOW, D_MODEL), lambda b, i: (cur(b, i), 0))] * 2,
        out_shape=[jax.ShapeDtypeStruct((T, D_MODEL), F32), jax.ShapeDtypeStruct((T, D_MODEL), BF16)],
        scratch_shapes=[pltpu.VMEM((D_MODEL, D_MODEL), BF16)],
        compiler_params=_params(2),
        name=name,
    )(sinks, q, klo, klo, khi, khi, v, v, w_o, res, g_arr)


def _attn_decode_kernel(sink_ref, q_ref, kn_ref, vn_ref, ck_ref, cv_ref, o_ref, *, layer, t_dec):
    nseq = DEC_ROWS // t_dec
    wc = ck_ref.shape[1]
    qb = q_ref[...]
    knew = kn_ref[...].astype(BF16)
    vnew = vn_ref[...].astype(BF16)
    rows = GROUP * DEC_ROWS
    row = lax.broadcasted_iota(jnp.int32, (rows, 1), 0)
    row_seq = (row % DEC_ROWS) // t_dec
    row_tok = row % t_dec
    jold = lax.broadcasted_iota(jnp.int32, (rows, wc), 1)
    vis_old = (jold > row_tok + (wc - WINDOW)) & (jold <= row_tok + wc)
    cnew = lax.broadcasted_iota(jnp.int32, (rows, DEC_ROWS), 1)
    vis_new = (cnew // t_dec == row_seq) & (cnew % t_dec <= row_tok)
    scores = []
    for g in range(N_KV_HEADS):
        heads = [g * GROUP + r for r in range(GROUP)]
        hs = slice(HEAD_DIM * g, HEAD_DIM * (g + 1))
        qs = jnp.concatenate([qb[:, HEAD_DIM * h:HEAD_DIM * (h + 1)] for h in heads], axis=0)
        s_old = jnp.zeros((rows, wc), F32)
        for bi in range(nseq):
            kc = ck_ref[bi][:, hs].astype(BF16)
            s_old = s_old + jnp.where(row_seq == bi, _dot_nt(qs, kc), 0.0)
        scores.append((s_old, _dot_nt(qs, knew[:, hs])))
    for g in range(N_KV_HEADS):
        heads = [g * GROUP + r for r in range(GROUP)]
        hs = slice(HEAD_DIM * g, HEAD_DIM * (g + 1))
        s_old = jnp.where(vis_old, scores[g][0], NEG_INF)
        s_new = jnp.where(vis_new, scores[g][1], NEG_INF)
        sk = _sink_column(sink_ref, layer, g, rows, DEC_ROWS)
        m = jnp.maximum(jnp.maximum(jnp.max(s_old, axis=-1, keepdims=True),
                                    jnp.max(s_new, axis=-1, keepdims=True)), sk)
        p_old = jnp.exp(s_old - m)
        p_new = jnp.exp(s_new - m)
        denom = (jnp.sum(p_old, axis=-1, keepdims=True) + jnp.sum(p_new, axis=-1, keepdims=True)
                 + jnp.exp(sk - m))
        inv = 1.0 / denom
        o = _dot((p_new * inv).astype(BF16), vnew[:, hs])
        pn_old = p_old * inv
        for bi in range(nseq):
            vc = cv_ref[bi][:, hs].astype(BF16)
            o = o + _dot(jnp.where(row_seq == bi, pn_old, 0.0).astype(BF16), vc)
        for r, h in enumerate(heads):
            o_ref[:, HEAD_DIM * h:HEAD_DIM * (h + 1)] = o[DEC_ROWS * r:DEC_ROWS * (r + 1)].astype(o_ref.dtype)


def _attn_decode(q, kn, v, cache_k, cache_v, sinks, layer, TP, DB, t_dec, name):
    nseq = DEC_ROWS // t_dec
    r0 = TP // DEC_ROWS
    kw = N_KV_HEADS * HEAD_DIM
    wc = cache_k.shape[1]
    kern = functools.partial(_attn_decode_kernel, layer=layer, t_dec=t_dec)
    return pl.pallas_call(
        kern,
        grid=(DB // nseq,),
        in_specs=[pl.BlockSpec(memory_space=pltpu.SMEM),
                  pl.BlockSpec((DEC_ROWS, D_MODEL), lambda i: (r0 + i, 0)),
                  pl.BlockSpec((DEC_ROWS, kw), lambda i: (r0 + i, 0)),
                  pl.BlockSpec((DEC_ROWS, kw), lambda i: (r0 + i, 0)),
                  pl.BlockSpec((nseq, wc, kw), lambda i: (i, 0, 0)),
                  pl.BlockSpec((nseq, wc, kw), lambda i: (i, 0, 0))],
        out_specs=pl.BlockSpec((DEC_ROWS, D_MODEL), lambda i: (i, 0)),
        out_shape=jax.ShapeDtypeStruct((DB * t_dec, D_MODEL), BF16),
        compiler_params=_params(1),
        name=name,
    )(sinks, q, kn, v, cache_k, cache_v)


def _rope_cos_sin(pos, half):
    inv = 1.0 / (ROPE_THETA ** (np.arange(half, dtype=np.float64) / half))
    ang = pos.astype(np.float64)[:, None] * inv[None, :]
    return np.cos(ang), np.sin(ang)


def _decay_tables(chunk, reps):
    lg = np.log(1.0 - 2.0 ** (-5.0 - np.arange(RET_HEADS, dtype=np.float64)))
    idx = np.arange(chunk, dtype=np.float64)
    rel = idx[:, None] - idx[None, :]
    dmask = np.where(rel >= 0, np.exp(lg[:, None, None] * np.maximum(rel, 0.0)), 0.0)
    qdec = np.exp(lg[:, None] * (idx + 1.0))[:, :, None]
    kdec = np.exp(lg[:, None] * (chunk - 1.0 - idx))[:, :, None]
    cdec = np.exp(lg * chunk)[:, None, None]
    if reps > 1:
        dmask = np.einsum("ab,hij->haibj", np.eye(reps), dmask).reshape(RET_HEADS, reps * chunk, reps * chunk)
        qdec = np.tile(qdec, (1, reps, 1))
        kdec = np.tile(kdec, (1, reps, 1))
    return dmask, qdec, kdec, cdec


def _tables(B, L, TD, t_dec, chunk_p):
    pos = np.concatenate([np.arange(L), PAST_LEN + np.arange(TD) % t_dec])
    ret_cos, ret_sin = _rope_cos_sin(pos, RET_DK // 2)
    per_row = lambda t: np.concatenate([np.tile(t[:L], (B, 1)), t[L:]], axis=0)
    c, s = _rope_cos_sin(pos, HEAD_DIM // 2)
    att_cos = np.concatenate([c, c, c, c], axis=1)
    att_sin = np.concatenate([-s, s, -s, s], axis=1)
    dmask_p, qdec_p, kdec_p, cdec_p = _decay_tables(chunk_p, 1)
    dmask_d, qdec_d, kdec_d, cdec_d = _decay_tables(math.gcd(t_dec, RET_CHUNK), RET_DEC_ROWS // t_dec)
    head_of_lane = np.arange(256) // HEAD_DIM
    f32 = dict(ret_cos_rows=per_row(ret_cos), ret_sin_rows=per_row(ret_sin), att_cos=att_cos, att_sin=att_sin,
               dmask_p=dmask_p, qdec_p=qdec_p, kdec_p=kdec_p, cdec_p=cdec_p,
               dmask_d=dmask_d, qdec_d=qdec_d, kdec_d=kdec_d, cdec_d=cdec_d)
    tabs = {name: jnp.asarray(t.astype(np.float32)) for name, t in f32.items()}
    tabs["bd"] = jnp.asarray((head_of_lane[:, None] == head_of_lane[None, :]).astype(np.float32), dtype=BF16)
    tabs["chunk_p"] = chunk_p
    return tabs


def kernel(x_prompt, x_sample, state_ret, cache_k, cache_v, ln_ret, w_ret_in, w_ret_out, ln_ffn, w_ffn_in,
           w_ffn_out, ln_kv, w_kv, k_norm, ln_attn, w_q, q_norm, sinks, w_o):
    B, L, D = x_prompt.shape
    DB, t_dec, _ = x_sample.shape
    n_ret = w_ret_in.shape[0]
    n_attn = w_q.shape[0]
    TP, TD = B * L, DB * t_dec
    T = TP + TD
    wc = cache_k.shape[1]
    kw = N_KV_HEADS * HEAD_DIM
    assert D == D_MODEL and L % RET_CHUNK == 0 and L % ROW_BLOCK == 0 and TD % ROW_BLOCK == 0
    assert t_dec == math.gcd(t_dec, RET_CHUNK) and DEC_ROWS % t_dec == 0 and wc == WINDOW and PAST_LEN >= WINDOW

    tabs = _tables(B, L, TD, t_dec, RET_CHUNK_PROMPT if L % RET_CHUNK_PROMPT == 0 else RET_CHUNK)
    ck = cache_k.reshape(DB, wc, kw)
    cv = cache_v.reshape(DB, wc, kw)
    q_norm_w = jnp.tile(q_norm, (1, N_HEADS))[:, None, :]
    k_norm_w = jnp.tile(k_norm[None, :], (1, N_KV_HEADS))
    ln_ret, ln_ffn, ln_attn = ln_ret[:, None, :], ln_ffn[:, None, :], ln_attn[:, None, :]

    ln_kv3 = ln_kv[None, None, :]
    prompt_states = []
    dec_states = None
    x = (x_prompt.reshape(TP, D), x_sample.reshape(TD, D))
    xn = _embed(*x, (ln_ret, 0), "embed")
    for l in range(n_ret):
        go_p, s_p, go_d, dec_states = _ret_layer(xn, w_ret_in, state_ret, l, dec_states, tabs, B, L, DB, t_dec,
                                                 f"retention_{l}")
        prompt_states.append(s_p)
        x, xn = _matmul_residual((go_p, go_d), w_ret_out, l, x, [(ln_ffn, l)], f"ret_out_{l}", TP)
        h = _swiglu(xn, w_ffn_in, l, f"ffn_in_{l}")
        if l + 1 < n_ret:
            x, xn = _matmul_residual(h, w_ffn_out, l, x, [(ln_ret, l + 1)], f"ffn_out_{l}", TP)
        else:
            x, xn_kv, xn = _matmul_residual(h, w_ffn_out, l, x, [(ln_kv3, 0), (ln_attn, 0)], f"ffn_out_{l}", TP)

    kn, v, klo, khi = _kv_proj(xn_kv, w_kv, k_norm_w, tabs, L, TP, "kv_proj")
    for j in range(n_attn):
        layer = n_ret + j
        q = _q_proj(xn, w_q, q_norm_w, j, tabs, L, TP, f"q_proj_{j}")
        x_p, xn_p = _attn_prompt(q, klo, khi, v, sinks, w_o, x, (ln_ffn, layer), j, B, L, f"attn_prompt_{j}")
        ao_d = _attn_decode(q, kn, v, ck, cv, sinks, j, TP, DB, t_dec, f"attn_decode_{j}")
        x, xn = _residual_tail_rows(ao_d, w_o, j, x, (ln_ffn, layer), x_p, xn_p, TP, f"attn_out_{j}")
        h = _swiglu(xn, w_ffn_in, layer, f"ffn_in_{layer}")
        if j + 1 < n_attn:
            x, xn = _matmul_residual(h, w_ffn_out, layer, x, [(ln_attn, j + 1)], f"ffn_out_{layer}", TP)
        else:
            y_p, y_s = _matmul_residual(h, w_ffn_out, layer, x, [], f"ffn_out_{layer}", TP, split_out=True)

    y_prompt = y_p.reshape(B, L, D)
    y_sample = y_s.reshape(DB, t_dec, D)
    state_prompt = jnp.stack(prompt_states)
    w_keep = min(WINDOW, L)
    tail_rows = lambda t: jnp.stack([t[(b + 1) * L - w_keep:(b + 1) * L] for b in range(B)]).reshape(
        B, w_keep, N_KV_HEADS, HEAD_DIM)
    kn_p, v_p = tail_rows(kn), tail_rows(v)
    kn_d = kn[TP:].reshape(DB, t_dec, N_KV_HEADS, HEAD_DIM)
    v_d = v[TP:].reshape(DB, t_dec, N_KV_HEADS, HEAD_DIM)
    cache_k_sample = jnp.concatenate([cache_k, kn_d], axis=1)[:, -wc:]
    cache_v_sample = jnp.concatenate([cache_v, v_d], axis=1)[:, -wc:]
    return (y_prompt, y_sample, state_prompt, dec_states, kn_p, v_p,
            cache_k_sample, cache_v_sample)
```

```python
import functools
import math

import numpy as np
import jax
import jax.numpy as jnp
from jax import lax
from jax.experimental import pallas as pl
from jax.experimental.pallas import tpu as pltpu

D_MODEL = 1024
PAST_LEN = 8192
RET_HEADS = 4
RET_DK = 256
RET_DV = 512
RET_CHUNK = 128
RET_CHUNK_PROMPT = 256
N_HEADS = 16
N_KV_HEADS = 4
HEAD_DIM = 64
GROUP = N_HEADS // N_KV_HEADS
WINDOW = 128
D_FF = 2816
ROPE_THETA = 10000.0
EPS = 1e-6
NEG_INF = -1e30

F32 = jnp.float32
BF16 = jnp.bfloat16

V7X_VMEM_LIMIT_BYTES = 56 * 1024 * 1024
DEC_ROWS = 16
RET_DEC_ROWS = 32
ROW_BLOCK = 512
SUB_ROWS = 256
ATTN_Q_BLOCKS = 8


def _dot(a, b):
    return jnp.dot(a, b, preferred_element_type=F32)


def _dot_nt(a, b):
    return lax.dot_general(a, b, (((1,), (1,)), ((), ())), preferred_element_type=F32)


def _dot_tn(a, b):
    return lax.dot_general(a, b, (((0,), (0,)), ((), ())), preferred_element_type=F32)


def _params(n_axes):
    return pltpu.CompilerParams(dimension_semantics=("arbitrary",) * n_axes,
                                vmem_limit_bytes=V7X_VMEM_LIMIT_BYTES)


def _row_tile(rows, prefs):
    for t in prefs:
        if rows % t == 0:
            return t
    raise ValueError(f"no row tile for {rows}")


def _rms_normed(x, gain_refs):
    ms = jnp.mean(x * x, axis=-1, keepdims=True)
    xh = x * lax.rsqrt(ms + EPS)
    return [(xh * g_ref[...]).astype(BF16) for g_ref in gain_refs]


def _embed_kernel(xp_ref, xs_ref, g_ref, xn_ref, *, n_prompt_blocks):
    i = pl.program_id(0)

    @pl.when(i < n_prompt_blocks)
    def _():
        xn_ref[...] = _rms_normed(xp_ref[...], [g_ref])[0]

    @pl.when(i >= n_prompt_blocks)
    def _():
        xn_ref[...] = _rms_normed(xs_ref[...], [g_ref])[0]


def _embed(xp, xs, gain, name):
    TP, D = xp.shape
    TD = xs.shape[0]
    npb = TP // ROW_BLOCK
    T = TP + TD
    g_arr, g_idx = gain
    blk = lambda fn: pl.BlockSpec((ROW_BLOCK, D), fn)
    return pl.pallas_call(
        functools.partial(_embed_kernel, n_prompt_blocks=npb),
        grid=(T // ROW_BLOCK,),
        in_specs=[blk(lambda i: (jnp.minimum(i, npb - 1), 0)),
                  blk(lambda i: (jnp.maximum(i - npb, 0), 0)),
                  pl.BlockSpec((None, 1, D), lambda i: (g_idx, 0, 0))],
        out_specs=blk(lambda i: (i, 0)),
        out_shape=jax.ShapeDtypeStruct((T, D), BF16),
        compiler_params=_params(1),
        name=name,
    )(xp, xs, g_arr)


def _sub_rows(tm):
    return max(r for r in range(16, 3 * SUB_ROWS + 1, 16) if tm % r == 0)


def _swiglu_kernel(x_ref, wa_ref, wb_ref, o_ref):
    wa = wa_ref[...].astype(BF16)
    wb = wb_ref[...].astype(BF16)
    sub = _sub_rows(x_ref.shape[0])
    for r0 in range(0, x_ref.shape[0], sub):
        rows = slice(r0, r0 + sub)
        a = _dot(x_ref[rows, :], wa)
        b = _dot(x_ref[rows, :], wb)
        o_ref[rows, :] = ((a * jax.nn.sigmoid(a)) * b).astype(o_ref.dtype)


def _swiglu(xn, w, layer, name):
    T, K = xn.shape
    tn = 256
    nb = D_FF // tn
    tm = _row_tile(T, (5632, 2816, 1536, 1024, 512))
    return pl.pallas_call(
        _swiglu_kernel,
        grid=(T // tm, nb),
        in_specs=[pl.BlockSpec((tm, K), lambda i, j: (i, 0)),
                  pl.BlockSpec((None, K, tn), lambda i, j: (layer, 0, j)),
                  pl.BlockSpec((None, K, tn), lambda i, j: (layer, 0, nb + j))],
        out_specs=pl.BlockSpec((tm, tn), lambda i, j: (i, j)),
        out_shape=jax.ShapeDtypeStruct((T, D_FF), BF16),
        compiler_params=_params(2),
        name=name,
    )(xn, w, w)


def _mm_res_kernel(*refs, n_a, n_r, n_gains, n_prompt_blocks, split_out, n_alias=0):
    a_refs = refs[:n_a]
    w_ref = refs[n_a]
    r_refs = refs[n_a + 1:n_a + 1 + n_r]
    rest = refs[n_a + 1 + n_r:]
    gain_refs = rest[:n_gains]
    out_refs = rest[n_gains + n_alias:-1]
    wb_ref = rest[-1]
    i = pl.program_id(0)

    @pl.when(i == 0)
    def _():
        wb_ref[...] = w_ref[...].astype(BF16)

    def step(a_ref, r_ref, dst):
        for r0 in range(0, a_ref.shape[0], SUB_ROWS):
            rows = slice(r0, r0 + SUB_ROWS)
            x = r_ref[rows, :] + _dot(a_ref[rows, :], wb_ref[...])
            out_refs[dst][rows, :] = x
            if not split_out:
                for o_ref, xn in zip(out_refs[1:], _rms_normed(x, gain_refs)):
                    o_ref[rows, :] = xn

    if n_a == 1 and n_r == 1 and not split_out:
        step(a_refs[0], r_refs[0], 0)
    else:
        @pl.when(i < n_prompt_blocks)
        def _():
            step(a_refs[0], r_refs[0], 0)

        @pl.when(i >= n_prompt_blocks)
        def _():
            step(a_refs[-1], r_refs[-1], 1 if split_out else 0)


def _matmul_residual(a, w, layer, res, gains, name, TP, split_out=False):
    a = a if isinstance(a, (tuple, list)) else (a,)
    res = res if isinstance(res, (tuple, list)) else (res,)
    T = sum(r.shape[0] for r in res)
    N = res[0].shape[1]
    K = a[0].shape[1]
    tm = ROW_BLOCK
    npb = TP // tm
    head = lambda i: (jnp.minimum(i, npb - 1), 0)
    tail = lambda i: (jnp.maximum(i - npb, 0), 0)
    whole = lambda i: (i, 0)
    rows_of = lambda arrs, width: [pl.BlockSpec((tm, width), fn)
                                   for fn in ((whole,) if len(arrs) == 1 else (head, tail))]
    in_specs = rows_of(a, K) + [pl.BlockSpec((None, K, N), lambda i: (layer, 0, 0))] + rows_of(res, N)
    in_specs += [pl.BlockSpec((None, 1, N), lambda i, idx=idx: (idx, 0, 0)) for _, idx in gains]
    if split_out:
        assert not gains
        out_specs = [pl.BlockSpec((tm, N), head), pl.BlockSpec((tm, N), tail)]
        out_shape = [jax.ShapeDtypeStruct((TP, N), F32), jax.ShapeDtypeStruct((T - TP, N), F32)]
    else:
        out_specs = [pl.BlockSpec((tm, N), whole)] * (1 + len(gains))
        out_shape = [jax.ShapeDtypeStruct((T, N), F32)] + [jax.ShapeDtypeStruct((T, N), BF16)] * len(gains)
    aliases = {len(a) + 1: 0} if len(res) == 1 and not split_out else {}
    return pl.pallas_call(
        functools.partial(_mm_res_kernel, n_a=len(a), n_r=len(res), n_gains=len(gains), n_prompt_blocks=npb,
                          split_out=split_out),
        grid=(T // tm,),
        in_specs=in_specs,
        out_specs=out_specs,
        out_shape=out_shape,
        scratch_shapes=[pltpu.VMEM((K, N), BF16)],
        input_output_aliases=aliases,
        compiler_params=_params(1),
        name=name,
    )(*a, w, *res, *[g for g, _ in gains])


def _residual_tail_rows(a_tail, w, layer, res, gain, x_buf, xn_buf, TP, name):
    T, N = res.shape
    K = a_tail.shape[1]
    tm = ROW_BLOCK
    npb = TP // tm
    g_arr, g_idx = gain
    tail = pl.BlockSpec((tm, N), lambda i: (npb + i, 0))
    return pl.pallas_call(
        functools.partial(_mm_res_kernel, n_a=1, n_r=1, n_gains=1, n_prompt_blocks=npb, split_out=False,
                          n_alias=2),
        grid=((T - TP) // tm,),
        in_specs=[pl.BlockSpec((tm, K), lambda i: (i, 0)),
                  pl.BlockSpec((None, K, N), lambda i: (layer, 0, 0)),
                  tail,
                  pl.BlockSpec((None, 1, N), lambda i: (g_idx, 0, 0)),
                  pl.BlockSpec(memory_space=pl.ANY), pl.BlockSpec(memory_space=pl.ANY)],
        out_specs=[tail, tail],
        out_shape=[jax.ShapeDtypeStruct((T, N), F32), jax.ShapeDtypeStruct((T, N), BF16)],
        scratch_shapes=[pltpu.VMEM((K, N), BF16)],
        input_output_aliases={4: 0, 5: 1},
        compiler_params=_params(1),
        name=name,
    )(a_tail, w, res, g_arr, x_buf, xn_buf)


def _groupnorm_gate(o, gate):
    mu = jnp.mean(o, axis=-1, keepdims=True)
    d = o - mu
    var = jnp.mean(d * d, axis=-1, keepdims=True)
    on = d * lax.rsqrt(var + EPS)
    return (gate * jax.nn.sigmoid(gate)) * on


def _rope_full_head(x, cos, sin):
    x1, x2 = x[:, :RET_DK // 2], x[:, RET_DK // 2:]
    return jnp.concatenate([x1 * cos - x2 * sin, x2 * cos + x1 * sin], axis=1)


def _ret_layer_kernel(x_ref, xd_ref, wq_ref, wk_ref, wv_ref, wg_ref, cos_ref, sin_ref, cosd_ref, sind_ref,
                      dmask_ref, qdec_ref, kdec_ref, cdec_ref, dmaskd_ref, qdecd_ref, kdecd_ref, cdecd_ref,
                      sd_ref, *rest, chunk_rows, n_chunks, t_dec):
    go_ref, sfin_ref, god_ref, snew_ref, s_ref, w_ref = rest[-6:]
    b, c = pl.program_id(1), pl.program_id(2)
    kcol, vcol, gcol = RET_DK, 2 * RET_DK, 2 * RET_DK + RET_DV

    @pl.when((b == 0) & (c == 0))
    def _():
        w_ref[:, :kcol] = wq_ref[...].astype(BF16)
        w_ref[:, kcol:vcol] = wk_ref[...].astype(BF16)
        w_ref[:, vcol:gcol] = wv_ref[...].astype(BF16)
        w_ref[:, gcol:] = wg_ref[...].astype(BF16)

    @pl.when(c == 0)
    def _():
        s_ref[...] = jnp.zeros_like(s_ref)

    def project(x, cos, sin):
        p = _dot(x, w_ref[...])
        q = _rope_full_head(p[:, :kcol], cos, sin).astype(BF16)
        k = (_rope_full_head(p[:, kcol:vcol], cos, sin) * (RET_DK ** -0.5)).astype(BF16)
        return q, k, p[:, vcol:gcol].astype(BF16), p[:, gcol:]

    dmask = dmask_ref[...]
    qdec = qdec_ref[...]
    kdec = kdec_ref[...]
    cdec = cdec_ref[...]
    def project_chunk(ci):
        rows = slice(ci * chunk_rows, (ci + 1) * chunk_rows)
        x, cos, sin = x_ref[rows, :], cos_ref[rows, :], sin_ref[rows, :]
        if ci == 0:
            x = jnp.concatenate([x, xd_ref[...]], axis=0)
            cos = jnp.concatenate([cos, cosd_ref[...]], axis=0)
            sin = jnp.concatenate([sin, sind_ref[...]], axis=0)
        return project(x, cos, sin)

    def decode_sequences(qd, kd, vd, gd):
        cdecd = cdecd_ref[...]
        inner = _dot((_dot_nt(qd, kd) * dmaskd_ref[...]).astype(BF16), vd)
        kdd = kd.astype(F32) * kdecd_ref[...]
        seq_of_row = lax.broadcasted_iota(jnp.int32, (RET_DEC_ROWS, 1), 0) // t_dec
        cross = jnp.zeros((RET_DEC_ROWS, RET_DV), F32)
        for bi in range(RET_DEC_ROWS // t_dec):
            mine = seq_of_row == bi
            s = sd_ref[bi]
            cross = cross + jnp.where(mine, _dot(qd, s.astype(BF16)), 0.0)
            snew_ref[bi] = s * cdecd + _dot_tn(jnp.where(mine, kdd, 0.0).astype(BF16), vd)
        cross = cross * qdecd_ref[...]
        god_ref[...] = _groupnorm_gate(inner + cross, gd).astype(god_ref.dtype)

    proj = project_chunk(0)
    qd, kd, vd, gd = (t[chunk_rows:] for t in proj)
    proj = tuple(t[:chunk_rows] for t in proj)
    for ci in range(n_chunks):
        nxt = project_chunk(ci + 1) if ci + 1 < n_chunks else None
        qb, kb, vb, gate = proj
        s = s_ref[...]
        scores = _dot_nt(qb, kb) * dmask
        inner = _dot(scores.astype(BF16), vb)
        cross = _dot(qb, s.astype(BF16)) * qdec
        s_ref[...] = s * cdec + _dot_tn((kb.astype(F32) * kdec).astype(BF16), vb)
        rows = slice(ci * chunk_rows, (ci + 1) * chunk_rows)
        go_ref[rows, :] = _groupnorm_gate(inner + cross, gate).astype(go_ref.dtype)
        proj = nxt
        if ci == 0:
            decode_sequences(qd, kd, vd, gd)

    @pl.when(c == pl.num_programs(2) - 1)
    def _():
        sfin_ref[...] = s_ref[...]


def _ret_layer(xn, w, state, layer, snew_all, tabs, B, L, DB, t_dec, name):
    TP, TD = B * L, DB * t_dec
    K = xn.shape[1]
    nseq = RET_DEC_ROWS // t_dec
    steps = DB // nseq
    assert DB % nseq == 0 and steps % B == 0, (DB, nseq, B)
    nblk = steps // B
    rb = L // nblk
    ck = tabs["chunk_p"]
    assert L % nblk == 0 and rb % ck == 0, (L, nblk, ck)
    r0 = TP // RET_DEC_ROWS
    blk = lambda h, b, c: b * nblk + c
    wcol = lambda width, col0: pl.BlockSpec((None, K, width), lambda h, b, c: (layer, 0, col0 + h))
    per_head = lambda *shape: pl.BlockSpec((None,) + shape, lambda h, b, c: (h,) + (0,) * len(shape))
    s_spec = pl.BlockSpec((None, nseq, None, RET_DK, RET_DV), lambda h, b, c: (layer, blk(h, b, c), h, 0, 0))
    aliased = [] if snew_all is None else [snew_all]
    kern = functools.partial(_ret_layer_kernel, chunk_rows=ck, n_chunks=rb // ck, t_dec=t_dec)
    return pl.pallas_call(
        kern,
        grid=(RET_HEADS, B, nblk),
        in_specs=[pl.BlockSpec((rb, K), lambda h, b, c: (blk(h, b, c), 0)),
                  pl.BlockSpec((RET_DEC_ROWS, K), lambda h, b, c: (r0 + blk(h, b, c), 0)),
                  wcol(RET_DK, 0), wcol(RET_DK, RET_HEADS), wcol(RET_DV, RET_HEADS), wcol(RET_DV, 2 * RET_HEADS),
                  pl.BlockSpec((rb, RET_DK // 2), lambda h, b, c: (blk(h, b, c), 0)),
                  pl.BlockSpec((rb, RET_DK // 2), lambda h, b, c: (blk(h, b, c), 0)),
                  pl.BlockSpec((RET_DEC_ROWS, RET_DK // 2), lambda h, b, c: (r0 + blk(h, b, c), 0)),
                  pl.BlockSpec((RET_DEC_ROWS, RET_DK // 2), lambda h, b, c: (r0 + blk(h, b, c), 0)),
                  per_head(ck, ck), per_head(ck, 1), per_head(ck, 1), per_head(1, 1),
                  per_head(RET_DEC_ROWS, RET_DEC_ROWS), per_head(RET_DEC_ROWS, 1), per_head(RET_DEC_ROWS, 1),
                  per_head(1, 1), s_spec] + [pl.BlockSpec(memory_space=pl.ANY)] * len(aliased),
        out_specs=[pl.BlockSpec((rb, RET_DV), lambda h, b, c: (blk(h, b, c), h)),
                   pl.BlockSpec((None, None, RET_DK, RET_DV), lambda h, b, c: (b, h, 0, 0)),
                   pl.BlockSpec((RET_DEC_ROWS, RET_DV), lambda h, b, c: (blk(h, b, c), h)),
                   s_spec],
        out_shape=[jax.ShapeDtypeStruct((TP, RET_HEADS * RET_DV), BF16),
                   jax.ShapeDtypeStruct((B, RET_HEADS, RET_DK, RET_DV), F32),
                   jax.ShapeDtypeStruct((TD, RET_HEADS * RET_DV), BF16),
                   jax.ShapeDtypeStruct(state.shape, state.dtype)],
        scratch_shapes=[pltpu.VMEM((RET_DK, RET_DV), F32), pltpu.VMEM((K, 2 * RET_DK + 2 * RET_DV), BF16)],
        input_output_aliases={19: 3} if aliased else {},
        compiler_params=_params(3),
        name=name,
    )(xn, xn, w, w, w, w, tabs["ret_cos_rows"], tabs["ret_sin_rows"], tabs["ret_cos_rows"], tabs["ret_sin_rows"],
      tabs["dmask_p"], tabs["qdec_p"], tabs["kdec_p"], tabs["cdec_p"],
      tabs["dmask_d"], tabs["qdec_d"], tabs["kdec_d"], tabs["cdec_d"], state, *aliased)


def _headnorm_rope(x, w, cos2, sin2, bd):
    width = x.shape[1]
    sq = (x * x).astype(BF16)
    parts = [_dot(sq[:, 256 * c:256 * (c + 1)], bd) for c in range(width // 256)]
    ss = parts[0] if len(parts) == 1 else jnp.concatenate(parts, axis=1)
    y = (x * lax.rsqrt(ss * (1.0 / HEAD_DIM) + EPS)) * w
    lane = lax.broadcasted_iota(jnp.int32, y.shape, 1)
    first_half = (lane & (HEAD_DIM - 1)) < HEAD_DIM // 2
    rot = jnp.where(first_half, pltpu.roll(y, width - HEAD_DIM // 2, 1), pltpu.roll(y, HEAD_DIM // 2, 1))
    nrep = width // 128
    cosw = jnp.concatenate([cos2] * nrep, axis=1)
    sinw = jnp.concatenate([sin2] * nrep, axis=1)
    return y * cosw + rot * sinw


def _kv_proj_kernel(x_ref, wkv_ref, w_ref, cos_ref, sin_ref, bd_ref, o_ref, v_ref, lo_ref, hi_ref, wb_ref):
    @pl.when(pl.program_id(0) == 0)
    def _():
        wb_ref[...] = wkv_ref[...].astype(BF16)

    kw = N_KV_HEADS * HEAD_DIM
    sub = SUB_ROWS
    low = lax.broadcasted_iota(jnp.int32, (sub, 128), 1) < HEAD_DIM
    zero = jnp.zeros((sub, 128), F32)
    for r0 in range(0, x_ref.shape[0], sub):
        rows = slice(r0, r0 + sub)
        kv = _dot(x_ref[rows, :], wb_ref[...])
        v_ref[rows, :] = kv[:, kw:]
        kn = _headnorm_rope(kv[:, :kw], w_ref[...], cos_ref[rows, :], sin_ref[rows, :], bd_ref[...])
        o_ref[rows, :] = kn
        for c in range(N_KV_HEADS // 2):
            tile = kn[:, 128 * c:128 * (c + 1)]
            swapped = pltpu.roll(tile, HEAD_DIM, 1)
            even, odd = slice(256 * c, 256 * c + 128), slice(256 * c + 128, 256 * (c + 1))
            lo_ref[rows, even] = jnp.where(low, tile, zero).astype(lo_ref.dtype)
            hi_ref[rows, even] = jnp.where(low, zero, swapped).astype(hi_ref.dtype)
            lo_ref[rows, odd] = jnp.where(low, swapped, zero).astype(lo_ref.dtype)
            hi_ref[rows, odd] = jnp.where(low, zero, tile).astype(hi_ref.dtype)


def _table_block(i, n_prompt_blocks, blocks_per_seq):
    return jnp.where(i < n_prompt_blocks, i % blocks_per_seq, blocks_per_seq + i - n_prompt_blocks)


def _kv_proj(xn, w_kv, k_norm_w, tabs, L, TP, name):
    T, K = xn.shape
    kw = N_KV_HEADS * HEAD_DIM
    npb, bps = TP // ROW_BLOCK, L // ROW_BLOCK
    tab = lambda i: (_table_block(i, npb, bps), 0)
    kv_spec = pl.BlockSpec((ROW_BLOCK, kw), lambda i: (i, 0))
    pad_spec = pl.BlockSpec((ROW_BLOCK, 128 * N_KV_HEADS), lambda i: (i, 0))
    pad_shape = jax.ShapeDtypeStruct((T, 128 * N_KV_HEADS), BF16)
    return pl.pallas_call(
        _kv_proj_kernel,
        grid=(T // ROW_BLOCK,),
        in_specs=[pl.BlockSpec((ROW_BLOCK, K), lambda i: (i, 0)),
                  pl.BlockSpec((K, 2 * kw), lambda i: (0, 0)),
                  pl.BlockSpec((1, kw), lambda i: (0, 0)),
                  pl.BlockSpec((ROW_BLOCK, 128), tab),
                  pl.BlockSpec((ROW_BLOCK, 128), tab),
                  pl.BlockSpec((256, 256), lambda i: (0, 0))],
        out_specs=[kv_spec, kv_spec, pad_spec, pad_spec],
        out_shape=[jax.ShapeDtypeStruct((T, kw), F32), jax.ShapeDtypeStruct((T, kw), F32), pad_shape, pad_shape],
        scratch_shapes=[pltpu.VMEM((K, 2 * kw), BF16)],
        compiler_params=_params(1),
        name=name,
    )(xn, w_kv, k_norm_w, tabs["att_cos"], tabs["att_sin"], tabs["bd"])


def _q_proj_kernel(x_ref, w_ref, qw_ref, cos_ref, sin_ref, bd_ref, o_ref, wb_ref):
    @pl.when(pl.program_id(0) == 0)
    def _():
        wb_ref[...] = w_ref[...].astype(BF16)

    bd = bd_ref[...]
    for r0 in range(0, x_ref.shape[0], 128):
        rows = slice(r0, r0 + 128)
        q = _dot(x_ref[rows, :], wb_ref[...])
        cos2, sin2 = cos_ref[rows, :], sin_ref[rows, :]
        for c0 in range(0, q.shape[1], 256):
            cols = slice(c0, c0 + 256)
            qc = _headnorm_rope(q[:, cols], qw_ref[:, cols], cos2, sin2, bd)
            o_ref[rows, cols] = (qc * (HEAD_DIM ** -0.5)).astype(o_ref.dtype)


def _q_proj(xn, w, q_norm_w, layer, tabs, L, TP, name):
    T, K = xn.shape
    N = w.shape[-1]
    npb, bps = TP // ROW_BLOCK, L // ROW_BLOCK
    tab = lambda i: (_table_block(i, npb, bps), 0)
    return pl.pallas_call(
        _q_proj_kernel,
        grid=(T // ROW_BLOCK,),
        in_specs=[pl.BlockSpec((ROW_BLOCK, K), lambda i: (i, 0)),
                  pl.BlockSpec((None, K, N), lambda i: (layer, 0, 0)),
                  pl.BlockSpec((None, 1, N), lambda i: (layer, 0, 0)),
                  pl.BlockSpec((ROW_BLOCK, 128), tab),
                  pl.BlockSpec((ROW_BLOCK, 128), tab),
                  pl.BlockSpec((256, 256), lambda i: (0, 0))],
        out_specs=pl.BlockSpec((ROW_BLOCK, N), lambda i: (i, 0)),
        out_shape=jax.ShapeDtypeStruct((T, N), BF16),
        scratch_shapes=[pltpu.VMEM((K, N), BF16)],
        compiler_params=_params(1),
        name=name,
    )(xn, w, q_norm_w, tabs["att_cos"], tabs["att_sin"], tabs["bd"])


def _sink_column(sink_ref, layer, g, rows, rows_per_head):
    head_in_group = lax.broadcasted_iota(jnp.int32, (rows, 1), 0) // rows_per_head
    col = jnp.full((rows, 1), sink_ref[layer, g * GROUP], F32)
    for r in range(1, GROUP):
        col = jnp.where(head_in_group == r, sink_ref[layer, g * GROUP + r], col)
    return col


def _attn_prompt_kernel(sink_ref, q_ref, lop_ref, loc_ref, hip_ref, hic_ref, vp_ref, vc_ref, wo_ref, r_ref, g_ref,
                        x_ref, xn_ref, wb_ref, *, layer, n_qblk):
    first = pl.program_id(1) == 0

    @pl.when((pl.program_id(0) == 0) & first)
    def _():
        wb_ref[...] = wo_ref[...].astype(BF16)

    qb = q_ref[...]
    klo = jnp.concatenate([lop_ref[...], loc_ref[...]], axis=0)
    khi = jnp.concatenate([hip_ref[...], hic_ref[...]], axis=0)
    v_t = jnp.concatenate([vp_ref[...], vc_ref[...]], axis=0).T.astype(BF16)
    key = lax.broadcasted_iota(jnp.int32, (WINDOW, WINDOW), 0)
    qi = lax.broadcasted_iota(jnp.int32, (WINDOW, WINDOW), 1)
    own = key <= qi
    scores = {}
    for t in range(n_qblk):
        qrows = slice(WINDOW * t, WINDOW * (t + 1))
        krows = slice(WINDOW * t, WINDOW * (t + 2))
        for g in range(N_KV_HEADS):
            gl = slice(128 * g, 128 * (g + 1))
            kk = jnp.concatenate([klo[krows, gl], khi[krows, gl]], axis=0)
            xq = jnp.concatenate([qb[qrows, 256 * g:256 * g + 128], qb[qrows, 256 * g + 128:256 * (g + 1)]],
                                 axis=0)
            scores[t, g] = _dot_nt(kk, xq)
    for t in range(n_qblk):
        qrows = slice(WINDOW * t, WINDOW * (t + 1))
        krows = slice(WINDOW * t, WINDOW * (t + 2))
        pieces = []
        for g in range(N_KV_HEADS):
            s4 = scores[t, g]
            v_g = v_t[HEAD_DIM * g:HEAD_DIM * (g + 1), krows]
            for pair in range(GROUP // 2):
                p2s, invs = [], []
                for parity in range(2):
                    sk = sink_ref[layer, g * GROUP + 2 * pair + parity]
                    blk = s4[2 * WINDOW * parity:2 * WINDOW * (parity + 1), WINDOW * pair:WINDOW * (pair + 1)]
                    s_prev = blk[:WINDOW]
                    if t == 0:
                        s_prev = jnp.where(first, NEG_INF, s_prev)
                    s = jnp.where(own, blk[WINDOW:], s_prev)
                    m = jnp.maximum(jnp.max(s, axis=0, keepdims=True), sk)
                    p = jnp.exp(s - m)
                    invs.append(1.0 / (jnp.sum(p, axis=0, keepdims=True) + jnp.exp(sk - m)))
                    p2s.append(jnp.concatenate([jnp.where(own, 0.0, p), jnp.where(own, p, 0.0)], axis=0))
                p2 = jnp.concatenate(p2s, axis=1).astype(BF16)
                o_t = _dot(v_g, p2) * jnp.concatenate(invs, axis=1)
                pieces += [o_t[:, :WINDOW], o_t[:, WINDOW:]]
        ao = jnp.concatenate(pieces, axis=0).T.astype(BF16)
        x = r_ref[qrows, :] + _dot(ao, wb_ref[...])
        x_ref[qrows, :] = x
        xn_ref[qrows, :] = _rms_normed(x, [g_ref])[0]


def _attn_prompt(q, klo, khi, v, sinks, w_o, res, gain, layer, B, L, name):
    T = res.shape[0]
    g_arr, g_idx = gain
    nb = L // WINDOW
    nq = ATTN_Q_BLOCKS if nb % ATTN_Q_BLOCKS == 0 else 1
    ns = nb // nq
    kw = N_KV_HEADS * HEAD_DIM
    kpad = 128 * N_KV_HEADS
    cur = lambda b, i: b * ns + i
    prev = lambda b, i: b * nb + jnp.maximum(i * nq - 1, 0)
    kern = functools.partial(_attn_prompt_kernel, layer=layer, n_qblk=nq)
    return pl.pallas_call(
        kern,
        grid=(B, ns),
        in_specs=[pl.BlockSpec(memory_space=pltpu.SMEM),
                  pl.BlockSpec((nq * WINDOW, D_MODEL), lambda b, i: (cur(b, i), 0)),
                  pl.BlockSpec((WINDOW, kpad), lambda b, i: (prev(b, i), 0)),
                  pl.BlockSpec((nq * WINDOW, kpad), lambda b, i: (cur(b, i), 0)),
                  pl.BlockSpec((WINDOW, kpad), lambda b, i: (prev(b, i), 0)),
                  pl.BlockSpec((nq * WINDOW, kpad), lambda b, i: (cur(b, i), 0)),
                  pl.BlockSpec((WINDOW, kw), lambda b, i: (prev(b, i), 0)),
                  pl.BlockSpec((nq * WINDOW, kw), lambda b, i: (cur(b, i), 0)),
                  pl.BlockSpec((None, D_MODEL, D_MODEL), lambda b, i: (layer, 0, 0), pipeline_mode=pl.Buffered(1)),
                  pl.BlockSpec((nq * WINDOW, D_MODEL), lambda b, i: (cur(b, i), 0)),
                  pl.BlockSpec((None, 1, D_MODEL), lambda b, i: (g_idx, 0, 0))],
        out_specs=[pl.BlockSpec((nq * WINDOW, D_MODEL), lambda b, i: (cur(b, i), 0))] * 2,
        out_shape=[jax.ShapeDtypeStruct((T, D_MODEL), F32), jax.ShapeDtypeStruct((T, D_MODEL), BF16)],
        scratch_shapes=[pltpu.VMEM((D_MODEL, D_MODEL), BF16)],
        compiler_params=_params(2),
        name=name,
    )(sinks, q, klo, klo, khi, khi, v, v, w_o, res, g_arr)


def _attn_decode_kernel(sink_ref, q_ref, kn_ref, vn_ref, ck_ref, cv_ref, o_ref, *, layer, t_dec):
    nseq = DEC_ROWS // t_dec
    wc = ck_ref.shape[1]
    qb = q_ref[...]
    knew = kn_ref[...].astype(BF16)
    vnew = vn_ref[...].astype(BF16)
    rows = GROUP * DEC_ROWS
    row = lax.broadcasted_iota(jnp.int32, (rows, 1), 0)
    row_seq = (row % DEC_ROWS) // t_dec
    row_tok = row % t_dec
    jold = lax.broadcasted_iota(jnp.int32, (rows, wc), 1)
    vis_old = (jold > row_tok + (wc - WINDOW)) & (jold <= row_tok + wc)
    cnew = lax.broadcasted_iota(jnp.int32, (rows, DEC_ROWS), 1)
    vis_new = (cnew // t_dec == row_seq) & (cnew % t_dec <= row_tok)
    scores = []
    for g in range(N_KV_HEADS):
        heads = [g * GROUP + r for r in range(GROUP)]
        hs = slice(HEAD_DIM * g, HEAD_DIM * (g + 1))
        qs = jnp.concatenate([qb[:, HEAD_DIM * h:HEAD_DIM * (h + 1)] for h in heads], axis=0)
        s_old = jnp.zeros((rows, wc), F32)
        for bi in range(nseq):
            kc = ck_ref[bi][:, hs].astype(BF16)
            s_old = s_old + jnp.where(row_seq == bi, _dot_nt(qs, kc), 0.0)
        scores.append((s_old, _dot_nt(qs, knew[:, hs])))
    for g in range(N_KV_HEADS):
        heads = [g * GROUP + r for r in range(GROUP)]
        hs = slice(HEAD_DIM * g, HEAD_DIM * (g + 1))
        s_old = jnp.where(vis_old, scores[g][0], NEG_INF)
        s_new = jnp.where(vis_new, scores[g][1], NEG_INF)
        sk = _sink_column(sink_ref, layer, g, rows, DEC_ROWS)
        m = jnp.maximum(jnp.maximum(jnp.max(s_old, axis=-1, keepdims=True),
                                    jnp.max(s_new, axis=-1, keepdims=True)), sk)
        p_old = jnp.exp(s_old - m)
        p_new = jnp.exp(s_new - m)
        denom = (jnp.sum(p_old, axis=-1, keepdims=True) + jnp.sum(p_new, axis=-1, keepdims=True)
                 + jnp.exp(sk - m))
        inv = 1.0 / denom
        o = _dot((p_new * inv).astype(BF16), vnew[:, hs])
        pn_old = p_old * inv
        for bi in range(nseq):
            vc = cv_ref[bi][:, hs].astype(BF16)
            o = o + _dot(jnp.where(row_seq == bi, pn_old, 0.0).astype(BF16), vc)
        for r, h in enumerate(heads):
            o_ref[:, HEAD_DIM * h:HEAD_DIM * (h + 1)] = o[DEC_ROWS * r:DEC_ROWS * (r + 1)].astype(o_ref.dtype)


def _attn_decode(q, kn, v, cache_k, cache_v, sinks, layer, TP, DB, t_dec, name):
    nseq = DEC_ROWS // t_dec
    r0 = TP // DEC_ROWS
    kw = N_KV_HEADS * HEAD_DIM
    wc = cache_k.shape[1]
    kern = functools.partial(_attn_decode_kernel, layer=layer, t_dec=t_dec)
    return pl.pallas_call(
        kern,
        grid=(DB // nseq,),
        in_specs=[pl.BlockSpec(memory_space=pltpu.SMEM),
                  pl.BlockSpec((DEC_ROWS, D_MODEL), lambda i: (r0 + i, 0)),
                  pl.BlockSpec((DEC_ROWS, kw), lambda i: (r0 + i, 0)),
                  pl.BlockSpec((DEC_ROWS, kw), lambda i: (r0 + i, 0)),
                  pl.BlockSpec((nseq, wc, kw), lambda i: (i, 0, 0)),
                  pl.BlockSpec((nseq, wc, kw), lambda i: (i, 0, 0))],
        out_specs=pl.BlockSpec((DEC_ROWS, D_MODEL), lambda i: (i, 0)),
        out_shape=jax.ShapeDtypeStruct((DB * t_dec, D_MODEL), BF16),
        compiler_params=_params(1),
        name=name,
    )(sinks, q, kn, v, cache_k, cache_v)


def _rope_cos_sin(pos, half):
    inv = 1.0 / (ROPE_THETA ** (np.arange(half, dtype=np.float64) / half))
    ang = pos.astype(np.float64)[:, None] * inv[None, :]
    return np.cos(ang), np.sin(ang)


def _decay_tables(chunk, reps):
    lg = np.log(1.0 - 2.0 ** (-5.0 - np.arange(RET_HEADS, dtype=np.float64)))
    idx = np.arange(chunk, dtype=np.float64)
    rel = idx[:, None] - idx[None, :]
    dmask = np.where(rel >= 0, np.exp(lg[:, None, None] * np.maximum(rel, 0.0)), 0.0)
    qdec = np.exp(lg[:, None] * (idx + 1.0))[:, :, None]
    kdec = np.exp(lg[:, None] * (chunk - 1.0 - idx))[:, :, None]
    cdec = np.exp(lg * chunk)[:, None, None]
    if reps > 1:
        dmask = np.einsum("ab,hij->haibj", np.eye(reps), dmask).reshape(RET_HEADS, reps * chunk, reps * chunk)
        qdec = np.tile(qdec, (1, reps, 1))
        kdec = np.tile(kdec, (1, reps, 1))
    return dmask, qdec, kdec, cdec


def _tables(B, L, TD, t_dec, chunk_p):
    pos = np.concatenate([np.arange(L), PAST_LEN + np.arange(TD) % t_dec])
    ret_cos, ret_sin = _rope_cos_sin(pos, RET_DK // 2)
    per_row = lambda t: np.concatenate([np.tile(t[:L], (B, 1)), t[L:]], axis=0)
    c, s = _rope_cos_sin(pos, HEAD_DIM // 2)
    att_cos = np.concatenate([c, c, c, c], axis=1)
    att_sin = np.concatenate([-s, s, -s, s], axis=1)
    dmask_p, qdec_p, kdec_p, cdec_p = _decay_tables(chunk_p, 1)
    dmask_d, qdec_d, kdec_d, cdec_d = _decay_tables(math.gcd(t_dec, RET_CHUNK), RET_DEC_ROWS // t_dec)
    head_of_lane = np.arange(256) // HEAD_DIM
    f32 = dict(ret_cos_rows=per_row(ret_cos), ret_sin_rows=per_row(ret_sin), att_cos=att_cos, att_sin=att_sin,
               dmask_p=dmask_p, qdec_p=qdec_p, kdec_p=kdec_p, cdec_p=cdec_p,
               dmask_d=dmask_d, qdec_d=qdec_d, kdec_d=kdec_d, cdec_d=cdec_d)
    tabs = {name: jnp.asarray(t.astype(np.float32)) for name, t in f32.items()}
    tabs["bd"] = jnp.asarray((head_of_lane[:, None] == head_of_lane[None, :]).astype(np.float32), dtype=BF16)
    tabs["chunk_p"] = chunk_p
    return tabs


def kernel(x_prompt, x_sample, state_ret, cache_k, cache_v, ln_ret, w_ret_in, w_ret_out, ln_ffn, w_ffn_in,
           w_ffn_out, ln_kv, w_kv, k_norm, ln_attn, w_q, q_norm, sinks, w_o):
    B, L, D = x_prompt.shape
    DB, t_dec, _ = x_sample.shape
    n_ret = w_ret_in.shape[0]
    n_attn = w_q.shape[0]
    TP, TD = B * L, DB * t_dec
    T = TP + TD
    wc = cache_k.shape[1]
    kw = N_KV_HEADS * HEAD_DIM
    assert D == D_MODEL and L % RET_CHUNK == 0 and L % ROW_BLOCK == 0 and TD % ROW_BLOCK == 0
    assert t_dec == math.gcd(t_dec, RET_CHUNK) and DEC_ROWS % t_dec == 0 and wc == WINDOW and PAST_LEN >= WINDOW

    tabs = _tables(B, L, TD, t_dec, RET_CHUNK_PROMPT if L % RET_CHUNK_PROMPT == 0 else RET_CHUNK)
    ck = cache_k.reshape(DB, wc, kw)
    cv = cache_v.reshape(DB, wc, kw)
    q_norm_w = jnp.tile(q_norm, (1, N_HEADS))[:, None, :]
    k_norm_w = jnp.tile(k_norm[None, :], (1, N_KV_HEADS))
    ln_ret, ln_ffn, ln_attn = ln_ret[:, None, :], ln_ffn[:, None, :], ln_attn[:, None, :]

    ln_kv3 = ln_kv[None, None, :]
    prompt_states = []
    dec_states = None
    x = (x_prompt.reshape(TP, D), x_sample.reshape(TD, D))
    xn = _embed(*x, (ln_ret, 0), "embed")
    for l in range(n_ret):
        go_p, s_p, go_d, dec_states = _ret_layer(xn, w_ret_in, state_ret, l, dec_states, tabs, B, L, DB, t_dec,
                                                 f"retention_{l}")
        prompt_states.append(s_p)
        x, xn = _matmul_residual((go_p, go_d), w_ret_out, l, x, [(ln_ffn, l)], f"ret_out_{l}", TP)
        h = _swiglu(xn, w_ffn_in, l, f"ffn_in_{l}")
        if l + 1 < n_ret:
            x, xn = _matmul_residual(h, w_ffn_out, l, x, [(ln_ret, l + 1)], f"ffn_out_{l}", TP)
        else:
            x, xn_kv, xn = _matmul_residual(h, w_ffn_out, l, x, [(ln_kv3, 0), (ln_attn, 0)], f"ffn_out_{l}", TP)

    kn, v, klo, khi = _kv_proj(xn_kv, w_kv, k_norm_w, tabs, L, TP, "kv_proj")
    for j in range(n_attn):
        layer = n_ret + j
        q = _q_proj(xn, w_q, q_norm_w, j, tabs, L, TP, f"q_proj_{j}")
        x_p, xn_p = _attn_prompt(q, klo, khi, v, sinks, w_o, x, (ln_ffn, layer), j, B, L, f"attn_prompt_{j}")
        ao_d = _attn_decode(q, kn, v, ck, cv, sinks, j, TP, DB, t_dec, f"attn_decode_{j}")
        x, xn = _residual_tail_rows(ao_d, w_o, j, x, (ln_ffn, layer), x_p, xn_p, TP, f"attn_out_{j}")
        h = _swiglu(xn, w_ffn_in, layer, f"ffn_in_{layer}")
        if j + 1 < n_attn:
            x, xn = _matmul_residual(h, w_ffn_out, layer, x, [(ln_attn, j + 1)], f"ffn_out_{layer}", TP)
        else:
            y_p, y_s = _matmul_residual(h, w_ffn_out, layer, x, [], f"ffn_out_{layer}", TP, split_out=True)

    y_prompt = y_p.reshape(B, L, D)
    y_sample = y_s.reshape(DB, t_dec, D)
    state_prompt = jnp.stack(prompt_states)
    w_keep = min(WINDOW, L)
    tail_rows = lambda t: jnp.stack([t[(b + 1) * L - w_keep:(b + 1) * L] for b in range(B)]).reshape(
        B, w_keep, N_KV_HEADS, HEAD_DIM)
    kn_p, v_p = tail_rows(kn), tail_rows(v)
    kn_d = kn[TP:].reshape(DB, t_dec, N_KV_HEADS, HEAD_DIM)
    v_d = v[TP:].reshape(DB, t_dec, N_KV_HEADS, HEAD_DIM)
    cache_k_sample = jnp.concatenate([cache_k, kn_d], axis=1)[:, -wc:]
    cache_v_sample = jnp.concatenate([cache_v, v_d], axis=1)[:, -wc:]
    return (y_prompt, y_sample, state_prompt, dec_states, kn_p, v_p,
            cache_k_sample, cache_v_sample)
```
